```python
import math
import jax, jax.numpy as jnp
from jax import lax
import numpy as np

D_MODEL = 2048
BATCH = 4
SEQ = 2048
DEPTH = 2
DEC_BATCH = 128
DEC_SEQ = 8
PAST_LEN = 16384
PAGE_SIZE = 128

SGU_WIDTH = D_MODEL // 2
SGU_GROUPS = 8
SGU_GROUP_DIM = SGU_WIDTH // SGU_GROUPS
SGU_CHUNK = 128
RET_WIDTH = D_MODEL // 2
RET_HEADS = 8
RET_DK = RET_WIDTH // RET_HEADS
RET_DV = RET_WIDTH // RET_HEADS
RET_CHUNK = 128
ROPE_BASE = 10000.0
D_FF = 11 * D_MODEL // 4
CONV_W = 3
EPS = 1e-6
IN_SIZES = (SGU_WIDTH, SGU_WIDTH, RET_HEADS * RET_DK, RET_HEADS * RET_DK, RET_WIDTH, RET_WIDTH, D_MODEL, D_MODEL)
IN_COLS = SGU_WIDTH * 2 + RET_HEADS * RET_DK * 2 + RET_WIDTH * 2 + D_MODEL * 2

kernel_name = "hybrid_sgu_retention_convffn_decode_step"


def rms_norm(x, g):
    xf = x.astype(jnp.float32)
    y = xf * lax.rsqrt(jnp.mean(xf * xf, axis=-1, keepdims=True) + EPS)
    return (y * g.astype(jnp.float32)).astype(x.dtype)


def layer_norm(x, g, b):
    xf = x.astype(jnp.float32)
    mu = jnp.mean(xf, axis=-1, keepdims=True)
    var = jnp.mean(jnp.square(xf - mu), axis=-1, keepdims=True)
    y = (xf - mu) * lax.rsqrt(var + EPS)
    return (y * g.astype(jnp.float32) + b.astype(jnp.float32)).astype(x.dtype)


def rope(x, pos):
    half = x.shape[-1] // 2
    inv = ROPE_BASE ** (-jnp.arange(half, dtype=jnp.float32) / half)
    ang = pos.astype(jnp.float32)[:, None] * inv[None, :]
    cos = jnp.cos(ang)[None, :, None, :]
    sin = jnp.sin(ang)[None, :, None, :]
    xf = x.astype(jnp.float32)
    x1, x2 = xf[..., :half], xf[..., half:]
    return jnp.concatenate([x1 * cos - x2 * sin, x1 * sin + x2 * cos], axis=-1)


def retention_chunkwise(q, k, v, s0):
    B, L, H, dk = q.shape
    dv = v.shape[-1]
    C = math.gcd(L, RET_CHUNK)
    n = L // C
    log_g = jnp.log(1.0 - jnp.exp2(-5.0 - jnp.arange(H, dtype=jnp.float32)))
    idx = jnp.arange(C, dtype=jnp.float32)
    diff = idx[:, None] - idx[None, :]
    causal = diff >= 0
    decay = jnp.where(causal[None], jnp.exp(log_g[:, None, None] * jnp.where(causal, diff, 0.0)[None]), 0.0)
    q_dec = jnp.exp(log_g[:, None] * (idx[None, :] + 1.0))
    k_dec = jnp.exp(log_g[:, None] * (C - 1.0 - idx[None, :]))
    chunk_dec = jnp.exp(log_g * C)

    def to_chunks(t):
        return t.reshape(B, n, C, H, t.shape[-1]).transpose(1, 0, 3, 2, 4)

    def step(s, inp):
        qi, ki, vi = inp
        scores = jnp.einsum('bhid,bhjd->bhij', qi, ki) * decay[None]
        inner = jnp.einsum('bhij,bhjv->bhiv', scores, vi)
        cross = jnp.einsum('bhid,bhdv->bhiv', qi, s) * q_dec[None, :, :, None]
        s_new = s * chunk_dec[None, :, None, None] + jnp.einsum('bhjd,bhjv->bhdv', ki * k_dec[None, :, :, None], vi)
        return s_new, inner + cross

    s_fin, out = lax.scan(step, s0, (to_chunks(q), to_chunks(k), to_chunks(v)))
    out = out.transpose(1, 0, 3, 2, 4).reshape(B, L, H, dv)
    return out, s_fin


def head_norm(o, g):
    mu = jnp.mean(o, axis=-1, keepdims=True)
    var = jnp.mean(jnp.square(o - mu), axis=-1, keepdims=True)
    y = ((o - mu) * lax.rsqrt(var + EPS)).reshape(o.shape[0], o.shape[1], -1)
    return y * g.astype(jnp.float32)


def token_mixer(h, s0, pos0, w_in, w_s, b_s, sgu_ln_g, sgu_ln_b, ret_gn_g, w_branch_a, w_branch_b, w_out):
    B, L, _ = h.shape
    proj = h @ w_in
    offs = np.cumsum((0,) + IN_SIZES)
    u, v, q, k, vr, gr, ga, gb = [proj[..., int(offs[i]):int(offs[i + 1])] for i in range(len(IN_SIZES))]

    u = jax.nn.gelu(u)
    v = layer_norm(jax.nn.gelu(v), sgu_ln_g, sgu_ln_b)
    C = min(SGU_CHUNK, L)
    n = L // C
    w_sp = jnp.where(jnp.tril(jnp.ones((C, C), dtype=bool))[None], w_s[:, :C, :C], 0.0).astype(v.dtype)
    vg = v.reshape(B, n, C, SGU_GROUPS, SGU_GROUP_DIM)
    z = jnp.einsum('gts,bnsgd->bntgd', w_sp, vg) + b_s[:, :C].T[None, None, :, :, None]
    y_a = u * z.reshape(B, L, SGU_WIDTH)

    pos = pos0 + jnp.arange(L, dtype=jnp.int32)
    qr = rope(q.reshape(B, L, RET_HEADS, RET_DK), pos)
    kr = rope(k.reshape(B, L, RET_HEADS, RET_DK), pos) * (RET_DK ** -0.5)
    vv = vr.reshape(B, L, RET_HEADS, RET_DV).astype(jnp.float32)
    o, s_fin = retention_chunkwise(qr, kr, vv, s0.astype(jnp.float32))
    y_b = jax.nn.silu(gr) * head_norm(o, ret_gn_g).astype(h.dtype)

    merged = jax.nn.sigmoid(ga) * (y_a @ w_branch_a) + jax.nn.sigmoid(gb) * (y_b @ w_branch_b)
    return merged @ w_out, s_fin.astype(s0.dtype), v


def conv_ffn(h, conv_prev, w_gate, w_up, conv_w, conv_b, w_down):
    L = h.shape[1]
    g = h @ w_gate
    up = h @ w_up
    gp = jnp.concatenate([conv_prev.astype(g.dtype), g], axis=1)
    conv = sum(gp[:, i:i + L] * conv_w[i] for i in range(CONV_W)) + conv_b
    y = jax.nn.gelu(conv) * up
    return y @ w_down, gp[:, L:]


def decoder_layer(x, c, pos0, s_ret, conv_prev, w_ada, b_ada, norm_pre1, norm_post1, norm_pre2, norm_post2,
                  w_in, sgu_w_s, sgu_b_s, sgu_ln_g, sgu_ln_b, ret_gn_g, w_branch_a, w_branch_b, w_out,
                  ffn_w_gate, ffn_w_up, ffn_conv_w, ffn_conv_b, ffn_w_down):
    mod = (jax.nn.silu(c) @ w_ada + b_ada)[:, None, :]
    sh1, sc1, g1, sh2, sc2, g2 = jnp.split(mod, 6, axis=-1)
    h = rms_norm(x, norm_pre1) * (1.0 + sc1) + sh1
    t, s_new, v_rows = token_mixer(h, s_ret, pos0, w_in, sgu_w_s, sgu_b_s, sgu_ln_g, sgu_ln_b, ret_gn_g,
                                   w_branch_a, w_branch_b, w_out)
    x = x + g1 * rms_norm(t, norm_post1)
    h2 = rms_norm(x, norm_pre2) * (1.0 + sc2) + sh2
    f, conv_new = conv_ffn(h2, conv_prev, ffn_w_gate, ffn_w_up, ffn_conv_w, ffn_conv_b, ffn_w_down)
    x = x + g2 * rms_norm(f, norm_post2)
    return x, s_new, conv_new, v_rows


def setup_inputs(seed: int = 0) -> dict:
    key = jax.random.key(seed)
    ks = jax.random.split(key, 32)
    f32 = jnp.float32
    nrm = lambda k, shape, s: jax.random.normal(k, shape, f32) * s
    return {
        "x_prompt": nrm(ks[0], (BATCH, SEQ, D_MODEL), 1.0),
        "x_sample": nrm(ks[1], (DEC_BATCH, DEC_SEQ, D_MODEL), 1.0),
        "state_ret": nrm(ks[2], (DEPTH, DEC_BATCH, RET_HEADS, RET_DK, RET_DV), 0.5),
        "state_conv": nrm(ks[3], (DEPTH, DEC_BATCH, CONV_W - 1, D_FF), 1.0),
        "c_prompt": nrm(ks[4], (BATCH, D_MODEL), 1.0),
        "c_sample": nrm(ks[5], (DEC_BATCH, D_MODEL), 1.0),
        "w_ada": nrm(ks[6], (DEPTH, D_MODEL, 6 * D_MODEL), 0.5 * D_MODEL ** -0.5),
        "b_ada": nrm(ks[7], (DEPTH, 6 * D_MODEL), 0.01),
        "norm_pre1": 1.0 + nrm(ks[8], (DEPTH, D_MODEL), 0.05),
        "norm_post1": 1.0 + nrm(ks[9], (DEPTH, D_MODEL), 0.05),
        "norm_pre2": 1.0 + nrm(ks[10], (DEPTH, D_MODEL), 0.05),
        "norm_post2": 1.0 + nrm(ks[11], (DEPTH, D_MODEL), 0.05),
        "w_in": nrm(ks[12], (DEPTH, D_MODEL, IN_COLS), D_MODEL ** -0.5),
        "sgu_w_s": nrm(ks[13], (DEPTH, SGU_GROUPS, SGU_CHUNK, SGU_CHUNK), SGU_CHUNK ** -0.5),
        "sgu_b_s": 1.0 + nrm(ks[14], (DEPTH, SGU_GROUPS, SGU_CHUNK), 0.05),
        "sgu_ln_g": 1.0 + nrm(ks[15], (DEPTH, SGU_WIDTH), 0.05),
        "sgu_ln_b": nrm(ks[16], (DEPTH, SGU_WIDTH), 0.01),
        "ret_gn_g": 1.0 + nrm(ks[17], (DEPTH, RET_WIDTH), 0.05),
        "w_branch_a": nrm(ks[18], (DEPTH, SGU_WIDTH, D_MODEL), SGU_WIDTH ** -0.5),
        "w_branch_b": nrm(ks[19], (DEPTH, RET_WIDTH, D_MODEL), RET_WIDTH ** -0.5),
        "w_out": nrm(ks[20], (DEPTH, D_MODEL, D_MODEL), D_MODEL ** -0.5),
        "ffn_w_gate": nrm(ks[21], (DEPTH, D_MODEL, D_FF), D_MODEL ** -0.5),
        "ffn_w_up": nrm(ks[22], (DEPTH, D_MODEL, D_FF), D_MODEL ** -0.5),
        "ffn_conv_w": nrm(ks[23], (DEPTH, CONV_W, D_FF), CONV_W ** -0.5),
        "ffn_conv_b": nrm(ks[24], (DEPTH, D_FF), 0.01),
        "ffn_w_down": nrm(ks[25], (DEPTH, D_FF, D_MODEL), D_FF ** -0.5),
    }


def reference(x_prompt, x_sample, state_ret, state_conv, c_prompt, c_sample, w_ada, b_ada,
              norm_pre1, norm_post1, norm_pre2, norm_post2, w_in, sgu_w_s, sgu_b_s, sgu_ln_g, sgu_ln_b,
              ret_gn_g, w_branch_a, w_branch_b, w_out, ffn_w_gate, ffn_w_up, ffn_conv_w, ffn_conv_b, ffn_w_down):
    xp, xs = x_prompt, x_sample
    nb = xp.shape[0]
    ret_p, ret_s, conv_p, conv_s, v_s = [], [], [], [], []
    for l in range(DEPTH):
        lw = (w_ada[l], b_ada[l], norm_pre1[l], norm_post1[l], norm_pre2[l], norm_post2[l], w_in[l],
              sgu_w_s[l], sgu_b_s[l], sgu_ln_g[l], sgu_ln_b[l], ret_gn_g[l], w_branch_a[l], w_branch_b[l],
              w_out[l], ffn_w_gate[l], ffn_w_up[l], ffn_conv_w[l], ffn_conv_b[l], ffn_w_down[l])
        s0_p = jnp.zeros((nb, RET_HEADS, RET_DK, RET_DV), xp.dtype)
        c0_p = jnp.zeros((nb, CONV_W - 1, D_FF), xp.dtype)
        xp, sp, cp, _ = decoder_layer(xp, c_prompt, 0, s0_p, c0_p, *lw)
        xs, ss, cs, vs = decoder_layer(xs, c_sample, PAST_LEN, state_ret[l], state_conv[l], *lw)
        ret_p.append(sp); ret_s.append(ss); conv_p.append(cp); conv_s.append(cs); v_s.append(vs)
    return (xp, xs, jnp.stack(ret_p), jnp.stack(ret_s), jnp.stack(conv_p), jnp.stack(conv_s), jnp.stack(v_s))
```

```python
import functools
import math

import jax
import jax.numpy as jnp
from jax import lax
from jax.experimental import pallas as pl
from jax.experimental.pallas import tpu as pltpu

F32 = jnp.float32
BF16 = jnp.bfloat16

EPS = 1e-6
ROPE_BASE = 10000.0
PAST_LEN = 16384
SGU_GROUPS = 8
SGU_CHUNK = 128
RET_HEADS = 8
RET_CHUNK = 128
CONV_W = 3

SUBLANES = 8
BF16_ROWS = 16
VMEM_LIMIT_BYTES = 56 * 1024 * 1024


def _params(n_axes):
    return pltpu.CompilerParams(dimension_semantics=("arbitrary",) * n_axes,
                                vmem_limit_bytes=VMEM_LIMIT_BYTES)


def _rms(x, g):
    return x * lax.rsqrt(jnp.mean(x * x, axis=-1, keepdims=True) + EPS) * g


def _silu(x):
    return x * jax.nn.sigmoid(x)


def _mod_kernel(c_ref, w_ref, b_ref, o_ref):
    a = _silu(c_ref[...]).astype(BF16)
    o_ref[...] = jnp.dot(a, w_ref[...].astype(BF16), preferred_element_type=F32) + b_ref[...]


def _modulation(c_all, w_ada, b_ada, tn=1024):
    depth, d, n = w_ada.shape
    rows = c_all.shape[0]
    return pl.pallas_call(
        _mod_kernel,
        grid=(depth, n // tn),
        in_specs=[pl.BlockSpec((rows, d), lambda l, j: (0, 0)),
                  pl.BlockSpec((None, d, tn), lambda l, j: (l, 0, j)),
                  pl.BlockSpec((None, 1, tn), lambda l, j: (l, 0, j))],
        out_specs=pl.BlockSpec((None, rows, tn), lambda l, j: (l, 0, j)),
        out_shape=jax.ShapeDtypeStruct((depth, rows, n), F32),
        compiler_params=_params(2),
        name="adaln_mod",
    )(c_all, w_ada, b_ada.reshape(depth, 1, n))


def _flat_bf16(h):
    return h.reshape(h.shape[0] * h.shape[1], h.shape[2]).astype(BF16)


def _prenorm_kernel(x_ref, g_ref, sc_ref, sh_ref, h_ref):
    h = _rms(x_ref[...], g_ref[...]) * (1.0 + sc_ref[...]) + sh_ref[...]
    h_ref[...] = _flat_bf16(h)


def _post_kernel(t_ref, x_ref, gpost_ref, gate_ref, gpre_ref, sc_ref, sh_ref, xo_ref, h_ref):
    x = x_ref[...] + gate_ref[...] * _rms(t_ref[...], gpost_ref[...])
    xo_ref[...] = x
    h = _rms(x, gpre_ref[...]) * (1.0 + sc_ref[...]) + sh_ref[...]
    h_ref[...] = _flat_bf16(h)


def _post_last_kernel(t_ref, x_ref, gpost_ref, gate_ref, xo_ref):
    xo_ref[...] = x_ref[...] + gate_ref[...] * _rms(t_ref[...], gpost_ref[...])


def _row_specs(nseq, seq_len, d, bs, tl):
    grid = (nseq // bs, seq_len // tl)
    act = pl.BlockSpec((bs, tl, d), lambda b, t: (b, t, 0))
    flat = pl.BlockSpec((bs * tl, d), lambda b, t: (b * (seq_len // tl) + t, 0))

    def vec(l):
        return pl.BlockSpec((1, 1, d), lambda b, t: (l, 0, 0))

    def mod(l, k):
        return pl.BlockSpec((None, bs, 1, d), lambda b, t: (l, b, 0, k))
    return grid, act, vec, flat, mod


def _vec3(p):
    return p.reshape(p.shape[0], 1, p.shape[1])


def _prenorm(x, g_pre, mod, l, k_sc, k_sh, bs, tl):
    nseq, seq_len, d = x.shape
    grid, act, vec, flat, modspec = _row_specs(nseq, seq_len, d, bs, tl)
    return pl.pallas_call(
        _prenorm_kernel, grid=grid,
        in_specs=[act, vec(l), modspec(l, k_sc), modspec(l, k_sh)],
        out_specs=flat,
        out_shape=jax.ShapeDtypeStruct((nseq * seq_len, d), BF16),
        compiler_params=_params(2), name="prenorm",
    )(x, _vec3(g_pre), mod, mod)


def _post(t, x, g_post, mod, l, k_gate, bs, tl, nxt=None):
    nseq, seq_len, d = x.shape
    grid, act, vec, flat, modspec = _row_specs(nseq, seq_len, d, bs, tl)
    t3 = t.reshape(nseq, seq_len, d)
    x_shape = jax.ShapeDtypeStruct((nseq, seq_len, d), F32)
    if nxt is None:
        return pl.pallas_call(
            _post_last_kernel, grid=grid,
            in_specs=[act, act, vec(l), modspec(l, k_gate)],
            out_specs=act, out_shape=x_shape,
            compiler_params=_params(2), name="post_last",
        )(t3, x, _vec3(g_post), mod), None
    g_pre, ln, k_sc, k_sh = nxt
    return pl.pallas_call(
        _post_kernel, grid=grid,
        in_specs=[act, act, vec(l), modspec(l, k_gate), vec(ln), modspec(ln, k_sc), modspec(ln, k_sh)],
        out_specs=[act, flat],
        out_shape=[x_shape, jax.ShapeDtypeStruct((nseq * seq_len, d), BF16)],
        compiler_params=_params(2), name="post",
    )(t3, x, _vec3(g_post), mod, _vec3(g_pre), mod, mod)


def _mm_kernel(x_ref, w_ref, o_ref, wb_ref):
    @pl.when(pl.program_id(1) == 0)
    def _():
        wb_ref[...] = w_ref[...].astype(BF16)
    o_ref[...] = jnp.dot(x_ref[...], wb_ref[...], preferred_element_type=F32).astype(o_ref.dtype)


def _matmul(x, w, l, tm, tn, out_dtype):
    rows, k = x.shape
    n = w.shape[2]
    return pl.pallas_call(
        _mm_kernel, grid=(n // tn, rows // tm),
        in_specs=[pl.BlockSpec((tm, k), lambda j, i: (i, 0)),
                  pl.BlockSpec((None, k, tn), lambda j, i: (l, 0, j))],
        out_specs=pl.BlockSpec((tm, tn), lambda j, i: (i, j)),
        out_shape=jax.ShapeDtypeStruct((rows, n), out_dtype),
        scratch_shapes=[pltpu.VMEM((k, tn), BF16)],
        compiler_params=_params(2), name="matmul",
    )(x, w)


IN_TILE = 1024
J_U, J_V, J_Q, J_K, J_VR, J_GR, J_GA, J_GB = 0, 1, 2, 3, 4, 5, 6, 8


def _inproj_kernel(h_ref, w_ref, lng_ref, lnb_ref, o_ref, wb_ref, *, tm, seq_len, pos0, head_dim):
    j = pl.program_id(0)
    i = pl.program_id(1)

    @pl.when(i == 0)
    def _():
        wb_ref[...] = w_ref[...].astype(BF16)

    acc = jnp.dot(h_ref[...], wb_ref[...], preferred_element_type=F32)

    @pl.when(j == J_U)
    def _():
        o_ref[...] = jax.nn.gelu(acc).astype(BF16)

    @pl.when(j == J_V)
    def _():
        v = jax.nn.gelu(acc)
        mu = jnp.mean(v, axis=-1, keepdims=True)
        var = jnp.mean(jnp.square(v - mu), axis=-1, keepdims=True)
        o_ref[...] = ((v - mu) * lax.rsqrt(var + EPS) * lng_ref[...] + lnb_ref[...]).astype(BF16)

    @pl.when((j == J_Q) | (j == J_K))
    def _():
        half = head_dim // 2
        lane = lax.broadcasted_iota(jnp.int32, (tm, head_dim), 1)
        row = lax.broadcasted_iota(jnp.int32, (tm, head_dim), 0)
        pos = pos0 + ((i * tm + row) & (seq_len - 1))
        inv = jnp.exp((lane & (half - 1)).astype(F32) * (-math.log(ROPE_BASE) / half))
        ang = pos.astype(F32) * inv
        cos = jnp.cos(ang)
        sin = jnp.sin(ang)
        sin = jnp.where(lane < half, -sin, sin)
        scale = jnp.where(j == J_K, head_dim ** -0.5, 1.0).astype(F32)
        for h in range(IN_TILE // head_dim):
            xh = acc[:, h * head_dim:(h + 1) * head_dim]
            r = xh * cos + pltpu.roll(xh, half, axis=1) * sin
            o_ref[:, h * head_dim:(h + 1) * head_dim] = (r * scale).astype(BF16)

    @pl.when(j == J_VR)
    def _():
        o_ref[...] = acc.astype(BF16)

    @pl.when(j == J_GR)
    def _():
        o_ref[...] = _silu(acc).astype(BF16)

    @pl.when(j >= J_GA)
    def _():
        o_ref[...] = jax.nn.sigmoid(acc).astype(BF16)


def _inproj(h, w_in, ln_g, ln_b, l, tm, seq_len, pos0):
    rows, k = h.shape
    n = w_in.shape[2]
    assert seq_len & (seq_len - 1) == 0 and n == 10 * IN_TILE
    kern = functools.partial(_inproj_kernel, tm=tm, seq_len=seq_len, pos0=pos0, head_dim=IN_TILE // RET_HEADS)
    vec = pl.BlockSpec((None, 1, IN_TILE), lambda j, i: (l, 0, 0))
    return pl.pallas_call(
        kern, grid=(n // IN_TILE, rows // tm),
        in_specs=[pl.BlockSpec((tm, k), lambda j, i: (i, 0)),
                  pl.BlockSpec((None, k, IN_TILE), lambda j, i: (l, 0, j)),
                  vec, vec],
        out_specs=pl.BlockSpec((tm, IN_TILE), lambda j, i: (i, j)),
        out_shape=jax.ShapeDtypeStruct((rows, n), BF16),
        scratch_shapes=[pltpu.VMEM((k, IN_TILE), BF16)],
        compiler_params=_params(2), name="inproj",
    )(h, w_in, _vec3(ln_g), _vec3(ln_b))


def _sgu_kernel(u_ref, v_ref, w_ref, b_ref, o_ref, *, tm, chunk):
    r = lax.broadcasted_iota(jnp.int32, (SGU_CHUNK, SGU_CHUNK), 0)
    c = lax.broadcasted_iota(jnp.int32, (SGU_CHUNK, SGU_CHUNK), 1)
    mask = (c <= r) & ((r & -chunk) == (c & -chunk))
    gd = IN_TILE // SGU_GROUPS
    for g in range(SGU_GROUPS):
        w = jnp.where(mask, w_ref[g], 0.0).astype(BF16)
        bias = b_ref[:, g:g + 1]
        for t in range(tm // SGU_CHUNK):
            rs = slice(t * SGU_CHUNK, (t + 1) * SGU_CHUNK)
            cs = slice(g * gd, (g + 1) * gd)
            z = jnp.dot(w, v_ref[rs, cs], preferred_element_type=F32) + bias
            o_ref[rs, cs] = (u_ref[rs, cs].astype(F32) * z).astype(BF16)


def _sgu_tables(w_s, b_s, seq_len):
    chunk = min(SGU_CHUNK, seq_len)
    reps = SGU_CHUNK // chunk
    w_t = jnp.tile(w_s[:, :, :chunk, :chunk], (1, 1, reps, reps))
    b_t = jnp.tile(jnp.swapaxes(b_s[:, :, :chunk], 1, 2), (1, reps, 1))
    return w_t, b_t


def _sgu(proj, w_t, b_t, l, seq_len, tm):
    rows = proj.shape[0]
    kern = functools.partial(_sgu_kernel, tm=tm, chunk=min(SGU_CHUNK, seq_len))
    return pl.pallas_call(
        kern, grid=(rows // tm,),
        in_specs=[pl.BlockSpec((tm, IN_TILE), lambda i: (i, J_U)),
                  pl.BlockSpec((tm, IN_TILE), lambda i: (i, J_V)),
                  pl.BlockSpec((None, SGU_GROUPS, SGU_CHUNK, SGU_CHUNK), lambda i: (l, 0, 0, 0)),
                  pl.BlockSpec((None, SGU_CHUNK, SGU_GROUPS), lambda i: (l, 0, 0))],
        out_specs=pl.BlockSpec((tm, IN_TILE), lambda i: (i, 0)),
        out_shape=jax.ShapeDtypeStruct((rows, IN_TILE), BF16),
        compiler_params=_params(1), name="sgu",
    )(proj, proj, w_t, b_t)


def _log_gamma(h):
    return math.log(1.0 - 2.0 ** (-5.0 - h))


def _ret_chunk(qh, kh, vh, grh, gnh, s, h, c):
    lg = _log_gamma(h)
    ri = lax.broadcasted_iota(jnp.int32, (c, c), 0)
    ci = lax.broadcasted_iota(jnp.int32, (c, c), 1)
    diff = (ri - ci).astype(F32)
    decay = jnp.where(ri >= ci, jnp.exp(lg * jnp.maximum(diff, 0.0)), 0.0)
    idx = lax.broadcasted_iota(jnp.int32, (c, 1), 0).astype(F32)
    q_dec = jnp.exp(lg * (idx + 1.0))
    k_dec = jnp.exp(lg * (c - 1.0 - idx))
    scores = lax.dot_general(qh, kh, (((1,), (1,)), ((), ())), preferred_element_type=F32) * decay
    inner = jnp.dot(scores.astype(BF16), vh, preferred_element_type=F32)
    cross = jnp.dot(qh, s.astype(BF16), preferred_element_type=F32) * q_dec
    kd = (kh.astype(F32) * k_dec).astype(BF16)
    s_new = s * math.exp(lg * c) + lax.dot_general(kd, vh, (((0,), (0,)), ((), ())), preferred_element_type=F32)
    o = inner + cross
    mu = jnp.mean(o, axis=-1, keepdims=True)
    var = jnp.mean(jnp.square(o - mu), axis=-1, keepdims=True)
    y = (o - mu) * lax.rsqrt(var + EPS) * gnh
    return grh * y, s_new


def _ret_prompt_kernel(q_ref, k_ref, v_ref, gr_ref, gn_ref, o_ref, so_ref, s_ref, *, hd):
    n = pl.program_id(1)

    @pl.when(n == 0)
    def _():
        s_ref[...] = jnp.zeros_like(s_ref)

    for h in range(RET_HEADS):
        cs = slice(h * hd, (h + 1) * hd)
        y, s_new = _ret_chunk(q_ref[:, cs], k_ref[:, cs], v_ref[:, cs], gr_ref[:, cs].astype(F32),
                              gn_ref[:, cs], s_ref[h], h, RET_CHUNK)
        o_ref[:, cs] = y.astype(BF16)
        s_ref[h] = s_new

    @pl.when(n == pl.num_programs(1) - 1)
    def _():
        so_ref[...] = s_ref[...]


def _ret_prompt(proj, gn_g, l, nseq, seq_len):
    hd = IN_TILE // RET_HEADS
    nc = seq_len // RET_CHUNK

    def col(jc):
        return pl.BlockSpec((RET_CHUNK, IN_TILE), lambda b, n: (b * nc + n, jc))
    kern = functools.partial(_ret_prompt_kernel, hd=hd)
    return pl.pallas_call(
        kern, grid=(nseq, nc),
        in_specs=[col(J_Q), col(J_K), col(J_VR), col(J_GR),
                  pl.BlockSpec((None, 1, IN_TILE), lambda b, n: (l, 0, 0))],
        out_specs=[pl.BlockSpec((RET_CHUNK, IN_TILE), lambda b, n: (b * nc + n, 0)),
                   pl.BlockSpec((None, RET_HEADS, hd, hd), lambda b, n: (b, 0, 0, 0))],
        out_shape=[jax.ShapeDtypeStruct((nseq * seq_len, IN_TILE), BF16),
                   jax.ShapeDtypeStruct((nseq, RET_HEADS, hd, hd), F32)],
        scratch_shapes=[pltpu.VMEM((RET_HEADS, hd, hd), F32)],
        compiler_params=_params(2), name="ret_prompt",
    )(proj, proj, proj, proj, _vec3(gn_g))


def _ret_sample_kernel(q_ref, k_ref, v_ref, gr_ref, gn_ref, s_ref, o_ref, so_ref, y_ref, *, hd, bs, seq_len):
    q = q_ref[...].astype(F32)
    k = k_ref[...].astype(F32)
    v = v_ref[...].astype(F32)
    gr = gr_ref[...].astype(F32)
    for b in range(bs):
        rs = slice(b * seq_len, (b + 1) * seq_len)
        for h in range(RET_HEADS):
            cs = slice(h * hd, (h + 1) * hd)
            y, s_new = _ret_chunk(q[rs, cs].astype(BF16), k[rs, cs].astype(BF16), v[rs, cs].astype(BF16),
                                  gr[rs, cs], gn_ref[:, cs], s_ref[b, h], h, seq_len)
            y_ref[rs, cs] = y
            so_ref[b, h] = s_new
    o_ref[...] = y_ref[...].astype(BF16)


def _ret_sample(proj, gn_g, state, l, seq_len, bs):
    nseq = state.shape[1]
    hd = IN_TILE // RET_HEADS

    def col(jc):
        return pl.BlockSpec((bs * seq_len, IN_TILE), lambda i: (i, jc))
    kern = functools.partial(_ret_sample_kernel, hd=hd, bs=bs, seq_len=seq_len)
    return pl.pallas_call(
        kern, grid=(nseq // bs,),
        in_specs=[col(J_Q), col(J_K), col(J_VR), col(J_GR),
                  pl.BlockSpec((None, 1, IN_TILE), lambda i: (l, 0, 0)),
                  pl.BlockSpec((None, bs, RET_HEADS, hd, hd), lambda i: (l, i, 0, 0, 0))],
        out_specs=[pl.BlockSpec((bs * seq_len, IN_TILE), lambda i: (i, 0)),
                   pl.BlockSpec((bs, RET_HEADS, hd, hd), lambda i: (i, 0, 0, 0))],
        out_shape=[jax.ShapeDtypeStruct((nseq * seq_len, IN_TILE), BF16),
                   jax.ShapeDtypeStruct(state.shape[1:], F32)],
        scratch_shapes=[pltpu.VMEM((bs * seq_len, IN_TILE), F32)],
        compiler_params=_params(1), name="ret_sample",
    )(proj, proj, proj, proj, _vec3(gn_g), state)


def _merge_kernel(ya_ref, yb_ref, ga_ref, gb_ref, wa_ref, wb_ref, o_ref, wab_ref, wbb_ref):
    @pl.when(pl.program_id(1) == 0)
    def _():
        wab_ref[...] = wa_ref[...].astype(BF16)
        wbb_ref[...] = wb_ref[...].astype(BF16)
    a = jnp.dot(ya_ref[...], wab_ref[...], preferred_element_type=F32)
    b = jnp.dot(yb_ref[...], wbb_ref[...], preferred_element_type=F32)
    o_ref[...] = (ga_ref[...].astype(F32) * a + gb_ref[...].astype(F32) * b).astype(BF16)


def _merge(ya, yb, proj, w_a, w_b, l, tm):
    rows, k = ya.shape
    n = w_a.shape[2]
    tn = IN_TILE
    row = pl.BlockSpec((tm, k), lambda j, i: (i, 0))
    wsp = pl.BlockSpec((None, k, tn), lambda j, i: (l, 0, j))
    return pl.pallas_call(
        _merge_kernel, grid=(n // tn, rows // tm),
        in_specs=[row, row,
                  pl.BlockSpec((tm, tn), lambda j, i: (i, J_GA + j)),
                  pl.BlockSpec((tm, tn), lambda j, i: (i, J_GB + j)),
                  wsp, wsp],
        out_specs=pl.BlockSpec((tm, tn), lambda j, i: (i, j)),
        out_shape=jax.ShapeDtypeStruct((rows, n), BF16),
        scratch_shapes=[pltpu.VMEM((k, tn), BF16), pltpu.VMEM((k, tn), BF16)],
        compiler_params=_params(2), name="merge",
    )(ya, yb, proj, proj, w_a, w_b)


def _ffn_up_kernel(h_ref, wg_ref, wu_ref, cw_ref, cb_ref, *rest, tm, seq_len, has_state):
    if has_state:
        st_ref, y_ref, so_ref, wgb_ref, wub_ref, gbuf_ref = rest
    else:
        y_ref, so_ref, wgb_ref, wub_ref, gbuf_ref = rest
    i = pl.program_id(1)
    pad = SUBLANES
    keep = CONV_W - 1

    @pl.when(i == 0)
    def _():
        wgb_ref[...] = wg_ref[...].astype(BF16)
        wub_ref[...] = wu_ref[...].astype(BF16)

    h = h_ref[...]
    g = jnp.dot(h, wgb_ref[...], preferred_element_type=F32)
    up = jnp.dot(h, wub_ref[...], preferred_element_type=F32)
    tn = g.shape[1]

    if has_state:
        gbuf_ref[0:pad, :] = jnp.zeros((pad, tn), F32)
    else:
        @pl.when(((i * tm) & (seq_len - 1)) == 0)
        def _():
            gbuf_ref[0:pad, :] = jnp.zeros((pad, tn), F32)
    gbuf_ref[pad:pad + tm, :] = g
    g1 = gbuf_ref[pad - 1:pad - 1 + tm, :]
    g2 = gbuf_ref[pad - 2:pad - 2 + tm, :]
    if has_state:
        bs = tm // seq_len
        st = st_ref[...]
        s0 = jnp.broadcast_to(st[:, 0:1, :], (bs, seq_len, tn)).reshape(tm, tn)
        s1 = jnp.broadcast_to(st[:, 1:2, :], (bs, seq_len, tn)).reshape(tm, tn)
        t = lax.broadcasted_iota(jnp.int32, (tm, tn), 0) & (seq_len - 1)
        g1 = jnp.where(t == 0, s1, g1)
        g2 = jnp.where(t == 0, s0, jnp.where(t == 1, s1, g2))
        so_ref[...] = g.reshape(bs, seq_len, tn)[:, seq_len - keep:, :]
    else:
        gbuf_ref[pad - keep:pad, :] = g[tm - keep:, :]
        so_ref[...] = g[tm - keep:, :]
    conv = g2 * cw_ref[0:1, :] + g1 * cw_ref[1:2, :] + g * cw_ref[2:3, :] + cb_ref[...]
    y_ref[...] = (jax.nn.gelu(conv) * up).astype(BF16)


def _ffn_up(h, w_gate, w_up, conv_w, conv_b, l, nseq, seq_len, tm, tn, state=None):
    rows, k = h.shape
    n = w_gate.shape[2]
    keep = CONV_W - 1
    has_state = state is not None
    assert seq_len & (seq_len - 1) == 0 and seq_len >= CONV_W
    kern = functools.partial(_ffn_up_kernel, tm=tm, seq_len=seq_len, has_state=has_state)
    wsp = pl.BlockSpec((None, k, tn), lambda j, i: (l, 0, j))
    in_specs = [pl.BlockSpec((tm, k), lambda j, i: (i, 0)), wsp, wsp,
                pl.BlockSpec((None, CONV_W, tn), lambda j, i: (l, 0, j)),
                pl.BlockSpec((None, 1, tn), lambda j, i: (l, 0, j))]
    args = [h, w_gate, w_up, conv_w, _vec3(conv_b)]
    if has_state:
        bs = tm // seq_len
        in_specs.append(pl.BlockSpec((None, bs, keep, tn), lambda j, i: (l, i, 0, j)))
        args.append(state)
        so_spec = pl.BlockSpec((bs, keep, tn), lambda j, i: (i, 0, j))
    else:
        tps = seq_len // tm
        so_spec = pl.BlockSpec((None, keep, tn), lambda j, i: (i // tps, 0, j))
    return pl.pallas_call(
        kern, grid=(n // tn, rows // tm),
        in_specs=in_specs,
        out_specs=[pl.BlockSpec((tm, tn), lambda j, i: (i, j)), so_spec],
        out_shape=[jax.ShapeDtypeStruct((rows, n), BF16),
                   jax.ShapeDtypeStruct((nseq, keep, n), F32)],
        scratch_shapes=[pltpu.VMEM((k, tn), BF16), pltpu.VMEM((k, tn), BF16),
                        pltpu.VMEM((tm + SUBLANES, tn), F32)],
        compiler_params=_params(2), name="ffn_up",
    )(*args)


K_SH1, K_SC1, K_G1, K_SH2, K_SC2, K_G2 = range(6)


def _layer(l, depth, x, h, mod, w, cfg, s_ret, conv_prev):
    nseq, seq_len, d = x.shape
    tm, bs, tl = cfg["tm"], cfg["bs"], cfg["tl"]

    proj = _inproj(h, w["w_in"], w["sgu_ln_g"], w["sgu_ln_b"], l, tm, seq_len, cfg["pos0"])
    ya = _sgu(proj, cfg["sgu_w"], cfg["sgu_b"], l, seq_len, tm)
    if s_ret is None:
        yb, s_new = _ret_prompt(proj, w["ret_gn_g"], l, nseq, seq_len)
    else:
        yb, s_new = _ret_sample(proj, w["ret_gn_g"], s_ret, l, seq_len, cfg["ret_bs"])
    merged = _merge(ya, yb, proj, w["w_branch_a"], w["w_branch_b"], l, tm)
    t = _matmul(merged, w["w_out"], l, tm, 1024, F32)
    x, h2 = _post(t, x, w["norm_post1"], mod, l, K_G1, bs, tl, nxt=(w["norm_pre2"], l, K_SC2, K_SH2))

    y, conv_new = _ffn_up(h2, w["ffn_w_gate"], w["ffn_w_up"], w["ffn_conv_w"], w["ffn_conv_b"], l,
                          nseq, seq_len, tm, 512, state=conv_prev)
    f = _matmul(y, w["ffn_w_down"], l, tm, 256, F32)
    nxt = None if l == depth - 1 else (w["norm_pre1"], l + 1, K_SC1, K_SH1)
    x, h_next = _post(f, x, w["norm_post2"], mod, l, K_G2, bs, tl, nxt=nxt)
    v_rows = proj[:, J_V * IN_TILE:(J_V + 1) * IN_TILE]
    return x, h_next, s_new, conv_new, v_rows


def kernel(x_prompt, x_sample, state_ret, state_conv, c_prompt, c_sample, w_ada, b_ada, norm_pre1, norm_post1, norm_pre2, norm_post2, w_in, sgu_w_s, sgu_b_s, sgu_ln_g, sgu_ln_b, ret_gn_g, w_branch_a, w_branch_b, w_out, ffn_w_gate, ffn_w_up, ffn_conv_w, ffn_conv_b, ffn_w_down):
    w = dict(norm_pre1=norm_pre1, norm_post1=norm_post1, norm_pre2=norm_pre2, norm_post2=norm_post2, w_in=w_in,
             sgu_ln_g=sgu_ln_g, sgu_ln_b=sgu_ln_b, ret_gn_g=ret_gn_g, w_branch_a=w_branch_a,
             w_branch_b=w_branch_b, w_out=w_out, ffn_w_gate=ffn_w_gate, ffn_w_up=ffn_w_up,
             ffn_conv_w=ffn_conv_w, ffn_conv_b=ffn_conv_b, ffn_w_down=ffn_w_down)
    depth = w_ada.shape[0]
    nb, seq, d = x_prompt.shape
    ndb, dseq, _ = x_sample.shape

    n_c = nb + ndb
    c_all = jnp.concatenate([c_prompt, c_sample, jnp.zeros((-n_c % BF16_ROWS, d), F32)], axis=0)
    mod = _modulation(c_all, w_ada, b_ada)
    mod_p = mod[:, :nb].reshape(depth, nb, 1, 6 * d)
    mod_s = mod[:, nb:n_c].reshape(depth, ndb, 1, 6 * d)

    sgu_wp, sgu_bp = _sgu_tables(sgu_w_s, sgu_b_s, seq)
    sgu_ws, sgu_bs = _sgu_tables(sgu_w_s, sgu_b_s, dseq)
    cfg_p = dict(tm=512, bs=1, tl=512, pos0=0, sgu_w=sgu_wp, sgu_b=sgu_bp)
    cfg_s = dict(tm=512, bs=32, tl=dseq, pos0=PAST_LEN, ret_bs=8, sgu_w=sgu_ws, sgu_b=sgu_bs)

    xp, xs = x_prompt, x_sample
    hp = _prenorm(xp, norm_pre1, mod_p, 0, K_SC1, K_SH1, cfg_p["bs"], cfg_p["tl"])
    hs = _prenorm(xs, norm_pre1, mod_s, 0, K_SC1, K_SH1, cfg_s["bs"], cfg_s["tl"])
    ret_p, ret_s, conv_p, conv_s, v_s = [], [], [], [], []
    for l in range(depth):
        xp, hp, sp, cp, _ = _layer(l, depth, xp, hp, mod_p, w, cfg_p, None, None)
        xs, hs, ss, cs, vs = _layer(l, depth, xs, hs, mod_s, w, cfg_s, state_ret, state_conv)
        ret_p.append(sp)
        ret_s.append(ss)
        conv_p.append(cp)
        conv_s.append(cs)
        v_s.append(vs.astype(F32).reshape(ndb, dseq, -1))
    return (xp, xs, jnp.stack(ret_p), jnp.stack(ret_s), jnp.stack(conv_p), jnp.stack(conv_s), jnp.stack(v_s))
```

```python
import functools
import math

import jax
import jax.numpy as jnp
from jax import lax
from jax.experimental import pallas as pl
from jax.experimental.pallas import tpu as pltpu

F32 = jnp.float32
BF16 = jnp.bfloat16

EPS = 1e-6
ROPE_BASE = 10000.0
PAST_LEN = 16384
SGU_GROUPS = 8
SGU_CHUNK = 128
RET_HEADS = 8
RET_CHUNK = 128
CONV_W = 3

SUBLANES = 8
BF16_ROWS = 16
ROW_CHUNK = 128
VMEM_LIMIT_BYTES = 56 * 1024 * 1024


def _params(n_axes):
    return pltpu.CompilerParams(dimension_semantics=("arbitrary",) * n_axes,
                                vmem_limit_bytes=VMEM_LIMIT_BYTES)


def _carried(kernel, n_in, prev):
    if prev is None:
        return kernel, [], [], {}

    def body(*refs):
        kernel(*refs[:n_in], *refs[n_in + 1:])
    return body, [pl.BlockSpec(memory_space=pl.ANY)], [prev], {n_in: 1}


def _rms(x, g):
    return x * lax.rsqrt(jnp.mean(x * x, axis=-1, keepdims=True) + EPS) * g


def _silu(x):
    return x * jax.nn.sigmoid(x)


def _mod_kernel(c_ref, w_ref, b_ref, op_ref, os_ref, *, n_p, s0):
    a = _silu(c_ref[...]).astype(BF16)
    m = jnp.dot(a, w_ref[...].astype(BF16), preferred_element_type=F32) + b_ref[...]
    op_ref[...] = m[0:n_p][:, None, :]
    os_ref[...] = m[s0:][:, None, :]


def _modulation(c_prompt, c_sample, w_ada, b_ada, tn=1024):
    depth, d, n = w_ada.shape
    n_p, n_s = c_prompt.shape[0], c_sample.shape[0]
    s0 = -(-n_p // BF16_ROWS) * BF16_ROWS
    c_all = jnp.concatenate([c_prompt, jnp.zeros((s0 - n_p, d), F32), c_sample], axis=0)
    rows = c_all.shape[0]
    assert rows % BF16_ROWS == 0
    return pl.pallas_call(
        functools.partial(_mod_kernel, n_p=n_p, s0=s0),
        grid=(depth, n // tn),
        in_specs=[pl.BlockSpec((rows, d), lambda l, j: (0, 0)),
                  pl.BlockSpec((None, d, tn), lambda l, j: (l, 0, j)),
                  pl.BlockSpec((None, 1, tn), lambda l, j: (l, 0, j))],
        out_specs=[pl.BlockSpec((None, n_p, 1, tn), lambda l, j: (l, 0, 0, j)),
                   pl.BlockSpec((None, n_s, 1, tn), lambda l, j: (l, 0, 0, j))],
        out_shape=[jax.ShapeDtypeStruct((depth, n_p, 1, n), F32),
                   jax.ShapeDtypeStruct((depth, n_s, 1, n), F32)],
        compiler_params=_params(2),
        name="adaln_mod",
    )(c_all, w_ada, b_ada.reshape(depth, 1, n))


def _flat_bf16(h):
    return h.reshape(h.shape[0] * h.shape[1], h.shape[2]).astype(BF16)


def _prenorm_kernel(x_ref, g_ref, sc_ref, sh_ref, h_ref):
    h = _rms(x_ref[...], g_ref[...]) * (1.0 + sc_ref[...]) + sh_ref[...]
    h_ref[...] = _flat_bf16(h)


def _post_kernel(t_ref, x_ref, gpost_ref, gate_ref, gpre_ref, sc_ref, sh_ref, xo_ref, h_ref):
    x = x_ref[...] + gate_ref[...] * _rms(t_ref[...], gpost_ref[...])
    xo_ref[...] = x
    h = _rms(x, gpre_ref[...]) * (1.0 + sc_ref[...]) + sh_ref[...]
    h_ref[...] = _flat_bf16(h)


def _post_last_kernel(t_ref, x_ref, gpost_ref, gate_ref, xo_ref):
    xo_ref[...] = x_ref[...] + gate_ref[...] * _rms(t_ref[...], gpost_ref[...])


def _row_specs(nseq, seq_len, d, bs, tl):
    grid = (nseq // bs, seq_len // tl)
    act = pl.BlockSpec((bs, tl, d), lambda b, t: (b, t, 0))
    flat = pl.BlockSpec((bs * tl, d), lambda b, t: (b * (seq_len // tl) + t, 0))

    def vec(l):
        return pl.BlockSpec((1, 1, d), lambda b, t: (l, 0, 0))

    def mod(l, k):
        return pl.BlockSpec((None, bs, 1, d), lambda b, t: (l, b, 0, k))
    return grid, act, vec, flat, mod


def _vec3(p):
    return p.reshape(p.shape[0], 1, p.shape[1])


def _prenorm(x, g_pre, mod, l, k_sc, k_sh, bs, tl):
    nseq, seq_len, d = x.shape
    grid, act, vec, flat, modspec = _row_specs(nseq, seq_len, d, bs, tl)
    return pl.pallas_call(
        _prenorm_kernel, grid=grid,
        in_specs=[act, vec(l), modspec(l, k_sc), modspec(l, k_sh)],
        out_specs=flat,
        out_shape=jax.ShapeDtypeStruct((nseq * seq_len, d), BF16),
        compiler_params=_params(2), name="prenorm",
    )(x, _vec3(g_pre), mod, mod)


def _post(t, x, g_post, mod, l, k_gate, bs, tl, nxt=None):
    nseq, seq_len, d = x.shape
    grid, act, vec, flat, modspec = _row_specs(nseq, seq_len, d, bs, tl)
    t3 = t.reshape(nseq, seq_len, d)
    x_shape = jax.ShapeDtypeStruct((nseq, seq_len, d), F32)
    if nxt is None:
        return pl.pallas_call(
            _post_last_kernel, grid=grid,
            in_specs=[act, act, vec(l), modspec(l, k_gate)],
            out_specs=act, out_shape=x_shape,
            compiler_params=_params(2), name="post_last",
        )(t3, x, _vec3(g_post), mod), None
    g_pre, ln, k_sc, k_sh = nxt
    return pl.pallas_call(
        _post_kernel, grid=grid,
        in_specs=[act, act, vec(l), modspec(l, k_gate), vec(ln), modspec(ln, k_sc), modspec(ln, k_sh)],
        out_specs=[act, flat],
        out_shape=[x_shape, jax.ShapeDtypeStruct((nseq * seq_len, d), BF16)],
        compiler_params=_params(2), name="post",
    )(t3, x, _vec3(g_post), mod, _vec3(g_pre), mod, mod)


def _mm_kernel(x_ref, w_ref, o_ref, wb_ref):
    @pl.when(pl.program_id(1) == 0)
    def _():
        wb_ref[...] = w_ref[...].astype(BF16)
    o_ref[...] = jnp.dot(x_ref[...], wb_ref[...], preferred_element_type=F32).astype(o_ref.dtype)


def _matmul(x, w, l, tm, tn, out_dtype):
    rows, k = x.shape
    n = w.shape[2]
    return pl.pallas_call(
        _mm_kernel, grid=(n // tn, rows // tm),
        in_specs=[pl.BlockSpec((tm, k), lambda j, i: (i, 0)),
                  pl.BlockSpec((None, k, tn), lambda j, i: (l, 0, j))],
        out_specs=pl.BlockSpec((tm, tn), lambda j, i: (i, j)),
        out_shape=jax.ShapeDtypeStruct((rows, n), out_dtype),
        scratch_shapes=[pltpu.VMEM((k, tn), BF16)],
        compiler_params=_params(2), name="matmul",
    )(x, w)


IN_TILE = 1024
J_U, J_V, J_Q, J_K, J_VR, J_GR, J_GA, J_GB = 0, 1, 2, 3, 4, 5, 6, 8


def _inproj_kernel(h_ref, w_ref, lng_ref, lnb_ref, o_ref, wb_ref, *, tm, rc, seq_len, pos0, head_dim):
    j = pl.program_id(0)
    i = pl.program_id(1)

    @pl.when(i == 0)
    def _():
        wb_ref[...] = w_ref[...].astype(BF16)

    def run(epilogue):
        for c in range(tm // rc):
            rows = slice(c * rc, (c + 1) * rc)
            acc = jnp.dot(h_ref[rows, :], wb_ref[...], preferred_element_type=F32)
            o_ref[rows, :] = epilogue(acc, c).astype(BF16)

    def layer_norm_gelu(acc, c):
        v = jax.nn.gelu(acc)
        mu = jnp.mean(v, axis=-1, keepdims=True)
        var = jnp.mean(jnp.square(v - mu), axis=-1, keepdims=True)
        return (v - mu) * lax.rsqrt(var + EPS) * lng_ref[...] + lnb_ref[...]

    def rope(acc, c):
        half = head_dim // 2
        lane = lax.broadcasted_iota(jnp.int32, (rc, head_dim), 1)
        row = lax.broadcasted_iota(jnp.int32, (rc, head_dim), 0)
        pos = pos0 + ((i * tm + c * rc + row) & (seq_len - 1))
        inv = jnp.exp((lane & (half - 1)).astype(F32) * (-math.log(ROPE_BASE) / half))
        ang = pos.astype(F32) * inv
        scale = jnp.where(j == J_K, head_dim ** -0.5, 1.0).astype(F32)
        cos = jnp.cos(ang) * scale
        sin = jnp.where(lane < half, -jnp.sin(ang), jnp.sin(ang)) * scale
        heads = [acc[:, h * head_dim:(h + 1) * head_dim] for h in range(IN_TILE // head_dim)]
        return jnp.concatenate([xh * cos + pltpu.roll(xh, half, axis=1) * sin for xh in heads], axis=1)

    pl.when(j == J_U)(lambda: run(lambda acc, c: jax.nn.gelu(acc)))
    pl.when(j == J_V)(lambda: run(layer_norm_gelu))
    pl.when((j == J_Q) | (j == J_K))(lambda: run(rope))
    pl.when(j == J_VR)(lambda: run(lambda acc, c: acc))
    pl.when(j == J_GR)(lambda: run(lambda acc, c: _silu(acc)))
    pl.when(j >= J_GA)(lambda: run(lambda acc, c: jax.nn.sigmoid(acc)))


def _inproj(h, w_in, ln_g, ln_b, l, tm, seq_len, pos0):
    rows, k = h.shape
    n = w_in.shape[2]
    assert seq_len & (seq_len - 1) == 0 and n == 10 * IN_TILE
    kern = functools.partial(_inproj_kernel, tm=tm, rc=ROW_CHUNK, seq_len=seq_len, pos0=pos0,
                             head_dim=IN_TILE // RET_HEADS)
    vec = pl.BlockSpec((None, 1, IN_TILE), lambda j, i: (l, 0, 0))
    return pl.pallas_call(
        kern, grid=(n // IN_TILE, rows // tm),
        in_specs=[pl.BlockSpec((tm, k), lambda j, i: (i, 0)),
                  pl.BlockSpec((None, k, IN_TILE), lambda j, i: (l, 0, j)),
                  vec, vec],
        out_specs=pl.BlockSpec((tm, IN_TILE), lambda j, i: (i, j)),
        out_shape=jax.ShapeDtypeStruct((rows, n), BF16),
        scratch_shapes=[pltpu.VMEM((k, IN_TILE), BF16)],
        compiler_params=_params(2), name="inproj",
    )(h, w_in, _vec3(ln_g), _vec3(ln_b))


def _sgu_kernel(u_ref, v_ref, w_ref, b_ref, o_ref, *vo_ref, tm, chunk):
    if vo_ref:
        vo_ref[0][...] = v_ref[...].astype(F32)
    r = lax.broadcasted_iota(jnp.int32, (SGU_CHUNK, SGU_CHUNK), 0)
    c = lax.broadcasted_iota(jnp.int32, (SGU_CHUNK, SGU_CHUNK), 1)
    mask = (c <= r) & ((r & -chunk) == (c & -chunk))
    gd = IN_TILE // SGU_GROUPS
    for g in range(SGU_GROUPS):
        w = jnp.where(mask, w_ref[g], 0.0).astype(BF16)
        bias = b_ref[:, g:g + 1]
        for t in range(tm // SGU_CHUNK):
            rs = slice(t * SGU_CHUNK, (t + 1) * SGU_CHUNK)
            cs = slice(g * gd, (g + 1) * gd)
            z = jnp.dot(w, v_ref[rs, cs], preferred_element_type=F32) + bias
            o_ref[rs, cs] = (u_ref[rs, cs].astype(F32) * z).astype(BF16)


def _sgu_tables(w_s, b_s, seq_len):
    chunk = min(SGU_CHUNK, seq_len)
    reps = SGU_CHUNK // chunk
    w_t = jnp.tile(w_s[:, :, :chunk, :chunk], (1, 1, reps, reps))
    b_t = jnp.tile(jnp.swapaxes(b_s[:, :, :chunk], 1, 2), (1, reps, 1))
    return w_t, b_t


def _sgu(proj, w_t, b_t, l, depth, seq_len, tm, emit_v, prev):
    rows = proj.shape[0]
    kern = functools.partial(_sgu_kernel, tm=tm, chunk=min(SGU_CHUNK, seq_len))
    in_specs = [pl.BlockSpec((tm, IN_TILE), lambda i: (i, J_U)),
                pl.BlockSpec((tm, IN_TILE), lambda i: (i, J_V)),
                pl.BlockSpec((None, SGU_GROUPS, SGU_CHUNK, SGU_CHUNK), lambda i: (l, 0, 0, 0)),
                pl.BlockSpec((None, SGU_CHUNK, SGU_GROUPS), lambda i: (l, 0, 0))]
    out_specs = [pl.BlockSpec((tm, IN_TILE), lambda i: (i, 0))]
    out_shape = [jax.ShapeDtypeStruct((rows, IN_TILE), BF16)]
    cspecs, cargs, aliases = [], [], {}
    if emit_v:
        kern, cspecs, cargs, aliases = _carried(kern, 4, prev)
        out_specs.append(pl.BlockSpec((None, tm, IN_TILE), lambda i: (l, i, 0)))
        out_shape.append(jax.ShapeDtypeStruct((depth, rows, IN_TILE), F32))
    out = pl.pallas_call(
        kern, grid=(rows // tm,),
        in_specs=in_specs + cspecs, out_specs=out_specs, out_shape=out_shape,
        input_output_aliases=aliases,
        compiler_params=_params(1), name="sgu",
    )(proj, proj, w_t, b_t, *cargs)
    return (out[0], out[1]) if emit_v else (out[0], None)


def _log_gamma(h):
    return math.log(1.0 - 2.0 ** (-5.0 - h))


def _ret_chunk(qh, kh, vh, grh, gnh, s, h, c):
    lg = _log_gamma(h)
    ri = lax.broadcasted_iota(jnp.int32, (c, c), 0)
    ci = lax.broadcasted_iota(jnp.int32, (c, c), 1)
    diff = (ri - ci).astype(F32)
    decay = jnp.where(ri >= ci, jnp.exp(lg * jnp.maximum(diff, 0.0)), 0.0)
    idx = lax.broadcasted_iota(jnp.int32, (c, 1), 0).astype(F32)
    q_dec = jnp.exp(lg * (idx + 1.0))
    k_dec = jnp.exp(lg * (c - 1.0 - idx))
    scores = lax.dot_general(qh, kh, (((1,), (1,)), ((), ())), preferred_element_type=F32) * decay
    inner = jnp.dot(scores.astype(BF16), vh, preferred_element_type=F32)
    cross = jnp.dot(qh, s.astype(BF16), preferred_element_type=F32) * q_dec
    kd = (kh.astype(F32) * k_dec).astype(BF16)
    s_new = s * math.exp(lg * c) + lax.dot_general(kd, vh, (((0,), (0,)), ((), ())), preferred_element_type=F32)
    o = inner + cross
    mu = jnp.mean(o, axis=-1, keepdims=True)
    var = jnp.mean(jnp.square(o - mu), axis=-1, keepdims=True)
    y = (o - mu) * lax.rsqrt(var + EPS) * gnh
    return grh * y, s_new


def _ret_prompt_kernel(q_ref, k_ref, v_ref, gr_ref, gn_ref, o_ref, so_ref, s_ref, *, hd):
    n = pl.program_id(1)

    @pl.when(n == 0)
    def _():
        s_ref[...] = jnp.zeros_like(s_ref)

    for h in range(RET_HEADS):
        cs = slice(h * hd, (h + 1) * hd)
        y, s_new = _ret_chunk(q_ref[:, cs], k_ref[:, cs], v_ref[:, cs], gr_ref[:, cs].astype(F32),
                              gn_ref[:, cs], s_ref[h], h, RET_CHUNK)
        o_ref[:, cs] = y.astype(BF16)
        s_ref[h] = s_new

    @pl.when(n == pl.num_programs(1) - 1)
    def _():
        so_ref[...] = s_ref[...]


def _ret_prompt(proj, gn_g, l, depth, nseq, seq_len, prev):
    hd = IN_TILE // RET_HEADS
    nc = seq_len // RET_CHUNK

    def col(jc):
        return pl.BlockSpec((RET_CHUNK, IN_TILE), lambda b, n: (b * nc + n, jc))
    kern, cspecs, cargs, aliases = _carried(functools.partial(_ret_prompt_kernel, hd=hd), 5, prev)
    return pl.pallas_call(
        kern, grid=(nseq, nc),
        in_specs=[col(J_Q), col(J_K), col(J_VR), col(J_GR),
                  pl.BlockSpec((None, 1, IN_TILE), lambda b, n: (l, 0, 0))] + cspecs,
        out_specs=[pl.BlockSpec((RET_CHUNK, IN_TILE), lambda b, n: (b * nc + n, 0)),
                   pl.BlockSpec((None, None, RET_HEADS, hd, hd), lambda b, n: (l, b, 0, 0, 0))],
        out_shape=[jax.ShapeDtypeStruct((nseq * seq_len, IN_TILE), BF16),
                   jax.ShapeDtypeStruct((depth, nseq, RET_HEADS, hd, hd), F32)],
        scratch_shapes=[pltpu.VMEM((RET_HEADS, hd, hd), F32)],
        input_output_aliases=aliases,
        compiler_params=_params(2), name="ret_prompt",
    )(proj, proj, proj, proj, _vec3(gn_g), *cargs)


def _ret_sample_kernel(q_ref, k_ref, v_ref, gr_ref, gn_ref, s_ref, o_ref, so_ref, y_ref, *, hd, bs, seq_len):
    q = q_ref[...].astype(F32)
    k = k_ref[...].astype(F32)
    v = v_ref[...].astype(F32)
    gr = gr_ref[...].astype(F32)
    for b in range(bs):
        rs = slice(b * seq_len, (b + 1) * seq_len)
        for h in range(RET_HEADS):
            cs = slice(h * hd, (h + 1) * hd)
            y, s_new = _ret_chunk(q[rs, cs].astype(BF16), k[rs, cs].astype(BF16), v[rs, cs].astype(BF16),
                                  gr[rs, cs], gn_ref[:, cs], s_ref[b, h], h, seq_len)
            y_ref[rs, cs] = y
            so_ref[b, h] = s_new
    o_ref[...] = y_ref[...].astype(BF16)


def _ret_sample(proj, gn_g, state, l, seq_len, bs, prev):
    nseq = state.shape[1]
    hd = IN_TILE // RET_HEADS

    def col(jc):
        return pl.BlockSpec((bs * seq_len, IN_TILE), lambda i: (i, jc))
    st = pl.BlockSpec((None, bs, RET_HEADS, hd, hd), lambda i: (l, i, 0, 0, 0))
    kern, cspecs, cargs, aliases = _carried(
        functools.partial(_ret_sample_kernel, hd=hd, bs=bs, seq_len=seq_len), 6, prev)
    return pl.pallas_call(
        kern, grid=(nseq // bs,),
        in_specs=[col(J_Q), col(J_K), col(J_VR), col(J_GR),
                  pl.BlockSpec((None, 1, IN_TILE), lambda i: (l, 0, 0)), st] + cspecs,
        out_specs=[pl.BlockSpec((bs * seq_len, IN_TILE), lambda i: (i, 0)), st],
        out_shape=[jax.ShapeDtypeStruct((nseq * seq_len, IN_TILE), BF16),
                   jax.ShapeDtypeStruct(state.shape, F32)],
        scratch_shapes=[pltpu.VMEM((bs * seq_len, IN_TILE), F32)],
        input_output_aliases=aliases,
        compiler_params=_params(1), name="ret_sample",
    )(proj, proj, proj, proj, _vec3(gn_g), state, *cargs)


def _merge_kernel(ya_ref, yb_ref, ga_ref, gb_ref, wa_ref, wb_ref, o_ref, wab_ref, wbb_ref):
    @pl.when(pl.program_id(1) == 0)
    def _():
        wab_ref[...] = wa_ref[...].astype(BF16)
        wbb_ref[...] = wb_ref[...].astype(BF16)
    for c in range(o_ref.shape[0] // ROW_CHUNK):
        rows = slice(c * ROW_CHUNK, (c + 1) * ROW_CHUNK)
        a = jnp.dot(ya_ref[rows, :], wab_ref[...], preferred_element_type=F32)
        b = jnp.dot(yb_ref[rows, :], wbb_ref[...], preferred_element_type=F32)
        o_ref[rows, :] = (ga_ref[rows, :].astype(F32) * a + gb_ref[rows, :].astype(F32) * b).astype(BF16)


def _merge(ya, yb, proj, w_a, w_b, l, tm):
    rows, k = ya.shape
    n = w_a.shape[2]
    tn = IN_TILE
    row = pl.BlockSpec((tm, k), lambda j, i: (i, 0))
    wsp = pl.BlockSpec((None, k, tn), lambda j, i: (l, 0, j))
    return pl.pallas_call(
        _merge_kernel, grid=(n // tn, rows // tm),
        in_specs=[row, row,
                  pl.BlockSpec((tm, tn), lambda j, i: (i, J_GA + j)),
                  pl.BlockSpec((tm, tn), lambda j, i: (i, J_GB + j)),
                  wsp, wsp],
        out_specs=pl.BlockSpec((tm, tn), lambda j, i: (i, j)),
        out_shape=jax.ShapeDtypeStruct((rows, n), BF16),
        scratch_shapes=[pltpu.VMEM((k, tn), BF16), pltpu.VMEM((k, tn), BF16)],
        compiler_params=_params(2), name="merge",
    )(ya, yb, proj, proj, w_a, w_b)


def _ffn_up_kernel(h_ref, wg_ref, wu_ref, cw_ref, cb_ref, *rest, tm, rc, seq_len, has_state):
    if has_state:
        st_ref, y_ref, so_ref, wgb_ref, wub_ref, gbuf_ref = rest
    else:
        y_ref, so_ref, wgb_ref, wub_ref, gbuf_ref = rest
    i = pl.program_id(1)
    pad = SUBLANES
    keep = CONV_W - 1

    @pl.when(i == 0)
    def _():
        wgb_ref[...] = wg_ref[...].astype(BF16)
        wub_ref[...] = wu_ref[...].astype(BF16)

    tn = y_ref.shape[1]
    if has_state:
        gbuf_ref[0:pad, :] = jnp.zeros((pad, tn), F32)
    else:
        @pl.when(((i * tm) & (seq_len - 1)) == 0)
        def _():
            gbuf_ref[0:pad, :] = jnp.zeros((pad, tn), F32)
    for c in range(tm // rc):
        r0 = c * rc
        h = h_ref[r0:r0 + rc, :]
        g = jnp.dot(h, wgb_ref[...], preferred_element_type=F32)
        up = jnp.dot(h, wub_ref[...], preferred_element_type=F32)
        gbuf_ref[pad + r0:pad + r0 + rc, :] = g
        g1 = gbuf_ref[pad - 1 + r0:pad - 1 + r0 + rc, :]
        g2 = gbuf_ref[pad - 2 + r0:pad - 2 + r0 + rc, :]
        if has_state:
            bs = rc // seq_len
            st = st_ref[c * bs:(c + 1) * bs]
            s0 = jnp.broadcast_to(st[:, 0:1, :], (bs, seq_len, tn)).reshape(rc, tn)
            s1 = jnp.broadcast_to(st[:, 1:2, :], (bs, seq_len, tn)).reshape(rc, tn)
            t = lax.broadcasted_iota(jnp.int32, (rc, tn), 0) & (seq_len - 1)
            g1 = jnp.where(t == 0, s1, g1)
            g2 = jnp.where(t == 0, s0, jnp.where(t == 1, s1, g2))
            so_ref[c * bs:(c + 1) * bs] = g.reshape(bs, seq_len, tn)[:, seq_len - keep:, :]
        conv = g2 * cw_ref[0:1, :] + g1 * cw_ref[1:2, :] + g * cw_ref[2:3, :] + cb_ref[...]
        y_ref[r0:r0 + rc, :] = (jax.nn.gelu(conv) * up).astype(BF16)
    if not has_state:
        last = gbuf_ref[pad + tm - keep:pad + tm, :]
        gbuf_ref[pad - keep:pad, :] = last
        so_ref[...] = last


def _ffn_up(h, w_gate, w_up, conv_w, conv_b, l, depth, nseq, seq_len, tm, tn, state, prev):
    rows, k = h.shape
    n = w_gate.shape[2]
    keep = CONV_W - 1
    has_state = state is not None
    assert seq_len & (seq_len - 1) == 0 and seq_len >= CONV_W
    kern = functools.partial(_ffn_up_kernel, tm=tm, rc=ROW_CHUNK, seq_len=seq_len, has_state=has_state)
    wsp = pl.BlockSpec((None, k, tn), lambda j, i: (l, 0, j))
    in_specs = [pl.BlockSpec((tm, k), lambda j, i: (i, 0)), wsp, wsp,
                pl.BlockSpec((None, CONV_W, tn), lambda j, i: (l, 0, j)),
                pl.BlockSpec((None, 1, tn), lambda j, i: (l, 0, j))]
    args = [h, w_gate, w_up, conv_w, _vec3(conv_b)]
    if has_state:
        bs = tm // seq_len
        so_spec = pl.BlockSpec((None, bs, keep, tn), lambda j, i: (l, i, 0, j))
        in_specs.append(so_spec)
        args.append(state)
    else:
        tps = seq_len // tm
        so_spec = pl.BlockSpec((None, None, keep, tn), lambda j, i: (l, i // tps, 0, j))
    kern, cspecs, cargs, aliases = _carried(kern, len(args), prev)
    return pl.pallas_call(
        kern, grid=(n // tn, rows // tm),
        in_specs=in_specs + cspecs,
        out_specs=[pl.BlockSpec((tm, tn), lambda j, i: (i, j)), so_spec],
        out_shape=[jax.ShapeDtypeStruct((rows, n), BF16),
                   jax.ShapeDtypeStruct((depth, nseq, keep, n), F32)],
        scratch_shapes=[pltpu.VMEM((k, tn), BF16), pltpu.VMEM((k, tn), BF16),
                        pltpu.VMEM((tm + SUBLANES, tn), F32)],
        input_output_aliases=aliases,
        compiler_params=_params(2), name="ffn_up",
    )(*args, *cargs)


K_SH1, K_SC1, K_G1, K_SH2, K_SC2, K_G2 = range(6)


def _layer(l, depth, x, h, mod, w, cfg, s_ret, conv_prev, outs):
    nseq, seq_len, d = x.shape
    tm, bs, tl = cfg["tm"], cfg["bs"], cfg["tl"]
    ret_out, conv_out, v_out = outs

    proj = _inproj(h, w["w_in"], w["sgu_ln_g"], w["sgu_ln_b"], l, tm, seq_len, cfg["pos0"])
    ya, v_out = _sgu(proj, cfg["sgu_w"], cfg["sgu_b"], l, depth, seq_len, tm, cfg["emit_v"], v_out)
    if s_ret is None:
        yb, ret_out = _ret_prompt(proj, w["ret_gn_g"], l, depth, nseq, seq_len, ret_out)
    else:
        yb, ret_out = _ret_sample(proj, w["ret_gn_g"], s_ret, l, seq_len, cfg["ret_bs"], ret_out)
    merged = _merge(ya, yb, proj, w["w_branch_a"], w["w_branch_b"], l, tm)
    t = _matmul(merged, w["w_out"], l, tm, 1024, F32)
    x, h2 = _post(t, x, w["norm_post1"], mod, l, K_G1, bs, tl, nxt=(w["norm_pre2"], l, K_SC2, K_SH2))

    y, conv_out = _ffn_up(h2, w["ffn_w_gate"], w["ffn_w_up"], w["ffn_conv_w"], w["ffn_conv_b"], l, depth,
                          nseq, seq_len, tm, 512, conv_prev, conv_out)
    f = _matmul(y, w["ffn_w_down"], l, tm, 512, F32)
    nxt = None if l == depth - 1 else (w["norm_pre1"], l + 1, K_SC1, K_SH1)
    x, h_next = _post(f, x, w["norm_post2"], mod, l, K_G2, bs, tl, nxt=nxt)
    return x, h_next, (ret_out, conv_out, v_out)


def kernel(x_prompt, x_sample, state_ret, state_conv, c_prompt, c_sample, w_ada, b_ada, norm_pre1, norm_post1, norm_pre2, norm_post2, w_in, sgu_w_s, sgu_b_s, sgu_ln_g, sgu_ln_b, ret_gn_g, w_branch_a, w_branch_b, w_out, ffn_w_gate, ffn_w_up, ffn_conv_w, ffn_conv_b, ffn_w_down):
    w = dict(norm_pre1=norm_pre1, norm_post1=norm_post1, norm_pre2=norm_pre2, norm_post2=norm_post2, w_in=w_in,
             sgu_ln_g=sgu_ln_g, sgu_ln_b=sgu_ln_b, ret_gn_g=ret_gn_g, w_branch_a=w_branch_a,
             w_branch_b=w_branch_b, w_out=w_out, ffn_w_gate=ffn_w_gate, ffn_w_up=ffn_w_up,
             ffn_conv_w=ffn_conv_w, ffn_conv_b=ffn_conv_b, ffn_w_down=ffn_w_down)
    depth = w_ada.shape[0]
    nb, seq, d = x_prompt.shape
    ndb, dseq, _ = x_sample.shape

    mod_p, mod_s = _modulation(c_prompt, c_sample, w_ada, b_ada)

    sgu_wp, sgu_bp = _sgu_tables(sgu_w_s, sgu_b_s, seq)
    sgu_ws, sgu_bs = _sgu_tables(sgu_w_s, sgu_b_s, dseq)
    cfg_p = dict(tm=512, bs=1, tl=512, pos0=0, sgu_w=sgu_wp, sgu_b=sgu_bp, emit_v=False)
    cfg_s = dict(tm=512, bs=32, tl=dseq, pos0=PAST_LEN, ret_bs=8, sgu_w=sgu_ws, sgu_b=sgu_bs, emit_v=True)

    xp, xs = x_prompt, x_sample
    hp = _prenorm(xp, norm_pre1, mod_p, 0, K_SC1, K_SH1, cfg_p["bs"], cfg_p["tl"])
    hs = _prenorm(xs, norm_pre1, mod_s, 0, K_SC1, K_SH1, cfg_s["bs"], cfg_s["tl"])
    outs_p = outs_s = (None, None, None)
    for l in range(depth):
        xp, hp, outs_p = _layer(l, depth, xp, hp, mod_p, w, cfg_p, None, None, outs_p)
        xs, hs, outs_s = _layer(l, depth, xs, hs, mod_s, w, cfg_s, state_ret, state_conv, outs_s)
    ret_p, conv_p, _ = outs_p
    ret_s, conv_s, v_s = outs_s
    return (xp, xs, ret_p, ret_s, conv_p, conv_s, v_s.reshape(depth, ndb, dseq, -1))
```

```python
import functools
import math

import jax
import jax.numpy as jnp
from jax import lax
from jax.experimental import pallas as pl
from jax.experimental.pallas import tpu as pltpu

F32 = jnp.float32
BF16 = jnp.bfloat16

EPS = 1e-6
ROPE_BASE = 10000.0
PAST_LEN = 16384
SGU_GROUPS = 8
SGU_CHUNK = 128
RET_HEADS = 8
RET_CHUNK = 128
CONV_W = 3

SUBLANES = 8
BF16_ROWS = 16
ROW_CHUNK = 128
VMEM_LIMIT_BYTES = 56 * 1024 * 1024


def _params(n_axes):
    return pltpu.CompilerParams(dimension_semantics=("arbitrary",) * n_axes,
                                vmem_limit_bytes=VMEM_LIMIT_BYTES)


def _carried(kernel, n_in, prev):
    if prev is None:
        return kernel, [], [], {}

    def body(*refs):
        kernel(*refs[:n_in], *refs[n_in + 1:])
    return body, [pl.BlockSpec(memory_space=pl.ANY)], [prev], {n_in: 1}


def _rms(x, g):
    return x * lax.rsqrt(jnp.mean(x * x, axis=-1, keepdims=True) + EPS) * g


def _silu(x):
    return x * jax.nn.sigmoid(x)


def _mod_kernel(c_ref, w_ref, b_ref, op_ref, os_ref, *, n_p, s0):
    a = _silu(c_ref[...]).astype(BF16)
    m = jnp.dot(a, w_ref[...].astype(BF16), preferred_element_type=F32) + b_ref[...]
    op_ref[...] = m[0:n_p][:, None, :]
    os_ref[...] = m[s0:][:, None, :]


def _modulation(c_prompt, c_sample, w_ada, b_ada, tn=1024):
    depth, d, n = w_ada.shape
    n_p, n_s = c_prompt.shape[0], c_sample.shape[0]
    s0 = -(-n_p // BF16_ROWS) * BF16_ROWS
    c_all = jnp.concatenate([c_prompt, jnp.zeros((s0 - n_p, d), F32), c_sample], axis=0)
    rows = c_all.shape[0]
    assert rows % BF16_ROWS == 0
    return pl.pallas_call(
        functools.partial(_mod_kernel, n_p=n_p, s0=s0),
        grid=(depth, n // tn),
        in_specs=[pl.BlockSpec((rows, d), lambda l, j: (0, 0)),
                  pl.BlockSpec((None, d, tn), lambda l, j: (l, 0, j)),
                  pl.BlockSpec((None, 1, tn), lambda l, j: (l, 0, j))],
        out_specs=[pl.BlockSpec((None, n_p, 1, tn), lambda l, j: (l, 0, 0, j)),
                   pl.BlockSpec((None, n_s, 1, tn), lambda l, j: (l, 0, 0, j))],
        out_shape=[jax.ShapeDtypeStruct((depth, n_p, 1, n), F32),
                   jax.ShapeDtypeStruct((depth, n_s, 1, n), F32)],
        compiler_params=_params(2),
        name="adaln_mod",
    )(c_all, w_ada, b_ada.reshape(depth, 1, n))


def _flat_bf16(h):
    return h.reshape(h.shape[0] * h.shape[1], h.shape[2]).astype(BF16)


def _prenorm_kernel(x_ref, g_ref, sc_ref, sh_ref, h_ref):
    h = _rms(x_ref[...], g_ref[...]) * (1.0 + sc_ref[...]) + sh_ref[...]
    h_ref[...] = _flat_bf16(h)


def _post_kernel(t_ref, x_ref, gpost_ref, gate_ref, gpre_ref, sc_ref, sh_ref, xo_ref, h_ref):
    x = x_ref[...] + gate_ref[...] * _rms(t_ref[...], gpost_ref[...])
    xo_ref[...] = x
    h = _rms(x, gpre_ref[...]) * (1.0 + sc_ref[...]) + sh_ref[...]
    h_ref[...] = _flat_bf16(h)


def _post_last_kernel(t_ref, x_ref, gpost_ref, gate_ref, xo_ref):
    xo_ref[...] = x_ref[...] + gate_ref[...] * _rms(t_ref[...], gpost_ref[...])


def _row_specs(nseq, seq_len, d, bs, tl):
    grid = (nseq // bs, seq_len // tl)
    act = pl.BlockSpec((bs, tl, d), lambda b, t: (b, t, 0))
    flat = pl.BlockSpec((bs * tl, d), lambda b, t: (b * (seq_len // tl) + t, 0))

    def vec(l):
        return pl.BlockSpec((1, 1, d), lambda b, t: (l, 0, 0))

    def mod(l, k):
        return pl.BlockSpec((None, bs, 1, d), lambda b, t: (l, b, 0, k))
    return grid, act, vec, flat, mod


def _vec3(p):
    return p.reshape(p.shape[0], 1, p.shape[1])


def _prenorm(x, g_pre, mod, l, k_sc, k_sh, bs, tl):
    nseq, seq_len, d = x.shape
    grid, act, vec, flat, modspec = _row_specs(nseq, seq_len, d, bs, tl)
    return pl.pallas_call(
        _prenorm_kernel, grid=grid,
        in_specs=[act, vec(l), modspec(l, k_sc), modspec(l, k_sh)],
        out_specs=flat,
        out_shape=jax.ShapeDtypeStruct((nseq * seq_len, d), BF16),
        compiler_params=_params(2), name="prenorm",
    )(x, _vec3(g_pre), mod, mod)


def _post(t, x, g_post, mod, l, k_gate, bs, tl, nxt=None):
    nseq, seq_len, d = x.shape
    grid, act, vec, flat, modspec = _row_specs(nseq, seq_len, d, bs, tl)
    t3 = t.reshape(nseq, seq_len, d)
    x_shape = jax.ShapeDtypeStruct((nseq, seq_len, d), F32)
    if nxt is None:
        return pl.pallas_call(
            _post_last_kernel, grid=grid,
            in_specs=[act, act, vec(l), modspec(l, k_gate)],
            out_specs=act, out_shape=x_shape,
            compiler_params=_params(2), name="post_last",
        )(t3, x, _vec3(g_post), mod), None
    g_pre, ln, k_sc, k_sh = nxt
    return pl.pallas_call(
        _post_kernel, grid=grid,
        in_specs=[act, act, vec(l), modspec(l, k_gate), vec(ln), modspec(ln, k_sc), modspec(ln, k_sh)],
        out_specs=[act, flat],
        out_shape=[x_shape, jax.ShapeDtypeStruct((nseq * seq_len, d), BF16)],
        compiler_params=_params(2), name="post",
    )(t3, x, _vec3(g_post), mod, _vec3(g_pre), mod, mod)


def _mm_kernel(x_ref, w_ref, o_ref, wb_ref):
    @pl.when(pl.program_id(1) == 0)
    def _():
        wb_ref[...] = w_ref[...].astype(BF16)
    o_ref[...] = jnp.dot(x_ref[...], wb_ref[...], preferred_element_type=F32).astype(o_ref.dtype)


def _matmul(x, w, l, tm, tn, out_dtype):
    rows, k = x.shape
    n = w.shape[2]
    return pl.pallas_call(
        _mm_kernel, grid=(n // tn, rows // tm),
        in_specs=[pl.BlockSpec((tm, k), lambda j, i: (i, 0)),
                  pl.BlockSpec((None, k, tn), lambda j, i: (l, 0, j))],
        out_specs=pl.BlockSpec((tm, tn), lambda j, i: (i, j)),
        out_shape=jax.ShapeDtypeStruct((rows, n), out_dtype),
        scratch_shapes=[pltpu.VMEM((k, tn), BF16)],
        compiler_params=_params(2), name="matmul",
    )(x, w)


IN_TILE = 1024
J_U, J_V, J_Q, J_K, J_VR, J_GR, J_GA, J_GB = 0, 1, 2, 3, 4, 5, 6, 8


def _inproj_kernel(h_ref, w_ref, lng_ref, lnb_ref, o_ref, wb_ref, *, tm, rc, seq_len, pos0, head_dim):
    j = pl.program_id(0)
    i = pl.program_id(1)

    @pl.when(i == 0)
    def _():
        wb_ref[...] = w_ref[...].astype(BF16)

    def run(epilogue):
        for c in range(tm // rc):
            rows = slice(c * rc, (c + 1) * rc)
            acc = jnp.dot(h_ref[rows, :], wb_ref[...], preferred_element_type=F32)
            o_ref[rows, :] = epilogue(acc, c).astype(BF16)

    def layer_norm_gelu(acc, c):
        v = jax.nn.gelu(acc)
        mu = jnp.mean(v, axis=-1, keepdims=True)
        var = jnp.mean(jnp.square(v - mu), axis=-1, keepdims=True)
        return (v - mu) * lax.rsqrt(var + EPS) * lng_ref[...] + lnb_ref[...]

    def rope(acc, c):
        half = head_dim // 2
        lane = lax.broadcasted_iota(jnp.int32, (rc, head_dim), 1)
        row = lax.broadcasted_iota(jnp.int32, (rc, head_dim), 0)
        pos = pos0 + ((i * tm + c * rc + row) & (seq_len - 1))
        inv = jnp.exp((lane & (half - 1)).astype(F32) * (-math.log(ROPE_BASE) / half))
        ang = pos.astype(F32) * inv
        scale = jnp.where(j == J_K, head_dim ** -0.5, 1.0).astype(F32)
        cos = jnp.cos(ang) * scale
        sin = jnp.where(lane < half, -jnp.sin(ang), jnp.sin(ang)) * scale
        heads = [acc[:, h * head_dim:(h + 1) * head_dim] for h in range(IN_TILE // head_dim)]
        return jnp.concatenate([xh * cos + pltpu.roll(xh, half, axis=1) * sin for xh in heads], axis=1)

    pl.when(j == J_U)(lambda: run(lambda acc, c: jax.nn.gelu(acc)))
    pl.when(j == J_V)(lambda: run(layer_norm_gelu))
    pl.when((j == J_Q) | (j == J_K))(lambda: run(rope))
    pl.when(j == J_VR)(lambda: run(lambda acc, c: acc))
    pl.when(j == J_GR)(lambda: run(lambda acc, c: _silu(acc)))
    pl.when(j >= J_GA)(lambda: run(lambda acc, c: jax.nn.sigmoid(acc)))


def _inproj(h, w_in, ln_g, ln_b, l, tm, seq_len, pos0):
    rows, k = h.shape
    n = w_in.shape[2]
    assert seq_len & (seq_len - 1) == 0 and n == 10 * IN_TILE
    kern = functools.partial(_inproj_kernel, tm=tm, rc=ROW_CHUNK, seq_len=seq_len, pos0=pos0,
                             head_dim=IN_TILE // RET_HEADS)
    vec = pl.BlockSpec((None, 1, IN_TILE), lambda j, i: (l, 0, 0))
    return pl.pallas_call(
        kern, grid=(n // IN_TILE, rows // tm),
        in_specs=[pl.BlockSpec((tm, k), lambda j, i: (i, 0)),
                  pl.BlockSpec((None, k, IN_TILE), lambda j, i: (l, 0, j)),
                  vec, vec],
        out_specs=pl.BlockSpec((tm, IN_TILE), lambda j, i: (i, j)),
        out_shape=jax.ShapeDtypeStruct((rows, n), BF16),
        scratch_shapes=[pltpu.VMEM((k, IN_TILE), BF16)],
        compiler_params=_params(2), name="inproj",
    )(h, w_in, _vec3(ln_g), _vec3(ln_b))


def _sgu_kernel(u_ref, v_ref, w_ref, b_ref, o_ref, *vo_ref, tm, chunk):
    if vo_ref:
        vo_ref[0][...] = v_ref[...].astype(F32)
    r = lax.broadcasted_iota(jnp.int32, (SGU_CHUNK, SGU_CHUNK), 0)
    c = lax.broadcasted_iota(jnp.int32, (SGU_CHUNK, SGU_CHUNK), 1)
    mask = (c <= r) & ((r & -chunk) == (c & -chunk))
    gd = IN_TILE // SGU_GROUPS
    for g in range(SGU_GROUPS):
        w = jnp.where(mask, w_ref[g], 0.0).astype(BF16)
        bias = b_ref[:, g:g + 1]
        for t in range(tm // SGU_CHUNK):
            rs = slice(t * SGU_CHUNK, (t + 1) * SGU_CHUNK)
            cs = slice(g * gd, (g + 1) * gd)
            z = jnp.dot(w, v_ref[rs, cs], preferred_element_type=F32) + bias
            o_ref[rs, cs] = (u_ref[rs, cs].astype(F32) * z).astype(BF16)


def _sgu_tables(w_s, b_s, seq_len):
    chunk = min(SGU_CHUNK, seq_len)
    reps = SGU_CHUNK // chunk
    w_t = jnp.tile(w_s[:, :, :chunk, :chunk], (1, 1, reps, reps))
    b_t = jnp.tile(jnp.swapaxes(b_s[:, :, :chunk], 1, 2), (1, reps, 1))
    return w_t, b_t


def _sgu(proj, w_t, b_t, l, depth, seq_len, tm, emit_v, prev):
    rows = proj.shape[0]
    kern = functools.partial(_sgu_kernel, tm=tm, chunk=min(SGU_CHUNK, seq_len))
    in_specs = [pl.BlockSpec((tm, IN_TILE), lambda i: (i, J_U)),
                pl.BlockSpec((tm, IN_TILE), lambda i: (i, J_V)),
                pl.BlockSpec((None, SGU_GROUPS, SGU_CHUNK, SGU_CHUNK), lambda i: (l, 0, 0, 0)),
                pl.BlockSpec((None, SGU_CHUNK, SGU_GROUPS), lambda i: (l, 0, 0))]
    out_specs = [pl.BlockSpec((tm, IN_TILE), lambda i: (i, 0))]
    out_shape = [jax.ShapeDtypeStruct((rows, IN_TILE), BF16)]
    cspecs, cargs, aliases = [], [], {}
    if emit_v:
        kern, cspecs, cargs, aliases = _carried(kern, 4, prev)
        out_specs.append(pl.BlockSpec((None, tm, IN_TILE), lambda i: (l, i, 0)))
        out_shape.append(jax.ShapeDtypeStruct((depth, rows, IN_TILE), F32))
    out = pl.pallas_call(
        kern, grid=(rows // tm,),
        in_specs=in_specs + cspecs, out_specs=out_specs, out_shape=out_shape,
        input_output_aliases=aliases,
        compiler_params=_params(1), name="sgu",
    )(proj, proj, w_t, b_t, *cargs)
    return (out[0], out[1]) if emit_v else (out[0], None)


def _log_gamma(h):
    return math.log(1.0 - 2.0 ** (-5.0 - h))


def _ret_chunk(qh, kh, vh, grh, gnh, s, h, c):
    lg = _log_gamma(h)
    ri = lax.broadcasted_iota(jnp.int32, (c, c), 0)
    ci = lax.broadcasted_iota(jnp.int32, (c, c), 1)
    diff = (ri - ci).astype(F32)
    decay = jnp.where(ri >= ci, jnp.exp(lg * jnp.maximum(diff, 0.0)), 0.0)
    idx = lax.broadcasted_iota(jnp.int32, (c, 1), 0).astype(F32)
    q_dec = jnp.exp(lg * (idx + 1.0))
    k_dec = jnp.exp(lg * (c - 1.0 - idx))
    scores = lax.dot_general(qh, kh, (((1,), (1,)), ((), ())), preferred_element_type=F32) * decay
    inner = jnp.dot(scores.astype(BF16), vh, preferred_element_type=F32)
    cross = jnp.dot(qh, s.astype(BF16), preferred_element_type=F32) * q_dec
    kd = (kh.astype(F32) * k_dec).astype(BF16)
    s_new = s * math.exp(lg * c) + lax.dot_general(kd, vh, (((0,), (0,)), ((), ())), preferred_element_type=F32)
    o = inner + cross
    mu = jnp.mean(o, axis=-1, keepdims=True)
    var = jnp.mean(jnp.square(o - mu), axis=-1, keepdims=True)
    y = (o - mu) * lax.rsqrt(var + EPS) * gnh
    return grh * y, s_new


def _ret_prompt_kernel(q_ref, k_ref, v_ref, gr_ref, gn_ref, o_ref, so_ref, s_ref, *, hd):
    n = pl.program_id(1)

    @pl.when(n == 0)
    def _():
        s_ref[...] = jnp.zeros_like(s_ref)

    for h in range(RET_HEADS):
        cs = slice(h * hd, (h + 1) * hd)
        y, s_new = _ret_chunk(q_ref[:, cs], k_ref[:, cs], v_ref[:, cs], gr_ref[:, cs].astype(F32),
                              gn_ref[:, cs], s_ref[h], h, RET_CHUNK)
        o_ref[:, cs] = y.astype(BF16)
        s_ref[h] = s_new

    @pl.when(n == pl.num_programs(1) - 1)
    def _():
        so_ref[...] = s_ref[...]


def _ret_prompt(proj, gn_g, l, depth, nseq, seq_len, prev):
    hd = IN_TILE // RET_HEADS
    nc = seq_len // RET_CHUNK

    def col(jc):
        return pl.BlockSpec((RET_CHUNK, IN_TILE), lambda b, n: (b * nc + n, jc))
    kern, cspecs, cargs, aliases = _carried(functools.partial(_ret_prompt_kernel, hd=hd), 5, prev)
    return pl.pallas_call(
        kern, grid=(nseq, nc),
        in_specs=[col(J_Q), col(J_K), col(J_VR), col(J_GR),
                  pl.BlockSpec((None, 1, IN_TILE), lambda b, n: (l, 0, 0))] + cspecs,
        out_specs=[pl.BlockSpec((RET_CHUNK, IN_TILE), lambda b, n: (b * nc + n, 0)),
                   pl.BlockSpec((None, None, RET_HEADS, hd, hd), lambda b, n: (l, b, 0, 0, 0))],
        out_shape=[jax.ShapeDtypeStruct((nseq * seq_len, IN_TILE), BF16),
                   jax.ShapeDtypeStruct((depth, nseq, RET_HEADS, hd, hd), F32)],
        scratch_shapes=[pltpu.VMEM((RET_HEADS, hd, hd), F32)],
        input_output_aliases=aliases,
        compiler_params=_params(2), name="ret_prompt",
    )(proj, proj, proj, proj, _vec3(gn_g), *cargs)


def _ret_sample_kernel(q_ref, k_ref, v_ref, gr_ref, gn_ref, s_ref, o_ref, so_ref, y_ref, *, hd, bs, seq_len):
    q = q_ref[...].astype(F32)
    k = k_ref[...].astype(F32)
    v = v_ref[...].astype(F32)
    gr = gr_ref[...].astype(F32)
    for b in range(bs):
        rs = slice(b * seq_len, (b + 1) * seq_len)
        for h in range(RET_HEADS):
            cs = slice(h * hd, (h + 1) * hd)
            y, s_new = _ret_chunk(q[rs, cs].astype(BF16), k[rs, cs].astype(BF16), v[rs, cs].astype(BF16),
                                  gr[rs, cs], gn_ref[:, cs], s_ref[b, h], h, seq_len)
            y_ref[rs, cs] = y
            so_ref[b, h] = s_new
    o_ref[...] = y_ref[...].astype(BF16)


def _ret_sample(proj, gn_g, state, l, seq_len, bs, prev):
    nseq = state.shape[1]
    hd = IN_TILE // RET_HEADS

    def col(jc):
        return pl.BlockSpec((bs * seq_len, IN_TILE), lambda i: (i, jc))
    st = pl.BlockSpec((None, bs, RET_HEADS, hd, hd), lambda i: (l, i, 0, 0, 0))
    kern, cspecs, cargs, aliases = _carried(
        functools.partial(_ret_sample_kernel, hd=hd, bs=bs, seq_len=seq_len), 6, prev)
    return pl.pallas_call(
        kern, grid=(nseq // bs,),
        in_specs=[col(J_Q), col(J_K), col(J_VR), col(J_GR),
                  pl.BlockSpec((None, 1, IN_TILE), lambda i: (l, 0, 0)), st] + cspecs,
        out_specs=[pl.BlockSpec((bs * seq_len, IN_TILE), lambda i: (i, 0)), st],
        out_shape=[jax.ShapeDtypeStruct((nseq * seq_len, IN_TILE), BF16),
                   jax.ShapeDtypeStruct(state.shape, F32)],
        scratch_shapes=[pltpu.VMEM((bs * seq_len, IN_TILE), F32)],
        input_output_aliases=aliases,
        compiler_params=_params(1), name="ret_sample",
    )(proj, proj, proj, proj, _vec3(gn_g), state, *cargs)


def _merge_kernel(ya_ref, yb_ref, ga_ref, gb_ref, wa_ref, wb_ref, o_ref, wab_ref, wbb_ref):
    @pl.when(pl.program_id(1) == 0)
    def _():
        wab_ref[...] = wa_ref[...].astype(BF16)
        wbb_ref[...] = wb_ref[...].astype(BF16)
    for c in range(o_ref.shape[0] // ROW_CHUNK):
        rows = slice(c * ROW_CHUNK, (c + 1) * ROW_CHUNK)
        a = jnp.dot(ya_ref[rows, :], wab_ref[...], preferred_element_type=F32)
        b = jnp.dot(yb_ref[rows, :], wbb_ref[...], preferred_element_type=F32)
        o_ref[rows, :] = (ga_ref[rows, :].astype(F32) * a + gb_ref[rows, :].astype(F32) * b).astype(BF16)


def _merge(ya, yb, proj, w_a, w_b, l, tm):
    rows, k = ya.shape
    n = w_a.shape[2]
    tn = IN_TILE
    row = pl.BlockSpec((tm, k), lambda j, i: (i, 0))
    wsp = pl.BlockSpec((None, k, tn), lambda j, i: (l, 0, j))
    return pl.pallas_call(
        _merge_kernel, grid=(n // tn, rows // tm),
        in_specs=[row, row,
                  pl.BlockSpec((tm, tn), lambda j, i: (i, J_GA + j)),
                  pl.BlockSpec((tm, tn), lambda j, i: (i, J_GB + j)),
                  wsp, wsp],
        out_specs=pl.BlockSpec((tm, tn), lambda j, i: (i, j)),
        out_shape=jax.ShapeDtypeStruct((rows, n), BF16),
        scratch_shapes=[pltpu.VMEM((k, tn), BF16), pltpu.VMEM((k, tn), BF16)],
        compiler_params=_params(2), name="merge",
    )(ya, yb, proj, proj, w_a, w_b)


def _ffn_up_kernel(h_ref, wg_ref, wu_ref, cw_ref, cb_ref, *rest, tm, rc, seq_len, has_state):
    if has_state:
        st_ref, y_ref, so_ref, wgb_ref, wub_ref, gbuf_ref = rest
    else:
        y_ref, so_ref, wgb_ref, wub_ref, gbuf_ref = rest
    i = pl.program_id(1)
    pad = SUBLANES
    keep = CONV_W - 1

    @pl.when(i == 0)
    def _():
        wgb_ref[...] = wg_ref[...].astype(BF16)
        wub_ref[...] = wu_ref[...].astype(BF16)

    tn = y_ref.shape[1]
    if has_state:
        gbuf_ref[0:pad, :] = jnp.zeros((pad, tn), F32)
    else:
        @pl.when(((i * tm) & (seq_len - 1)) == 0)
        def _():
            gbuf_ref[0:pad, :] = jnp.zeros((pad, tn), F32)
    for c in range(tm // rc):
        r0 = c * rc
        h = h_ref[r0:r0 + rc, :]
        g = jnp.dot(h, wgb_ref[...], preferred_element_type=F32)
        up = jnp.dot(h, wub_ref[...], preferred_element_type=F32)
        gbuf_ref[pad + r0:pad + r0 + rc, :] = g
        g1 = gbuf_ref[pad - 1 + r0:pad - 1 + r0 + rc, :]
        g2 = gbuf_ref[pad - 2 + r0:pad - 2 + r0 + rc, :]
        if has_state:
            bs = rc // seq_len
            st = st_ref[c * bs:(c + 1) * bs]
            s0 = jnp.broadcast_to(st[:, 0:1, :], (bs, seq_len, tn)).reshape(rc, tn)
            s1 = jnp.broadcast_to(st[:, 1:2, :], (bs, seq_len, tn)).reshape(rc, tn)
            t = lax.broadcasted_iota(jnp.int32, (rc, tn), 0) & (seq_len - 1)
            g1 = jnp.where(t == 0, s1, g1)
            g2 = jnp.where(t == 0, s0, jnp.where(t == 1, s1, g2))
            so_ref[c * bs:(c + 1) * bs] = g.reshape(bs, seq_len, tn)[:, seq_len - keep:, :]
        conv = g2 * cw_ref[0:1, :] + g1 * cw_ref[1:2, :] + g * cw_ref[2:3, :] + cb_ref[...]
        y_ref[r0:r0 + rc, :] = (jax.nn.gelu(conv) * up).astype(BF16)
    if not has_state:
        last = gbuf_ref[pad + tm - keep:pad + tm, :]
        gbuf_ref[pad - keep:pad, :] = last
        so_ref[...] = last


def _ffn_up(h, w_gate, w_up, conv_w, conv_b, l, depth, nseq, seq_len, tm, tn, state, prev):
    rows, k = h.shape
    n = w_gate.shape[2]
    keep = CONV_W - 1
    has_state = state is not None
    assert seq_len & (seq_len - 1) == 0 and seq_len >= CONV_W
    kern = functools.partial(_ffn_up_kernel, tm=tm, rc=ROW_CHUNK, seq_len=seq_len, has_state=has_state)
    wsp = pl.BlockSpec((None, k, tn), lambda j, i: (l, 0, j))
    in_specs = [pl.BlockSpec((tm, k), lambda j, i: (i, 0)), wsp, wsp,
                pl.BlockSpec((None, CONV_W, tn), lambda j, i: (l, 0, j)),
                pl.BlockSpec((None, 1, tn), lambda j, i: (l, 0, j))]
    args = [h, w_gate, w_up, conv_w, _vec3(conv_b)]
    if has_state:
        bs = tm // seq_len
        so_spec = pl.BlockSpec((None, bs, keep, tn), lambda j, i: (l, i, 0, j))
        in_specs.append(so_spec)
        args.append(state)
    else:
        tps = seq_len // tm
        so_spec = pl.BlockSpec((None, None, keep, tn), lambda j, i: (l, i // tps, 0, j))
    kern, cspecs, cargs, aliases = _carried(kern, len(args), prev)
    return pl.pallas_call(
        kern, grid=(n // tn, rows // tm),
        in_specs=in_specs + cspecs,
        out_specs=[pl.BlockSpec((tm, tn), lambda j, i: (i, j)), so_spec],
        out_shape=[jax.ShapeDtypeStruct((rows, n), BF16),
                   jax.ShapeDtypeStruct((depth, nseq, keep, n), F32)],
        scratch_shapes=[pltpu.VMEM((k, tn), BF16), pltpu.VMEM((k, tn), BF16),
                        pltpu.VMEM((tm + SUBLANES, tn), F32)],
        input_output_aliases=aliases,
        compiler_params=_params(2), name="ffn_up",
    )(*args, *cargs)


K_SH1, K_SC1, K_G1, K_SH2, K_SC2, K_G2 = range(6)


def _layer(l, depth, x, h, mod, w, cfg, s_ret, conv_prev, outs):
    nseq, seq_len, d = x.shape
    tm, bs, tl = cfg["tm"], cfg["bs"], cfg["tl"]
    ret_out, conv_out, v_out = outs

    proj = _inproj(h, w["w_in"], w["sgu_ln_g"], w["sgu_ln_b"], l, tm, seq_len, cfg["pos0"])
    ya, v_out = _sgu(proj, cfg["sgu_w"], cfg["sgu_b"], l, depth, seq_len, tm, cfg["emit_v"], v_out)
    if s_ret is None:
        yb, ret_out = _ret_prompt(proj, w["ret_gn_g"], l, depth, nseq, seq_len, ret_out)
    else:
        yb, ret_out = _ret_sample(proj, w["ret_gn_g"], s_ret, l, seq_len, cfg["ret_bs"], ret_out)
    merged = _merge(ya, yb, proj, w["w_branch_a"], w["w_branch_b"], l, tm)
    t = _matmul(merged, w["w_out"], l, tm, 1024, F32)
    x, h2 = _post(t, x, w["norm_post1"], mod, l, K_G1, bs, tl, nxt=(w["norm_pre2"], l, K_SC2, K_SH2))

    y, conv_out = _ffn_up(h2, w["ffn_w_gate"], w["ffn_w_up"], w["ffn_conv_w"], w["ffn_conv_b"], l, depth,
                          nseq, seq_len, tm, 512, conv_prev, conv_out)
    f = _matmul(y, w["ffn_w_down"], l, cfg["tm_down"], 512, F32)
    nxt = None if l == depth - 1 else (w["norm_pre1"], l + 1, K_SC1, K_SH1)
    x, h_next = _post(f, x, w["norm_post2"], mod, l, K_G2, bs, tl, nxt=nxt)
    return x, h_next, (ret_out, conv_out, v_out)


def kernel(x_prompt, x_sample, state_ret, state_conv, c_prompt, c_sample, w_ada, b_ada, norm_pre1, norm_post1, norm_pre2, norm_post2, w_in, sgu_w_s, sgu_b_s, sgu_ln_g, sgu_ln_b, ret_gn_g, w_branch_a, w_branch_b, w_out, ffn_w_gate, ffn_w_up, ffn_conv_w, ffn_conv_b, ffn_w_down):
    w = dict(norm_pre1=norm_pre1, norm_post1=norm_post1, norm_pre2=norm_pre2, norm_post2=norm_post2, w_in=w_in,
             sgu_ln_g=sgu_ln_g, sgu_ln_b=sgu_ln_b, ret_gn_g=ret_gn_g, w_branch_a=w_branch_a,
             w_branch_b=w_branch_b, w_out=w_out, ffn_w_gate=ffn_w_gate, ffn_w_up=ffn_w_up,
             ffn_conv_w=ffn_conv_w, ffn_conv_b=ffn_conv_b, ffn_w_down=ffn_w_down)
    depth = w_ada.shape[0]
    nb, seq, d = x_prompt.shape
    ndb, dseq, _ = x_sample.shape

    mod_p, mod_s = _modulation(c_prompt, c_sample, w_ada, b_ada)

    sgu_wp, sgu_bp = _sgu_tables(sgu_w_s, sgu_b_s, seq)
    sgu_ws, sgu_bs = _sgu_tables(sgu_w_s, sgu_b_s, dseq)
    cfg_p = dict(tm=1024, tm_down=512, bs=1, tl=512, pos0=0, sgu_w=sgu_wp, sgu_b=sgu_bp, emit_v=False)
    cfg_s = dict(tm=512, tm_down=512, bs=32, tl=dseq, pos0=PAST_LEN, ret_bs=8, sgu_w=sgu_ws, sgu_b=sgu_bs,
                 emit_v=True)

    xp, xs = x_prompt, x_sample
    hp = _prenorm(xp, norm_pre1, mod_p, 0, K_SC1, K_SH1, cfg_p["bs"], cfg_p["tl"])
    hs = _prenorm(xs, norm_pre1, mod_s, 0, K_SC1, K_SH1, cfg_s["bs"], cfg_s["tl"])
    outs_p = outs_s = (None, None, None)
    for l in range(depth):
        xp, hp, outs_p = _layer(l, depth, xp, hp, mod_p, w, cfg_p, None, None, outs_p)
        xs, hs, outs_s = _layer(l, depth, xs, hs, mod_s, w, cfg_s, state_ret, state_conv, outs_s)
    ret_p, conv_p, _ = outs_p
    ret_s, conv_s, v_s = outs_s
    return (xp, xs, ret_p, ret_s, conv_p, conv_s, v_s.reshape(depth, ndb, dseq, -1))
```

```python
import functools
import math

import jax
import jax.numpy as jnp
from jax import lax
from jax.experimental import pallas as pl
from jax.experimental.pallas import tpu as pltpu

F32 = jnp.float32
BF16 = jnp.bfloat16

EPS = 1e-6
ROPE_BASE = 10000.0
PAST_LEN = 16384
SGU_GROUPS = 8
SGU_CHUNK = 128
RET_HEADS = 8
RET_CHUNK = 128
CONV_W = 3

SUBLANES = 8
BF16_ROWS = 16
ROW_CHUNK = 128
VMEM_LIMIT_BYTES = 56 * 1024 * 1024


def _params(n_axes):
    return pltpu.CompilerParams(dimension_semantics=("arbitrary",) * n_axes,
                                vmem_limit_bytes=VMEM_LIMIT_BYTES)


def _stacked(kernel, n_in, prev, depth, l):
    if prev is None:
        def first(*refs):
            refs = list(refs)
            so = refs[n_in + 1]
            for k in range(depth):
                if k != l:
                    so[k] = jnp.zeros(so.shape[1:], so.dtype)
            refs[n_in + 1] = so.at[l]
            kernel(*refs)
        return first, [], [], {}

    def later(*refs):
        kernel(*refs[:n_in], *refs[n_in + 1:])
    return later, [pl.BlockSpec(memory_space=pl.ANY)], [prev], {n_in: 1}


def _stacked_spec(prev, depth, l, block, index_fn):
    if prev is None:
        return pl.BlockSpec((depth,) + block, lambda *g: (0,) + index_fn(*g))
    return pl.BlockSpec((None,) + block, lambda *g: (l,) + index_fn(*g))


def _rms(x, g):
    return x * lax.rsqrt(jnp.mean(x * x, axis=-1, keepdims=True) + EPS) * g


def _silu(x):
    return x * jax.nn.sigmoid(x)


def _vec3(p):
    return p.reshape(p.shape[0], 1, p.shape[1])


def _mod_kernel(c_ref, w_ref, b_ref, op_ref, os_ref, *, n_p, s0):
    a = _silu(c_ref[...]).astype(BF16)
    m = jnp.dot(a, w_ref[...].astype(BF16), preferred_element_type=F32) + b_ref[...]
    op_ref[...] = m[0:n_p][:, None, :]
    os_ref[...] = m[s0:][:, None, :]


def _modulation(c_prompt, c_sample, w_ada, b_ada, tn=1024):
    depth, d, n = w_ada.shape
    n_p, n_s = c_prompt.shape[0], c_sample.shape[0]
    s0 = -(-n_p // BF16_ROWS) * BF16_ROWS
    c_all = jnp.concatenate([c_prompt, jnp.zeros((s0 - n_p, d), F32), c_sample], axis=0)
    rows = c_all.shape[0]
    assert rows % BF16_ROWS == 0
    return pl.pallas_call(
        functools.partial(_mod_kernel, n_p=n_p, s0=s0),
        grid=(depth, n // tn),
        in_specs=[pl.BlockSpec((rows, d), lambda l, j: (0, 0)),
                  pl.BlockSpec((None, d, tn), lambda l, j: (l, 0, j)),
                  pl.BlockSpec((None, 1, tn), lambda l, j: (l, 0, j))],
        out_specs=[pl.BlockSpec((None, n_p, 1, tn), lambda l, j: (l, 0, 0, j)),
                   pl.BlockSpec((None, n_s, 1, tn), lambda l, j: (l, 0, 0, j))],
        out_shape=[jax.ShapeDtypeStruct((depth, n_p, 1, n), F32),
                   jax.ShapeDtypeStruct((depth, n_s, 1, n), F32)],
        compiler_params=_params(2),
        name="adaln_mod",
    )(c_all, w_ada, b_ada.reshape(depth, 1, n))


K_SH1, K_SC1, K_G1, K_SH2, K_SC2, K_G2 = range(6)


def _mod_spec(l, k, tm, seq_len, d, grid_rank):
    if seq_len >= tm:
        tps = seq_len // tm
        return pl.BlockSpec((None, 1, 1, d), lambda *g: (l, g[grid_rank - 1] // tps, 0, k))
    return pl.BlockSpec((None, tm // seq_len, 1, d), lambda *g: (l, g[grid_rank - 1], 0, k))


def _rows_of(m_ref, r0, nrows, seq_len):
    d = m_ref.shape[-1]
    if m_ref.shape[0] == 1:
        return m_ref[0]
    nb = nrows // seq_len
    m = m_ref[r0 // seq_len:r0 // seq_len + nb]
    return jnp.broadcast_to(m, (nb, seq_len, d)).reshape(nrows, d)


def _prenorm_kernel(x_ref, g_ref, sc_ref, sh_ref, h_ref, *, seq_len):
    tm = x_ref.shape[0]
    sc = _rows_of(sc_ref, 0, tm, seq_len)
    sh = _rows_of(sh_ref, 0, tm, seq_len)
    h_ref[...] = (_rms(x_ref[...], g_ref[...]) * (1.0 + sc) + sh).astype(BF16)


def _prenorm(x, g_pre, mod, l, k_sc, k_sh, seq_len, tm):
    rows, d = x.shape
    act = pl.BlockSpec((tm, d), lambda i: (i, 0))
    return pl.pallas_call(
        functools.partial(_prenorm_kernel, seq_len=seq_len), grid=(rows // tm,),
        in_specs=[act, pl.BlockSpec((None, 1, d), lambda i: (l, 0, 0)),
                  _mod_spec(l, k_sc, tm, seq_len, d, 1), _mod_spec(l, k_sh, tm, seq_len, d, 1)],
        out_specs=act,
        out_shape=jax.ShapeDtypeStruct((rows, d), BF16),
        compiler_params=_params(1), name="prenorm",
    )(x, _vec3(g_pre), mod, mod)


def _cast_kernel(w_ref, o_ref):
    o_ref[...] = w_ref[...].astype(BF16)


def _cast_bf16(w, tk=512):
    depth, k, n = w.shape
    spec = pl.BlockSpec((None, tk, n), lambda l, i: (l, i, 0))
    return pl.pallas_call(
        _cast_kernel, grid=(depth, k // tk), in_specs=[spec], out_specs=spec,
        out_shape=jax.ShapeDtypeStruct(w.shape, BF16),
        compiler_params=_params(2), name="cast_bf16",
    )(w)


def _mm_post_kernel(m_ref, w_ref, x_ref, gpost_ref, gate_ref, *rest, tm, rc, seq_len, has_next, cast):
    rest = list(rest)
    if has_next:
        gpre_ref, sc_ref, sh_ref = rest[:3]
        rest = rest[3:]
    xo_ref = rest.pop(0)
    h_ref = rest.pop(0) if has_next else None
    if cast:
        wb_ref = rest.pop(0)

        @pl.when(pl.program_id(0) == 0)
        def _():
            wb_ref[...] = w_ref[...].astype(BF16)
    else:
        wb_ref = w_ref

    for c in range(tm // rc):
        r0 = c * rc
        rows = slice(r0, r0 + rc)
        t = jnp.dot(m_ref[rows, :], wb_ref[...], preferred_element_type=F32)
        x = x_ref[rows, :] + _rows_of(gate_ref, r0, rc, seq_len) * _rms(t, gpost_ref[...])
        xo_ref[rows, :] = x
        if has_next:
            h = (_rms(x, gpre_ref[...]) * (1.0 + _rows_of(sc_ref, r0, rc, seq_len))
                 + _rows_of(sh_ref, r0, rc, seq_len))
            h_ref[rows, :] = h.astype(BF16)


def _matmul_post(m, w, x, g_post, mod, l, k_gate, seq_len, tm, nxt):
    rows, k = m.shape
    d = w.shape[2]
    cast = w.dtype != BF16
    has_next = nxt is not None
    act = pl.BlockSpec((tm, d), lambda i: (i, 0))

    def vec():
        return pl.BlockSpec((None, 1, d), lambda i: (l, 0, 0))
    in_specs = [pl.BlockSpec((tm, k), lambda i: (i, 0)),
                pl.BlockSpec((None, k, d), lambda i: (l, 0, 0), pipeline_mode=pl.Buffered(1)),
                act, vec(), _mod_spec(l, k_gate, tm, seq_len, d, 1)]
    args = [m, w, x, _vec3(g_post), mod]
    out_specs, out_shape = [act], [jax.ShapeDtypeStruct((rows, d), F32)]
    if has_next:
        g_pre, ln, k_sc, k_sh = nxt
        in_specs += [pl.BlockSpec((None, 1, d), lambda i: (ln, 0, 0)),
                     _mod_spec(ln, k_sc, tm, seq_len, d, 1), _mod_spec(ln, k_sh, tm, seq_len, d, 1)]
        args += [_vec3(g_pre), mod, mod]
        out_specs.append(act)
        out_shape.append(jax.ShapeDtypeStruct((rows, d), BF16))
    kern = functools.partial(_mm_post_kernel, tm=tm, rc=ROW_CHUNK, seq_len=seq_len, has_next=has_next, cast=cast)
    out = pl.pallas_call(
        kern, grid=(rows // tm,), in_specs=in_specs, out_specs=out_specs, out_shape=out_shape,
        scratch_shapes=[pltpu.VMEM((k, d), BF16)] if cast else [],
        compiler_params=_params(1), name="matmul_post",
    )(*args)
    return (out[0], out[1]) if has_next else (out[0], None)


IN_TILE = 1024
J_U, J_V, J_Q, J_K, J_VR, J_GR, J_GA, J_GB = 0, 1, 2, 3, 4, 5, 6, 8


def _inproj_kernel(h_ref, w_ref, lng_ref, lnb_ref, o_ref, wb_ref, *, tm, rc, seq_len, pos0, head_dim):
    j = pl.program_id(0)
    i = pl.program_id(1)

    @pl.when(i == 0)
    def _():
        wb_ref[...] = w_ref[...].astype(BF16)

    def run(epilogue):
        for c in range(tm // rc):
            rows = slice(c * rc, (c + 1) * rc)
            acc = jnp.dot(h_ref[rows, :], wb_ref[...], preferred_element_type=F32)
            o_ref[rows, :] = epilogue(acc, c).astype(BF16)

    def layer_norm_gelu(acc, c):
        v = jax.nn.gelu(acc)
        mu = jnp.mean(v, axis=-1, keepdims=True)
        var = jnp.mean(jnp.square(v - mu), axis=-1, keepdims=True)
        return (v - mu) * lax.rsqrt(var + EPS) * lng_ref[...] + lnb_ref[...]

    def rope(acc, c):
        half = head_dim // 2
        lane = lax.broadcasted_iota(jnp.int32, (rc, head_dim), 1)
        row = lax.broadcasted_iota(jnp.int32, (rc, head_dim), 0)
        pos = pos0 + ((i * tm + c * rc + row) & (seq_len - 1))
        inv = jnp.exp((lane & (half - 1)).astype(F32) * (-math.log(ROPE_BASE) / half))
        ang = pos.astype(F32) * inv
        scale = jnp.where(j == J_K, head_dim ** -0.5, 1.0).astype(F32)
        cos = jnp.cos(ang) * scale
        sin = jnp.where(lane < half, -jnp.sin(ang), jnp.sin(ang)) * scale
        heads = [acc[:, h * head_dim:(h + 1) * head_dim] for h in range(IN_TILE // head_dim)]
        return jnp.concatenate([xh * cos + pltpu.roll(xh, half, axis=1) * sin for xh in heads], axis=1)

    pl.when(j == J_U)(lambda: run(lambda acc, c: jax.nn.gelu(acc)))
    pl.when(j == J_V)(lambda: run(layer_norm_gelu))
    pl.when((j == J_Q) | (j == J_K))(lambda: run(rope))
    pl.when(j == J_VR)(lambda: run(lambda acc, c: acc))
    pl.when(j == J_GR)(lambda: run(lambda acc, c: _silu(acc)))
    pl.when(j >= J_GA)(lambda: run(lambda acc, c: jax.nn.sigmoid(acc)))


def _inproj(h, w_in, ln_g, ln_b, l, tm, seq_len, pos0):
    rows, k = h.shape
    n = w_in.shape[2]
    assert seq_len & (seq_len - 1) == 0 and n == 10 * IN_TILE
    kern = functools.partial(_inproj_kernel, tm=tm, rc=ROW_CHUNK, seq_len=seq_len, pos0=pos0,
                             head_dim=IN_TILE // RET_HEADS)
    vec = pl.BlockSpec((None, 1, IN_TILE), lambda j, i: (l, 0, 0))
    return pl.pallas_call(
        kern, grid=(n // IN_TILE, rows // tm),
        in_specs=[pl.BlockSpec((tm, k), lambda j, i: (i, 0)),
                  pl.BlockSpec((None, k, IN_TILE), lambda j, i: (l, 0, j)),
                  vec, vec],
        out_specs=pl.BlockSpec((tm, IN_TILE), lambda j, i: (i, j)),
        out_shape=jax.ShapeDtypeStruct((rows, n), BF16),
        scratch_shapes=[pltpu.VMEM((k, IN_TILE), BF16)],
        compiler_params=_params(2), name="inproj",
    )(h, w_in, _vec3(ln_g), _vec3(ln_b))


def _sgu_kernel(u_ref, v_ref, w_ref, b_ref, o_ref, *vo_ref, tm, chunk):
    if vo_ref:
        vo_ref[0][...] = v_ref[...].astype(F32)
    r = lax.broadcasted_iota(jnp.int32, (SGU_CHUNK, SGU_CHUNK), 0)
    c = lax.broadcasted_iota(jnp.int32, (SGU_CHUNK, SGU_CHUNK), 1)
    mask = (c <= r) & ((r & -chunk) == (c & -chunk))
    gd = IN_TILE // SGU_GROUPS
    for g in range(SGU_GROUPS):
        w = jnp.where(mask, w_ref[g], 0.0).astype(BF16)
        bias = b_ref[:, g:g + 1]
        for t in range(tm // SGU_CHUNK):
            rs = slice(t * SGU_CHUNK, (t + 1) * SGU_CHUNK)
            cs = slice(g * gd, (g + 1) * gd)
            z = jnp.dot(w, v_ref[rs, cs], preferred_element_type=F32) + bias
            o_ref[rs, cs] = (u_ref[rs, cs].astype(F32) * z).astype(BF16)


def _sgu_tables(w_s, b_s, seq_len):
    chunk = min(SGU_CHUNK, seq_len)
    reps = SGU_CHUNK // chunk
    w_t = jnp.tile(w_s[:, :, :chunk, :chunk], (1, 1, reps, reps))
    b_t = jnp.tile(jnp.swapaxes(b_s[:, :, :chunk], 1, 2), (1, reps, 1))
    return w_t, b_t


def _sgu(proj, w_t, b_t, l, depth, seq_len, tm, emit_v, prev):
    rows = proj.shape[0]
    kern = functools.partial(_sgu_kernel, tm=tm, chunk=min(SGU_CHUNK, seq_len))
    in_specs = [pl.BlockSpec((tm, IN_TILE), lambda i: (i, J_U)),
                pl.BlockSpec((tm, IN_TILE), lambda i: (i, J_V)),
                pl.BlockSpec((None, SGU_GROUPS, SGU_CHUNK, SGU_CHUNK), lambda i: (l, 0, 0, 0)),
                pl.BlockSpec((None, SGU_CHUNK, SGU_GROUPS), lambda i: (l, 0, 0))]
    out_specs = [pl.BlockSpec((tm, IN_TILE), lambda i: (i, 0))]
    out_shape = [jax.ShapeDtypeStruct((rows, IN_TILE), BF16)]
    cspecs, cargs, aliases = [], [], {}
    if emit_v:
        kern, cspecs, cargs, aliases = _stacked(kern, 4, prev, depth, l)
        out_specs.append(_stacked_spec(prev, depth, l, (tm, IN_TILE), lambda i: (i, 0)))
        out_shape.append(jax.ShapeDtypeStruct((depth, rows, IN_TILE), F32))
    out = pl.pallas_call(
        kern, grid=(rows // tm,),
        in_specs=in_specs + cspecs, out_specs=out_specs, out_shape=out_shape,
        input_output_aliases=aliases,
        compiler_params=_params(1), name="sgu",
    )(proj, proj, w_t, b_t, *cargs)
    return (out[0], out[1]) if emit_v else (out[0], None)


def _log_gamma(h):
    return math.log(1.0 - 2.0 ** (-5.0 - h))


def _ret_decays(h, c):
    lg = _log_gamma(h)
    ri = lax.broadcasted_iota(jnp.int32, (c, c), 0)
    ci = lax.broadcasted_iota(jnp.int32, (c, c), 1)
    diff = (ri - ci).astype(F32)
    decay = jnp.where(ri >= ci, jnp.exp(lg * jnp.maximum(diff, 0.0)), 0.0)
    idx = lax.broadcasted_iota(jnp.int32, (c, 1), 0).astype(F32)
    return decay, jnp.exp(lg * (idx + 1.0)), jnp.exp(lg * (c - 1.0 - idx))


def _ret_chunk(qh, kh, vh, grh, gnh, s, h, c, decays):
    decay, q_dec, k_dec = decays
    scores = lax.dot_general(qh, kh, (((1,), (1,)), ((), ())), preferred_element_type=F32) * decay
    inner = jnp.dot(scores.astype(BF16), vh, preferred_element_type=F32)
    cross = jnp.dot(qh, s.astype(BF16), preferred_element_type=F32) * q_dec
    kd = (kh.astype(F32) * k_dec).astype(BF16)
    s_new = (s * math.exp(_log_gamma(h) * c)
             + lax.dot_general(kd, vh, (((0,), (0,)), ((), ())), preferred_element_type=F32))
    o = inner + cross
    mu = jnp.mean(o, axis=-1, keepdims=True)
    var = jnp.mean(jnp.square(o - mu), axis=-1, keepdims=True)
    y = (o - mu) * lax.rsqrt(var + EPS) * gnh
    return grh * y, s_new


def _ret_prompt_kernel(q_ref, k_ref, v_ref, gr_ref, gn_ref, o_ref, so_ref, s_ref, dec_ref, *, hd):
    n = pl.program_id(1)

    @pl.when(n == 0)
    def _():
        s_ref[...] = jnp.zeros_like(s_ref)

    @pl.when((n == 0) & (pl.program_id(0) == 0))
    def _():
        for h in range(RET_HEADS):
            dec_ref[h] = _ret_decays(h, RET_CHUNK)[0]

    for h in range(RET_HEADS):
        cs = slice(h * hd, (h + 1) * hd)
        _, q_dec, k_dec = _ret_decays(h, RET_CHUNK)
        y, s_new = _ret_chunk(q_ref[:, cs], k_ref[:, cs], v_ref[:, cs], gr_ref[:, cs].astype(F32),
                              gn_ref[:, cs], s_ref[h], h, RET_CHUNK, (dec_ref[h], q_dec, k_dec))
        o_ref[:, cs] = y.astype(BF16)
        s_ref[h] = s_new

    @pl.when(n == pl.num_programs(1) - 1)
    def _():
        so_ref[...] = s_ref[...]


def _ret_prompt(proj, gn_g, l, depth, nseq, seq_len, prev):
    hd = IN_TILE // RET_HEADS
    nc = seq_len // RET_CHUNK

    def col(jc):
        return pl.BlockSpec((RET_CHUNK, IN_TILE), lambda b, n: (b * nc + n, jc))
    kern, cspecs, cargs, aliases = _stacked(functools.partial(_ret_prompt_kernel, hd=hd), 5, prev, depth, l)
    return pl.pallas_call(
        kern, grid=(nseq, nc),
        in_specs=[col(J_Q), col(J_K), col(J_VR), col(J_GR),
                  pl.BlockSpec((None, 1, IN_TILE), lambda b, n: (l, 0, 0))] + cspecs,
        out_specs=[pl.BlockSpec((RET_CHUNK, IN_TILE), lambda b, n: (b * nc + n, 0)),
                   _stacked_spec(prev, depth, l, (None, RET_HEADS, hd, hd), lambda b, n: (b, 0, 0, 0))],
        out_shape=[jax.ShapeDtypeStruct((nseq * seq_len, IN_TILE), BF16),
                   jax.ShapeDtypeStruct((depth, nseq, RET_HEADS, hd, hd), F32)],
        scratch_shapes=[pltpu.VMEM((RET_HEADS, hd, hd), F32),
                        pltpu.VMEM((RET_HEADS, RET_CHUNK, RET_CHUNK), F32)],
        input_output_aliases=aliases,
        compiler_params=_params(2), name="ret_prompt",
    )(proj, proj, proj, proj, _vec3(gn_g), *cargs)


def _ret_sample_kernel(q_ref, k_ref, v_ref, gr_ref, gn_ref, s_ref, o_ref, so_ref, y_ref, *, hd, bs, seq_len):
    q = q_ref[...].astype(F32)
    k = k_ref[...].astype(F32)
    v = v_ref[...].astype(F32)
    gr = gr_ref[...].astype(F32)
    for h in range(RET_HEADS):
        cs = slice(h * hd, (h + 1) * hd)
        decays = _ret_decays(h, seq_len)
        for b in range(bs):
            rs = slice(b * seq_len, (b + 1) * seq_len)
            y, s_new = _ret_chunk(q[rs, cs].astype(BF16), k[rs, cs].astype(BF16), v[rs, cs].astype(BF16),
                                  gr[rs, cs], gn_ref[:, cs], s_ref[b, h], h, seq_len, decays)
            y_ref[rs, cs] = y
            so_ref[b, h] = s_new
    o_ref[...] = y_ref[...].astype(BF16)


def _ret_sample(proj, gn_g, state, l, seq_len, bs, prev):
    depth, nseq = state.shape[:2]
    hd = IN_TILE // RET_HEADS

    def col(jc):
        return pl.BlockSpec((bs * seq_len, IN_TILE), lambda i: (i, jc))
    kern, cspecs, cargs, aliases = _stacked(
        functools.partial(_ret_sample_kernel, hd=hd, bs=bs, seq_len=seq_len), 6, prev, depth, l)
    return pl.pallas_call(
        kern, grid=(nseq // bs,),
        in_specs=[col(J_Q), col(J_K), col(J_VR), col(J_GR),
                  pl.BlockSpec((None, 1, IN_TILE), lambda i: (l, 0, 0)),
                  pl.BlockSpec((None, bs, RET_HEADS, hd, hd), lambda i: (l, i, 0, 0, 0))] + cspecs,
        out_specs=[pl.BlockSpec((bs * seq_len, IN_TILE), lambda i: (i, 0)),
                   _stacked_spec(prev, depth, l, (bs, RET_HEADS, hd, hd), lambda i: (i, 0, 0, 0))],
        out_shape=[jax.ShapeDtypeStruct((nseq * seq_len, IN_TILE), BF16),
                   jax.ShapeDtypeStruct(state.shape, F32)],
        scratch_shapes=[pltpu.VMEM((bs * seq_len, IN_TILE), F32)],
        input_output_aliases=aliases,
        compiler_params=_params(1), name="ret_sample",
    )(proj, proj, proj, proj, _vec3(gn_g), state, *cargs)


def _merge_kernel(ya_ref, yb_ref, ga_ref, gb_ref, wa_ref, wb_ref, o_ref, wab_ref, wbb_ref):
    @pl.when(pl.program_id(1) == 0)
    def _():
        wab_ref[...] = wa_ref[...].astype(BF16)
        wbb_ref[...] = wb_ref[...].astype(BF16)
    for c in range(o_ref.shape[0] // ROW_CHUNK):
        rows = slice(c * ROW_CHUNK, (c + 1) * ROW_CHUNK)
        a = jnp.dot(ya_ref[rows, :], wab_ref[...], preferred_element_type=F32)
        b = jnp.dot(yb_ref[rows, :], wbb_ref[...], preferred_element_type=F32)
        o_ref[rows, :] = (ga_ref[rows, :].astype(F32) * a + gb_ref[rows, :].astype(F32) * b).astype(BF16)


def _merge(ya, yb, proj, w_a, w_b, l, tm):
    rows, k = ya.shape
    n = w_a.shape[2]
    tn = IN_TILE
    row = pl.BlockSpec((tm, k), lambda j, i: (i, 0))
    wsp = pl.BlockSpec((None, k, tn), lambda j, i: (l, 0, j))
    return pl.pallas_call(
        _merge_kernel, grid=(n // tn, rows // tm),
        in_specs=[row, row,
                  pl.BlockSpec((tm, tn), lambda j, i: (i, J_GA + j)),
                  pl.BlockSpec((tm, tn), lambda j, i: (i, J_GB + j)),
                  wsp, wsp],
        out_specs=pl.BlockSpec((tm, tn), lambda j, i: (i, j)),
        out_shape=jax.ShapeDtypeStruct((rows, n), BF16),
        scratch_shapes=[pltpu.VMEM((k, tn), BF16), pltpu.VMEM((k, tn), BF16)],
        compiler_params=_params(2), name="merge",
    )(ya, yb, proj, proj, w_a, w_b)


def _ffn_up_kernel(h_ref, wg_ref, wu_ref, cw_ref, cb_ref, *rest, tm, rc, seq_len, has_state):
    if has_state:
        st_ref, y_ref, so_ref, wgb_ref, wub_ref, gbuf_ref = rest
    else:
        y_ref, so_ref, wgb_ref, wub_ref, gbuf_ref = rest
    i = pl.program_id(1)
    pad = SUBLANES
    keep = CONV_W - 1

    @pl.when(i == 0)
    def _():
        wgb_ref[...] = wg_ref[...].astype(BF16)
        wub_ref[...] = wu_ref[...].astype(BF16)

    tn = y_ref.shape[1]
    if has_state:
        gbuf_ref[0:pad, :] = jnp.zeros((pad, tn), F32)
    else:
        @pl.when(((i * tm) & (seq_len - 1)) == 0)
        def _():
            gbuf_ref[0:pad, :] = jnp.zeros((pad, tn), F32)
    for c in range(tm // rc):
        r0 = c * rc
        h = h_ref[r0:r0 + rc, :]
        g = jnp.dot(h, wgb_ref[...], preferred_element_type=F32)
        up = jnp.dot(h, wub_ref[...], preferred_element_type=F32)
        gbuf_ref[pad + r0:pad + r0 + rc, :] = g
        g1 = gbuf_ref[pad - 1 + r0:pad - 1 + r0 + rc, :]
        g2 = gbuf_ref[pad - 2 + r0:pad - 2 + r0 + rc, :]
        if has_state:
            bs = rc // seq_len
            st = st_ref[c * bs:(c + 1) * bs]
            s0 = jnp.broadcast_to(st[:, 0:1, :], (bs, seq_len, tn)).reshape(rc, tn)
            s1 = jnp.broadcast_to(st[:, 1:2, :], (bs, seq_len, tn)).reshape(rc, tn)
            t = lax.broadcasted_iota(jnp.int32, (rc, tn), 0) & (seq_len - 1)
            g1 = jnp.where(t == 0, s1, g1)
            g2 = jnp.where(t == 0, s0, jnp.where(t == 1, s1, g2))
            so_ref[c * bs:(c + 1) * bs] = g.reshape(bs, seq_len, tn)[:, seq_len - keep:, :]
        conv = g2 * cw_ref[0:1, :] + g1 * cw_ref[1:2, :] + g * cw_ref[2:3, :] + cb_ref[...]
        y_ref[r0:r0 + rc, :] = (jax.nn.gelu(conv) * up).astype(BF16)
    if not has_state:
        last = gbuf_ref[pad + tm - keep:pad + tm, :]
        gbuf_ref[pad - keep:pad, :] = last
        so_ref[...] = last


def _ffn_up(h, w_gate, w_up, conv_w, conv_b, l, depth, nseq, seq_len, tm, tn, state, prev):
    rows, k = h.shape
    n = w_gate.shape[2]
    keep = CONV_W - 1
    has_state = state is not None
    assert seq_len & (seq_len - 1) == 0 and seq_len >= CONV_W
    kern = functools.partial(_ffn_up_kernel, tm=tm, rc=ROW_CHUNK, seq_len=seq_len, has_state=has_state)
    wsp = pl.BlockSpec((None, k, tn), lambda j, i: (l, 0, j))
    in_specs = [pl.BlockSpec((tm, k), lambda j, i: (i, 0)), wsp, wsp,
                pl.BlockSpec((None, CONV_W, tn), lambda j, i: (l, 0, j)),
                pl.BlockSpec((None, 1, tn), lambda j, i: (l, 0, j))]
    args = [h, w_gate, w_up, conv_w, _vec3(conv_b)]
    if has_state:
        bs = tm // seq_len
        in_specs.append(pl.BlockSpec((None, bs, keep, tn), lambda j, i: (l, i, 0, j)))
        args.append(state)
        so_spec = _stacked_spec(prev, depth, l, (bs, keep, tn), lambda j, i: (i, 0, j))
    else:
        tps = seq_len // tm
        so_spec = _stacked_spec(prev, depth, l, (None, keep, tn), lambda j, i: (i // tps, 0, j))
    kern, cspecs, cargs, aliases = _stacked(kern, len(args), prev, depth, l)
    return pl.pallas_call(
        kern, grid=(n // tn, rows // tm),
        in_specs=in_specs + cspecs,
        out_specs=[pl.BlockSpec((tm, tn), lambda j, i: (i, j)), so_spec],
        out_shape=[jax.ShapeDtypeStruct((rows, n), BF16),
                   jax.ShapeDtypeStruct((depth, nseq, keep, n), F32)],
        scratch_shapes=[pltpu.VMEM((k, tn), BF16), pltpu.VMEM((k, tn), BF16),
                        pltpu.VMEM((tm + SUBLANES, tn), F32)],
        input_output_aliases=aliases,
        compiler_params=_params(2), name="ffn_up",
    )(*args, *cargs)


def _layer(l, depth, x, h, mod, w, cfg, s_ret, conv_prev, outs):
    nseq, seq_len, tm, tm_post = cfg["nseq"], cfg["seq_len"], cfg["tm"], cfg["tm_post"]
    ret_out, conv_out, v_out = outs

    proj = _inproj(h, w["w_in"], w["sgu_ln_g"], w["sgu_ln_b"], l, tm, seq_len, cfg["pos0"])
    ya, v_out = _sgu(proj, cfg["sgu_w"], cfg["sgu_b"], l, depth, seq_len, tm, cfg["emit_v"], v_out)
    if s_ret is None:
        yb, ret_out = _ret_prompt(proj, w["ret_gn_g"], l, depth, nseq, seq_len, ret_out)
    else:
        yb, ret_out = _ret_sample(proj, w["ret_gn_g"], s_ret, l, seq_len, cfg["ret_bs"], ret_out)
    merged = _merge(ya, yb, proj, w["w_branch_a"], w["w_branch_b"], l, tm)
    x, h2 = _matmul_post(merged, w["w_out"], x, w["norm_post1"], mod, l, K_G1, seq_len, cfg["tm_out"],
                         (w["norm_pre2"], l, K_SC2, K_SH2))

    y, conv_out = _ffn_up(h2, w["ffn_w_gate"], w["ffn_w_up"], w["ffn_conv_w"], w["ffn_conv_b"], l, depth,
                          nseq, seq_len, tm, 512, conv_prev, conv_out)
    nxt = None if l == depth - 1 else (w["norm_pre1"], l + 1, K_SC1, K_SH1)
    x, h_next = _matmul_post(y, w["ffn_w_down_bf16"], x, w["norm_post2"], mod, l, K_G2, seq_len, tm_post, nxt)
    return x, h_next, (ret_out, conv_out, v_out)


def kernel(x_prompt, x_sample, state_ret, state_conv, c_prompt, c_sample, w_ada, b_ada, norm_pre1, norm_post1, norm_pre2, norm_post2, w_in, sgu_w_s, sgu_b_s, sgu_ln_g, sgu_ln_b, ret_gn_g, w_branch_a, w_branch_b, w_out, ffn_w_gate, ffn_w_up, ffn_conv_w, ffn_conv_b, ffn_w_down):
    w = dict(norm_pre1=norm_pre1, norm_post1=norm_post1, norm_pre2=norm_pre2, norm_post2=norm_post2, w_in=w_in,
             sgu_ln_g=sgu_ln_g, sgu_ln_b=sgu_ln_b, ret_gn_g=ret_gn_g, w_branch_a=w_branch_a,
             w_branch_b=w_branch_b, w_out=w_out, ffn_w_gate=ffn_w_gate, ffn_w_up=ffn_w_up,
             ffn_conv_w=ffn_conv_w, ffn_conv_b=ffn_conv_b, ffn_w_down_bf16=_cast_bf16(ffn_w_down))
    depth = w_ada.shape[0]
    nb, seq, d = x_prompt.shape
    ndb, dseq, _ = x_sample.shape

    mod_p, mod_s = _modulation(c_prompt, c_sample, w_ada, b_ada)

    sgu_wp, sgu_bp = _sgu_tables(sgu_w_s, sgu_b_s, seq)
    sgu_ws, sgu_bs = _sgu_tables(sgu_w_s, sgu_b_s, dseq)
    cfg_p = dict(nseq=nb, seq_len=seq, tm=1024, tm_out=512, tm_post=256, pos0=0, sgu_w=sgu_wp, sgu_b=sgu_bp,
                 emit_v=False)
    cfg_s = dict(nseq=ndb, seq_len=dseq, tm=512, tm_out=512, tm_post=256, pos0=PAST_LEN, ret_bs=8, sgu_w=sgu_ws,
                 sgu_b=sgu_bs, emit_v=True)

    xp = x_prompt.reshape(nb * seq, d)
    xs = x_sample.reshape(ndb * dseq, d)
    hp = _prenorm(xp, norm_pre1, mod_p, 0, K_SC1, K_SH1, seq, 512)
    hs = _prenorm(xs, norm_pre1, mod_s, 0, K_SC1, K_SH1, dseq, 512)
    outs_p = outs_s = (None, None, None)
    for l in range(depth):
        xp, hp, outs_p = _layer(l, depth, xp, hp, mod_p, w, cfg_p, None, None, outs_p)
        xs, hs, outs_s = _layer(l, depth, xs, hs, mod_s, w, cfg_s, state_ret, state_conv, outs_s)
    ret_p, conv_p, _ = outs_p
    ret_s, conv_s, v_s = outs_s
    return (xp.reshape(nb, seq, d), xs.reshape(ndb, dseq, d), ret_p, ret_s, conv_p, conv_s,
            v_s.reshape(depth, ndb, dseq, -1))
```

```python
import functools
import math

import jax
import jax.numpy as jnp
from jax import lax
from jax.experimental import pallas as pl
from jax.experimental.pallas import tpu as pltpu

F32 = jnp.float32
BF16 = jnp.bfloat16

EPS = 1e-6
ROPE_BASE = 10000.0
PAST_LEN = 16384
SGU_GROUPS = 8
SGU_CHUNK = 128
RET_HEADS = 8
RET_CHUNK = 128
CONV_W = 3

SUBLANES = 8
BF16_ROWS = 16
ROW_CHUNK = 128
VMEM_LIMIT_BYTES = 56 * 1024 * 1024


def _params(n_axes):
    return pltpu.CompilerParams(dimension_semantics=("arbitrary",) * n_axes,
                                vmem_limit_bytes=VMEM_LIMIT_BYTES)


def _stacked(kernel, n_in, prev, depth, l):
    if prev is None:
        def first(*refs):
            refs = list(refs)
            so = refs[n_in + 1]
            for k in range(depth):
                if k != l:
                    so[k] = jnp.zeros(so.shape[1:], so.dtype)
            refs[n_in + 1] = so.at[l]
            kernel(*refs)
        return first, [], [], {}

    def later(*refs):
        kernel(*refs[:n_in], *refs[n_in + 1:])
    return later, [pl.BlockSpec(memory_space=pl.ANY)], [prev], {n_in: 1}


def _stacked_spec(prev, depth, l, block, index_fn):
    if prev is None:
        return pl.BlockSpec((depth,) + block, lambda *g: (0,) + index_fn(*g))
    return pl.BlockSpec((None,) + block, lambda *g: (l,) + index_fn(*g))


def _rms(x, g):
    return x * lax.rsqrt(jnp.mean(x * x, axis=-1, keepdims=True) + EPS) * g


def _silu(x):
    return x * jax.nn.sigmoid(x)


def _vec3(p):
    return p.reshape(p.shape[0], 1, p.shape[1])


def _mod_kernel(c_ref, w_ref, b_ref, op_ref, os_ref, *, n_p, s0):
    a = _silu(c_ref[...]).astype(BF16)
    m = jnp.dot(a, w_ref[...].astype(BF16), preferred_element_type=F32) + b_ref[...]
    op_ref[...] = m[0:n_p][:, None, :]
    os_ref[...] = m[s0:][:, None, :]


def _modulation(c_prompt, c_sample, w_ada, b_ada, tn=1024):
    depth, d, n = w_ada.shape
    n_p, n_s = c_prompt.shape[0], c_sample.shape[0]
    s0 = -(-n_p // BF16_ROWS) * BF16_ROWS
    c_all = jnp.concatenate([c_prompt, jnp.zeros((s0 - n_p, d), F32), c_sample], axis=0)
    rows = c_all.shape[0]
    assert rows % BF16_ROWS == 0
    return pl.pallas_call(
        functools.partial(_mod_kernel, n_p=n_p, s0=s0),
        grid=(depth, n // tn),
        in_specs=[pl.BlockSpec((rows, d), lambda l, j: (0, 0)),
                  pl.BlockSpec((None, d, tn), lambda l, j: (l, 0, j)),
                  pl.BlockSpec((None, 1, tn), lambda l, j: (l, 0, j))],
        out_specs=[pl.BlockSpec((None, n_p, 1, tn), lambda l, j: (l, 0, 0, j)),
                   pl.BlockSpec((None, n_s, 1, tn), lambda l, j: (l, 0, 0, j))],
        out_shape=[jax.ShapeDtypeStruct((depth, n_p, 1, n), F32),
                   jax.ShapeDtypeStruct((depth, n_s, 1, n), F32)],
        compiler_params=_params(2),
        name="adaln_mod",
    )(c_all, w_ada, b_ada.reshape(depth, 1, n))


K_SH1, K_SC1, K_G1, K_SH2, K_SC2, K_G2 = range(6)


def _mod_spec(l, k, tm, seq_len, d, grid_rank):
    if seq_len >= tm:
        tps = seq_len // tm
        return pl.BlockSpec((None, 1, 1, d), lambda *g: (l, g[grid_rank - 1] // tps, 0, k))
    return pl.BlockSpec((None, tm // seq_len, 1, d), lambda *g: (l, g[grid_rank - 1], 0, k))


def _rows_of(m_ref, r0, nrows, seq_len):
    d = m_ref.shape[-1]
    if m_ref.shape[0] == 1:
        return m_ref[0]
    nb = nrows // seq_len
    m = m_ref[r0 // seq_len:r0 // seq_len + nb]
    return jnp.broadcast_to(m, (nb, seq_len, d)).reshape(nrows, d)


def _prenorm_kernel(x_ref, g_ref, sc_ref, sh_ref, h_ref, *, seq_len):
    tm = x_ref.shape[0]
    sc = _rows_of(sc_ref, 0, tm, seq_len)
    sh = _rows_of(sh_ref, 0, tm, seq_len)
    h_ref[...] = (_rms(x_ref[...], g_ref[...]) * (1.0 + sc) + sh).astype(BF16)


def _prenorm(x, g_pre, mod, l, k_sc, k_sh, seq_len, tm):
    rows, d = x.shape
    act = pl.BlockSpec((tm, d), lambda i: (i, 0))
    return pl.pallas_call(
        functools.partial(_prenorm_kernel, seq_len=seq_len), grid=(rows // tm,),
        in_specs=[act, pl.BlockSpec((None, 1, d), lambda i: (l, 0, 0)),
                  _mod_spec(l, k_sc, tm, seq_len, d, 1), _mod_spec(l, k_sh, tm, seq_len, d, 1)],
        out_specs=act,
        out_shape=jax.ShapeDtypeStruct((rows, d), BF16),
        compiler_params=_params(1), name="prenorm",
    )(x, _vec3(g_pre), mod, mod)


def _cast_kernel(w_ref, o_ref):
    o_ref[...] = w_ref[...].astype(BF16)


def _cast_bf16(w, tk=512):
    depth, k, n = w.shape
    spec = pl.BlockSpec((None, tk, n), lambda l, i: (l, i, 0))
    return pl.pallas_call(
        _cast_kernel, grid=(depth, k // tk), in_specs=[spec], out_specs=spec,
        out_shape=jax.ShapeDtypeStruct(w.shape, BF16),
        compiler_params=_params(2), name="cast_bf16",
    )(w)


def _mm_post_kernel(m_ref, w_ref, x_ref, gpost_ref, gate_ref, *rest, tm, rc, seq_len, has_next, cast):
    rest = list(rest)
    if has_next:
        gpre_ref, sc_ref, sh_ref = rest[:3]
        rest = rest[3:]
    xo_ref = rest.pop(0)
    h_ref = rest.pop(0) if has_next else None
    if cast:
        wb_ref = rest.pop(0)

        @pl.when(pl.program_id(0) == 0)
        def _():
            wb_ref[...] = w_ref[...].astype(BF16)
    else:
        wb_ref = w_ref

    for c in range(tm // rc):
        r0 = c * rc
        rows = slice(r0, r0 + rc)
        t = jnp.dot(m_ref[rows, :], wb_ref[...], preferred_element_type=F32)
        x = x_ref[rows, :] + _rows_of(gate_ref, r0, rc, seq_len) * _rms(t, gpost_ref[...])
        xo_ref[rows, :] = x
        if has_next:
            h = (_rms(x, gpre_ref[...]) * (1.0 + _rows_of(sc_ref, r0, rc, seq_len))
                 + _rows_of(sh_ref, r0, rc, seq_len))
            h_ref[rows, :] = h.astype(BF16)


def _matmul_post(m, w, x, g_post, mod, l, k_gate, seq_len, tm, nxt):
    rows, k = m.shape
    d = w.shape[2]
    cast = w.dtype != BF16
    has_next = nxt is not None
    act = pl.BlockSpec((tm, d), lambda i: (i, 0))

    def vec():
        return pl.BlockSpec((None, 1, d), lambda i: (l, 0, 0))
    in_specs = [pl.BlockSpec((tm, k), lambda i: (i, 0)),
                pl.BlockSpec((None, k, d), lambda i: (l, 0, 0), pipeline_mode=pl.Buffered(1)),
                act, vec(), _mod_spec(l, k_gate, tm, seq_len, d, 1)]
    args = [m, w, x, _vec3(g_post), mod]
    out_specs, out_shape = [act], [jax.ShapeDtypeStruct((rows, d), F32)]
    if has_next:
        g_pre, ln, k_sc, k_sh = nxt
        in_specs += [pl.BlockSpec((None, 1, d), lambda i: (ln, 0, 0)),
                     _mod_spec(ln, k_sc, tm, seq_len, d, 1), _mod_spec(ln, k_sh, tm, seq_len, d, 1)]
        args += [_vec3(g_pre), mod, mod]
        out_specs.append(act)
        out_shape.append(jax.ShapeDtypeStruct((rows, d), BF16))
    kern = functools.partial(_mm_post_kernel, tm=tm, rc=ROW_CHUNK, seq_len=seq_len, has_next=has_next, cast=cast)
    out = pl.pallas_call(
        kern, grid=(rows // tm,), in_specs=in_specs, out_specs=out_specs, out_shape=out_shape,
        scratch_shapes=[pltpu.VMEM((k, d), BF16)] if cast else [],
        compiler_params=_params(1), name="matmul_post",
    )(*args)
    return (out[0], out[1]) if has_next else (out[0], None)


IN_TILE = 1024
J_U, J_V, J_Q, J_K, J_VR, J_GR, J_GA, J_GB = 0, 1, 2, 3, 4, 5, 6, 8


def _inproj_kernel(h_ref, w_ref, lng_ref, lnb_ref, o_ref, wb_ref, *, tm, rc, seq_len, pos0, head_dim):
    j = pl.program_id(0)
    i = pl.program_id(1)

    @pl.when(i == 0)
    def _():
        wb_ref[...] = w_ref[...].astype(BF16)

    def run(epilogue):
        for c in range(tm // rc):
            rows = slice(c * rc, (c + 1) * rc)
            acc = jnp.dot(h_ref[rows, :], wb_ref[...], preferred_element_type=F32)
            o_ref[rows, :] = epilogue(acc, c).astype(BF16)

    def layer_norm_gelu(acc, c):
        v = jax.nn.gelu(acc)
        mu = jnp.mean(v, axis=-1, keepdims=True)
        var = jnp.mean(jnp.square(v - mu), axis=-1, keepdims=True)
        return (v - mu) * lax.rsqrt(var + EPS) * lng_ref[...] + lnb_ref[...]

    def rope(acc, c):
        half = head_dim // 2
        lane = lax.broadcasted_iota(jnp.int32, (rc, head_dim), 1)
        row = lax.broadcasted_iota(jnp.int32, (rc, head_dim), 0)
        pos = pos0 + ((i * tm + c * rc + row) & (seq_len - 1))
        inv = jnp.exp((lane & (half - 1)).astype(F32) * (-math.log(ROPE_BASE) / half))
        ang = pos.astype(F32) * inv
        scale = jnp.where(j == J_K, head_dim ** -0.5, 1.0).astype(F32)
        cos = jnp.cos(ang) * scale
        sin = jnp.where(lane < half, -jnp.sin(ang), jnp.sin(ang)) * scale
        heads = [acc[:, h * head_dim:(h + 1) * head_dim] for h in range(IN_TILE // head_dim)]
        return jnp.concatenate([xh * cos + pltpu.roll(xh, half, axis=1) * sin for xh in heads], axis=1)

    pl.when(j == J_U)(lambda: run(lambda acc, c: jax.nn.gelu(acc)))
    pl.when(j == J_V)(lambda: run(layer_norm_gelu))
    pl.when((j == J_Q) | (j == J_K))(lambda: run(rope))
    pl.when(j == J_VR)(lambda: run(lambda acc, c: acc))
    pl.when(j == J_GR)(lambda: run(lambda acc, c: _silu(acc)))
    pl.when(j >= J_GA)(lambda: run(lambda acc, c: jax.nn.sigmoid(acc)))


def _inproj(h, w_in, ln_g, ln_b, l, tm, rc, seq_len, pos0):
    rows, k = h.shape
    n = w_in.shape[2]
    assert seq_len & (seq_len - 1) == 0 and n == 10 * IN_TILE
    kern = functools.partial(_inproj_kernel, tm=tm, rc=rc, seq_len=seq_len, pos0=pos0,
                             head_dim=IN_TILE // RET_HEADS)
    vec = pl.BlockSpec((None, 1, IN_TILE), lambda j, i: (l, 0, 0))
    return pl.pallas_call(
        kern, grid=(n // IN_TILE, rows // tm),
        in_specs=[pl.BlockSpec((tm, k), lambda j, i: (i, 0)),
                  pl.BlockSpec((None, k, IN_TILE), lambda j, i: (l, 0, j)),
                  vec, vec],
        out_specs=pl.BlockSpec((tm, IN_TILE), lambda j, i: (i, j)),
        out_shape=jax.ShapeDtypeStruct((rows, n), BF16),
        scratch_shapes=[pltpu.VMEM((k, IN_TILE), BF16)],
        compiler_params=_params(2), name="inproj",
    )(h, w_in, _vec3(ln_g), _vec3(ln_b))


def _sgu_kernel(u_ref, v_ref, w_ref, b_ref, o_ref, *vo_ref, tm, chunk):
    if vo_ref:
        vo_ref[0][...] = v_ref[...].astype(F32)
    r = lax.broadcasted_iota(jnp.int32, (SGU_CHUNK, SGU_CHUNK), 0)
    c = lax.broadcasted_iota(jnp.int32, (SGU_CHUNK, SGU_CHUNK), 1)
    mask = (c <= r) & ((r & -chunk) == (c & -chunk))
    gd = IN_TILE // SGU_GROUPS
    for g in range(SGU_GROUPS):
        w = jnp.where(mask, w_ref[g], 0.0).astype(BF16)
        bias = b_ref[:, g:g + 1]
        for t in range(tm // SGU_CHUNK):
            rs = slice(t * SGU_CHUNK, (t + 1) * SGU_CHUNK)
            cs = slice(g * gd, (g + 1) * gd)
            z = jnp.dot(w, v_ref[rs, cs], preferred_element_type=F32) + bias
            o_ref[rs, cs] = (u_ref[rs, cs].astype(F32) * z).astype(BF16)


def _sgu_tables(w_s, b_s, seq_len):
    chunk = min(SGU_CHUNK, seq_len)
    reps = SGU_CHUNK // chunk
    w_t = jnp.tile(w_s[:, :, :chunk, :chunk], (1, 1, reps, reps))
    b_t = jnp.tile(jnp.swapaxes(b_s[:, :, :chunk], 1, 2), (1, reps, 1))
    return w_t, b_t


def _sgu(proj, w_t, b_t, l, depth, seq_len, tm, emit_v, prev):
    rows = proj.shape[0]
    kern = functools.partial(_sgu_kernel, tm=tm, chunk=min(SGU_CHUNK, seq_len))
    in_specs = [pl.BlockSpec((tm, IN_TILE), lambda i: (i, J_U)),
                pl.BlockSpec((tm, IN_TILE), lambda i: (i, J_V)),
                pl.BlockSpec((None, SGU_GROUPS, SGU_CHUNK, SGU_CHUNK), lambda i: (l, 0, 0, 0)),
                pl.BlockSpec((None, SGU_CHUNK, SGU_GROUPS), lambda i: (l, 0, 0))]
    out_specs = [pl.BlockSpec((tm, IN_TILE), lambda i: (i, 0))]
    out_shape = [jax.ShapeDtypeStruct((rows, IN_TILE), BF16)]
    cspecs, cargs, aliases = [], [], {}
    if emit_v:
        kern, cspecs, cargs, aliases = _stacked(kern, 4, prev, depth, l)
        out_specs.append(_stacked_spec(prev, depth, l, (tm, IN_TILE), lambda i: (i, 0)))
        out_shape.append(jax.ShapeDtypeStruct((depth, rows, IN_TILE), F32))
    out = pl.pallas_call(
        kern, grid=(rows // tm,),
        in_specs=in_specs + cspecs, out_specs=out_specs, out_shape=out_shape,
        input_output_aliases=aliases,
        compiler_params=_params(1), name="sgu",
    )(proj, proj, w_t, b_t, *cargs)
    return (out[0], out[1]) if emit_v else (out[0], None)


def _log_gamma(h):
    return math.log(1.0 - 2.0 ** (-5.0 - h))


def _ret_decays(h, c):
    lg = _log_gamma(h)
    ri = lax.broadcasted_iota(jnp.int32, (c, c), 0)
    ci = lax.broadcasted_iota(jnp.int32, (c, c), 1)
    diff = (ri - ci).astype(F32)
    decay = jnp.where(ri >= ci, jnp.exp(lg * jnp.maximum(diff, 0.0)), 0.0)
    idx = lax.broadcasted_iota(jnp.int32, (c, 1), 0).astype(F32)
    return decay, jnp.exp(lg * (idx + 1.0)), jnp.exp(lg * (c - 1.0 - idx))


def _ret_chunk(qh, kh, vh, grh, gnh, s, h, c, decays):
    decay, q_dec, k_dec = decays
    scores = lax.dot_general(qh, kh, (((1,), (1,)), ((), ())), preferred_element_type=F32) * decay
    inner = jnp.dot(scores.astype(BF16), vh, preferred_element_type=F32)
    cross = jnp.dot(qh, s.astype(BF16), preferred_element_type=F32) * q_dec
    kd = (kh.astype(F32) * k_dec).astype(BF16)
    s_new = (s * math.exp(_log_gamma(h) * c)
             + lax.dot_general(kd, vh, (((0,), (0,)), ((), ())), preferred_element_type=F32))
    o = inner + cross
    mu = jnp.mean(o, axis=-1, keepdims=True)
    var = jnp.mean(jnp.square(o - mu), axis=-1, keepdims=True)
    y = (o - mu) * lax.rsqrt(var + EPS) * gnh
    return grh * y, s_new


def _ret_prompt_kernel(q_ref, k_ref, v_ref, gr_ref, gn_ref, o_ref, so_ref, s_ref, dec_ref, *, hd):
    n = pl.program_id(1)

    @pl.when(n == 0)
    def _():
        s_ref[...] = jnp.zeros_like(s_ref)

    @pl.when((n == 0) & (pl.program_id(0) == 0))
    def _():
        for h in range(RET_HEADS):
            dec_ref[h] = _ret_decays(h, RET_CHUNK)[0]

    for h in range(RET_HEADS):
        cs = slice(h * hd, (h + 1) * hd)
        _, q_dec, k_dec = _ret_decays(h, RET_CHUNK)
        y, s_new = _ret_chunk(q_ref[:, cs], k_ref[:, cs], v_ref[:, cs], gr_ref[:, cs].astype(F32),
                              gn_ref[:, cs], s_ref[h], h, RET_CHUNK, (dec_ref[h], q_dec, k_dec))
        o_ref[:, cs] = y.astype(BF16)
        s_ref[h] = s_new

    @pl.when(n == pl.num_programs(1) - 1)
    def _():
        so_ref[...] = s_ref[...]


def _ret_prompt(proj, gn_g, l, depth, nseq, seq_len, prev):
    hd = IN_TILE // RET_HEADS
    nc = seq_len // RET_CHUNK

    def col(jc):
        return pl.BlockSpec((RET_CHUNK, IN_TILE), lambda b, n: (b * nc + n, jc))
    kern, cspecs, cargs, aliases = _stacked(functools.partial(_ret_prompt_kernel, hd=hd), 5, prev, depth, l)
    return pl.pallas_call(
        kern, grid=(nseq, nc),
        in_specs=[col(J_Q), col(J_K), col(J_VR), col(J_GR),
                  pl.BlockSpec((None, 1, IN_TILE), lambda b, n: (l, 0, 0))] + cspecs,
        out_specs=[pl.BlockSpec((RET_CHUNK, IN_TILE), lambda b, n: (b * nc + n, 0)),
                   _stacked_spec(prev, depth, l, (None, RET_HEADS, hd, hd), lambda b, n: (b, 0, 0, 0))],
        out_shape=[jax.ShapeDtypeStruct((nseq * seq_len, IN_TILE), BF16),
                   jax.ShapeDtypeStruct((depth, nseq, RET_HEADS, hd, hd), F32)],
        scratch_shapes=[pltpu.VMEM((RET_HEADS, hd, hd), F32),
                        pltpu.VMEM((RET_HEADS, RET_CHUNK, RET_CHUNK), F32)],
        input_output_aliases=aliases,
        compiler_params=_params(2), name="ret_prompt",
    )(proj, proj, proj, proj, _vec3(gn_g), *cargs)


def _ret_sample_kernel(q_ref, k_ref, v_ref, gr_ref, gn_ref, s_ref, o_ref, so_ref, y_ref, *, hd, bs, seq_len):
    q = q_ref[...].astype(F32)
    k = k_ref[...].astype(F32)
    v = v_ref[...].astype(F32)
    gr = gr_ref[...].astype(F32)
    for h in range(RET_HEADS):
        cs = slice(h * hd, (h + 1) * hd)
        decays = _ret_decays(h, seq_len)
        for b in range(bs):
            rs = slice(b * seq_len, (b + 1) * seq_len)
            y, s_new = _ret_chunk(q[rs, cs].astype(BF16), k[rs, cs].astype(BF16), v[rs, cs].astype(BF16),
                                  gr[rs, cs], gn_ref[:, cs], s_ref[b, h], h, seq_len, decays)
            y_ref[rs, cs] = y
            so_ref[b, h] = s_new
    o_ref[...] = y_ref[...].astype(BF16)


def _ret_sample(proj, gn_g, state, l, seq_len, bs, prev):
    depth, nseq = state.shape[:2]
    hd = IN_TILE // RET_HEADS

    def col(jc):
        return pl.BlockSpec((bs * seq_len, IN_TILE), lambda i: (i, jc))
    kern, cspecs, cargs, aliases = _stacked(
        functools.partial(_ret_sample_kernel, hd=hd, bs=bs, seq_len=seq_len), 6, prev, depth, l)
    return pl.pallas_call(
        kern, grid=(nseq // bs,),
        in_specs=[col(J_Q), col(J_K), col(J_VR), col(J_GR),
                  pl.BlockSpec((None, 1, IN_TILE), lambda i: (l, 0, 0)),
                  pl.BlockSpec((None, bs, RET_HEADS, hd, hd), lambda i: (l, i, 0, 0, 0))] + cspecs,
        out_specs=[pl.BlockSpec((bs * seq_len, IN_TILE), lambda i: (i, 0)),
                   _stacked_spec(prev, depth, l, (bs, RET_HEADS, hd, hd), lambda i: (i, 0, 0, 0))],
        out_shape=[jax.ShapeDtypeStruct((nseq * seq_len, IN_TILE), BF16),
                   jax.ShapeDtypeStruct(state.shape, F32)],
        scratch_shapes=[pltpu.VMEM((bs * seq_len, IN_TILE), F32)],
        input_output_aliases=aliases,
        compiler_params=_params(1), name="ret_sample",
    )(proj, proj, proj, proj, _vec3(gn_g), state, *cargs)


def _merge_kernel(ya_ref, yb_ref, ga_ref, gb_ref, wa_ref, wb_ref, o_ref, wab_ref, wbb_ref, *, rc):
    @pl.when(pl.program_id(1) == 0)
    def _():
        wab_ref[...] = wa_ref[...].astype(BF16)
        wbb_ref[...] = wb_ref[...].astype(BF16)
    for c in range(o_ref.shape[0] // rc):
        rows = slice(c * rc, (c + 1) * rc)
        a = jnp.dot(ya_ref[rows, :], wab_ref[...], preferred_element_type=F32)
        b = jnp.dot(yb_ref[rows, :], wbb_ref[...], preferred_element_type=F32)
        o_ref[rows, :] = (ga_ref[rows, :].astype(F32) * a + gb_ref[rows, :].astype(F32) * b).astype(BF16)


def _merge(ya, yb, proj, w_a, w_b, l, tm, rc):
    rows, k = ya.shape
    n = w_a.shape[2]
    tn = IN_TILE
    row = pl.BlockSpec((tm, k), lambda j, i: (i, 0))
    wsp = pl.BlockSpec((None, k, tn), lambda j, i: (l, 0, j))
    return pl.pallas_call(
        functools.partial(_merge_kernel, rc=rc), grid=(n // tn, rows // tm),
        in_specs=[row, row,
                  pl.BlockSpec((tm, tn), lambda j, i: (i, J_GA + j)),
                  pl.BlockSpec((tm, tn), lambda j, i: (i, J_GB + j)),
                  wsp, wsp],
        out_specs=pl.BlockSpec((tm, tn), lambda j, i: (i, j)),
        out_shape=jax.ShapeDtypeStruct((rows, n), BF16),
        scratch_shapes=[pltpu.VMEM((k, tn), BF16), pltpu.VMEM((k, tn), BF16)],
        compiler_params=_params(2), name="merge",
    )(ya, yb, proj, proj, w_a, w_b)


def _ffn_up_kernel(h_ref, wg_ref, wu_ref, cw_ref, cb_ref, *rest, tm, rc, seq_len, has_state):
    if has_state:
        st_ref, y_ref, so_ref, wgb_ref, wub_ref, gbuf_ref = rest
    else:
        y_ref, so_ref, wgb_ref, wub_ref, gbuf_ref = rest
    i = pl.program_id(1)
    pad = SUBLANES
    keep = CONV_W - 1

    @pl.when(i == 0)
    def _():
        wgb_ref[...] = wg_ref[...].astype(BF16)
        wub_ref[...] = wu_ref[...].astype(BF16)

    tn = y_ref.shape[1]
    if has_state:
        gbuf_ref[0:pad, :] = jnp.zeros((pad, tn), F32)
    else:
        @pl.when(((i * tm) & (seq_len - 1)) == 0)
        def _():
            gbuf_ref[0:pad, :] = jnp.zeros((pad, tn), F32)
    for c in range(tm // rc):
        r0 = c * rc
        h = h_ref[r0:r0 + rc, :]
        g = jnp.dot(h, wgb_ref[...], preferred_element_type=F32)
        up = jnp.dot(h, wub_ref[...], preferred_element_type=F32)
        gbuf_ref[pad + r0:pad + r0 + rc, :] = g
        g1 = gbuf_ref[pad - 1 + r0:pad - 1 + r0 + rc, :]
        g2 = gbuf_ref[pad - 2 + r0:pad - 2 + r0 + rc, :]
        if has_state:
            bs = rc // seq_len
            st = st_ref[c * bs:(c + 1) * bs]
            s0 = jnp.broadcast_to(st[:, 0:1, :], (bs, seq_len, tn)).reshape(rc, tn)
            s1 = jnp.broadcast_to(st[:, 1:2, :], (bs, seq_len, tn)).reshape(rc, tn)
            t = lax.broadcasted_iota(jnp.int32, (rc, tn), 0) & (seq_len - 1)
            g1 = jnp.where(t == 0, s1, g1)
            g2 = jnp.where(t == 0, s0, jnp.where(t == 1, s1, g2))
            so_ref[c * bs:(c + 1) * bs] = g.reshape(bs, seq_len, tn)[:, seq_len - keep:, :]
        conv = g2 * cw_ref[0:1, :] + g1 * cw_ref[1:2, :] + g * cw_ref[2:3, :] + cb_ref[...]
        y_ref[r0:r0 + rc, :] = (jax.nn.gelu(conv) * up).astype(BF16)
    if not has_state:
        last = gbuf_ref[pad + tm - keep:pad + tm, :]
        gbuf_ref[pad - keep:pad, :] = last
        so_ref[...] = last


def _ffn_up(h, w_gate, w_up, conv_w, conv_b, l, depth, nseq, seq_len, tm, rc, tn, state, prev):
    rows, k = h.shape
    n = w_gate.shape[2]
    keep = CONV_W - 1
    has_state = state is not None
    assert seq_len & (seq_len - 1) == 0 and seq_len >= CONV_W
    kern = functools.partial(_ffn_up_kernel, tm=tm, rc=rc, seq_len=seq_len, has_state=has_state)
    wsp = pl.BlockSpec((None, k, tn), lambda j, i: (l, 0, j))
    in_specs = [pl.BlockSpec((tm, k), lambda j, i: (i, 0)), wsp, wsp,
                pl.BlockSpec((None, CONV_W, tn), lambda j, i: (l, 0, j)),
                pl.BlockSpec((None, 1, tn), lambda j, i: (l, 0, j))]
    args = [h, w_gate, w_up, conv_w, _vec3(conv_b)]
    if has_state:
        bs = tm // seq_len
        in_specs.append(pl.BlockSpec((None, bs, keep, tn), lambda j, i: (l, i, 0, j)))
        args.append(state)
        so_spec = _stacked_spec(prev, depth, l, (bs, keep, tn), lambda j, i: (i, 0, j))
    else:
        tps = seq_len // tm
        so_spec = _stacked_spec(prev, depth, l, (None, keep, tn), lambda j, i: (i // tps, 0, j))
    kern, cspecs, cargs, aliases = _stacked(kern, len(args), prev, depth, l)
    return pl.pallas_call(
        kern, grid=(n // tn, rows // tm),
        in_specs=in_specs + cspecs,
        out_specs=[pl.BlockSpec((tm, tn), lambda j, i: (i, j)), so_spec],
        out_shape=[jax.ShapeDtypeStruct((rows, n), BF16),
                   jax.ShapeDtypeStruct((depth, nseq, keep, n), F32)],
        scratch_shapes=[pltpu.VMEM((k, tn), BF16), pltpu.VMEM((k, tn), BF16),
                        pltpu.VMEM((tm + SUBLANES, tn), F32)],
        input_output_aliases=aliases,
        compiler_params=_params(2), name="ffn_up",
    )(*args, *cargs)


def _layer(l, depth, x, h, mod, w, cfg, s_ret, conv_prev, outs):
    nseq, seq_len, tm, tm_post = cfg["nseq"], cfg["seq_len"], cfg["tm"], cfg["tm_post"]
    ret_out, conv_out, v_out = outs

    proj = _inproj(h, w["w_in"], w["sgu_ln_g"], w["sgu_ln_b"], l, cfg["tm_in"], cfg["rc_in"], seq_len,
                   cfg["pos0"])
    ya, v_out = _sgu(proj, cfg["sgu_w"], cfg["sgu_b"], l, depth, seq_len, tm, cfg["emit_v"], v_out)
    if s_ret is None:
        yb, ret_out = _ret_prompt(proj, w["ret_gn_g"], l, depth, nseq, seq_len, ret_out)
    else:
        yb, ret_out = _ret_sample(proj, w["ret_gn_g"], s_ret, l, seq_len, cfg["ret_bs"], ret_out)
    merged = _merge(ya, yb, proj, w["w_branch_a"], w["w_branch_b"], l, tm, cfg["rc_merge"])
    x, h2 = _matmul_post(merged, w["w_out"], x, w["norm_post1"], mod, l, K_G1, seq_len, cfg["tm_out"],
                         (w["norm_pre2"], l, K_SC2, K_SH2))

    y, conv_out = _ffn_up(h2, w["ffn_w_gate"], w["ffn_w_up"], w["ffn_conv_w"], w["ffn_conv_b"], l, depth,
                          nseq, seq_len, cfg["tm_ffn"], ROW_CHUNK, 512, conv_prev, conv_out)
    nxt = None if l == depth - 1 else (w["norm_pre1"], l + 1, K_SC1, K_SH1)
    x, h_next = _matmul_post(y, w["ffn_w_down_bf16"], x, w["norm_post2"], mod, l, K_G2, seq_len, tm_post, nxt)
    return x, h_next, (ret_out, conv_out, v_out)


def kernel(x_prompt, x_sample, state_ret, state_conv, c_prompt, c_sample, w_ada, b_ada, norm_pre1, norm_post1, norm_pre2, norm_post2, w_in, sgu_w_s, sgu_b_s, sgu_ln_g, sgu_ln_b, ret_gn_g, w_branch_a, w_branch_b, w_out, ffn_w_gate, ffn_w_up, ffn_conv_w, ffn_conv_b, ffn_w_down):
    w = dict(norm_pre1=norm_pre1, norm_post1=norm_post1, norm_pre2=norm_pre2, norm_post2=norm_post2, w_in=w_in,
             sgu_ln_g=sgu_ln_g, sgu_ln_b=sgu_ln_b, ret_gn_g=ret_gn_g, w_branch_a=w_branch_a,
             w_branch_b=w_branch_b, w_out=w_out, ffn_w_gate=ffn_w_gate, ffn_w_up=ffn_w_up,
             ffn_conv_w=ffn_conv_w, ffn_conv_b=ffn_conv_b, ffn_w_down_bf16=_cast_bf16(ffn_w_down))
    depth = w_ada.shape[0]
    nb, seq, d = x_prompt.shape
    ndb, dseq, _ = x_sample.shape

    mod_p, mod_s = _modulation(c_prompt, c_sample, w_ada, b_ada)

    sgu_wp, sgu_bp = _sgu_tables(sgu_w_s, sgu_b_s, seq)
    sgu_ws, sgu_bs = _sgu_tables(sgu_w_s, sgu_b_s, dseq)
    cfg_p = dict(nseq=nb, seq_len=seq, tm=1024, tm_in=2048, rc_in=256, rc_merge=256, tm_ffn=2048, tm_out=512,
                 tm_post=256, pos0=0, sgu_w=sgu_wp, sgu_b=sgu_bp, emit_v=False)
    cfg_s = dict(nseq=ndb, seq_len=dseq, tm=512, tm_in=512, rc_in=128, rc_merge=128, tm_ffn=512, tm_out=512,
                 tm_post=256, pos0=PAST_LEN, ret_bs=8, sgu_w=sgu_ws, sgu_b=sgu_bs, emit_v=True)

    xp = x_prompt.reshape(nb * seq, d)
    xs = x_sample.reshape(ndb * dseq, d)
    hp = _prenorm(xp, norm_pre1, mod_p, 0, K_SC1, K_SH1, seq, 512)
    hs = _prenorm(xs, norm_pre1, mod_s, 0, K_SC1, K_SH1, dseq, 512)
    outs_p = outs_s = (None, None, None)
    for l in range(depth):
        xp, hp, outs_p = _layer(l, depth, xp, hp, mod_p, w, cfg_p, None, None, outs_p)
        xs, hs, outs_s = _layer(l, depth, xs, hs, mod_s, w, cfg_s, state_ret, state_conv, outs_s)
    ret_p, conv_p, _ = outs_p
    ret_s, conv_s, v_s = outs_s
    return (xp.reshape(nb, seq, d), xs.reshape(ndb, dseq, d), ret_p, ret_s, conv_p, conv_s,
            v_s.reshape(depth, ndb, dseq, -1))
```

```python
import functools
import math

import jax
import jax.numpy as jnp
from jax import lax
from jax.experimental import pallas as pl
from jax.experimental.pallas import tpu as pltpu

F32 = jnp.float32
BF16 = jnp.bfloat16

EPS = 1e-6
ROPE_BASE = 10000.0
PAST_LEN = 16384
SGU_GROUPS = 8
SGU_CHUNK = 128
RET_HEADS = 8
RET_CHUNK = 128
CONV_W = 3

SUBLANES = 8
BF16_ROWS = 16
ROW_CHUNK = 128
VMEM_LIMIT_BYTES = 56 * 1024 * 1024


def _params(n_axes):
    return pltpu.CompilerParams(dimension_semantics=("arbitrary",) * n_axes,
                                vmem_limit_bytes=VMEM_LIMIT_BYTES)


def _stacked(kernel, n_in, prev, depth, l):
    if prev is None:
        def first(*refs):
            refs = list(refs)
            so = refs[n_in + 1]
            for k in range(depth):
                if k != l:
                    so[k] = jnp.zeros(so.shape[1:], so.dtype)
            refs[n_in + 1] = so.at[l]
            kernel(*refs)
        return first, [], [], {}

    def later(*refs):
        kernel(*refs[:n_in], *refs[n_in + 1:])
    return later, [pl.BlockSpec(memory_space=pl.ANY)], [prev], {n_in: 1}


def _stacked_spec(prev, depth, l, block, index_fn):
    if prev is None:
        return pl.BlockSpec((depth,) + block, lambda *g: (0,) + index_fn(*g))
    return pl.BlockSpec((None,) + block, lambda *g: (l,) + index_fn(*g))


def _rms(x, g):
    return x * lax.rsqrt(jnp.mean(x * x, axis=-1, keepdims=True) + EPS) * g


def _silu(x):
    return x * jax.nn.sigmoid(x)


def _vec3(p):
    return p.reshape(p.shape[0], 1, p.shape[1])


def _mod_kernel(c_ref, w_ref, b_ref, op_ref, os_ref, *, n_p, s0):
    a = _silu(c_ref[...]).astype(BF16)
    m = jnp.dot(a, w_ref[...].astype(BF16), preferred_element_type=F32) + b_ref[...]
    op_ref[...] = m[0:n_p][:, None, :]
    os_ref[...] = m[s0:][:, None, :]


def _modulation(c_prompt, c_sample, w_ada, b_ada, tn=1024):
    depth, d, n = w_ada.shape
    n_p, n_s = c_prompt.shape[0], c_sample.shape[0]
    s0 = -(-n_p // BF16_ROWS) * BF16_ROWS
    c_all = jnp.concatenate([c_prompt, jnp.zeros((s0 - n_p, d), F32), c_sample], axis=0)
    rows = c_all.shape[0]
    assert rows % BF16_ROWS == 0
    return pl.pallas_call(
        functools.partial(_mod_kernel, n_p=n_p, s0=s0),
        grid=(depth, n // tn),
        in_specs=[pl.BlockSpec((rows, d), lambda l, j: (0, 0)),
                  pl.BlockSpec((None, d, tn), lambda l, j: (l, 0, j)),
                  pl.BlockSpec((None, 1, tn), lambda l, j: (l, 0, j))],
        out_specs=[pl.BlockSpec((None, n_p, 1, tn), lambda l, j: (l, 0, 0, j)),
                   pl.BlockSpec((None, n_s, 1, tn), lambda l, j: (l, 0, 0, j))],
        out_shape=[jax.ShapeDtypeStruct((depth, n_p, 1, n), F32),
                   jax.ShapeDtypeStruct((depth, n_s, 1, n), F32)],
        compiler_params=_params(2),
        name="adaln_mod",
    )(c_all, w_ada, b_ada.reshape(depth, 1, n))


K_SH1, K_SC1, K_G1, K_SH2, K_SC2, K_G2 = range(6)


def _mod_spec(l, k, tm, seq_len, d, grid_rank):
    if seq_len >= tm:
        tps = seq_len // tm
        return pl.BlockSpec((None, 1, 1, d), lambda *g: (l, g[grid_rank - 1] // tps, 0, k))
    return pl.BlockSpec((None, tm // seq_len, 1, d), lambda *g: (l, g[grid_rank - 1], 0, k))


def _rows_of(m_ref, r0, nrows, seq_len):
    d = m_ref.shape[-1]
    if m_ref.shape[0] == 1:
        return m_ref[0]
    nb = nrows // seq_len
    m = m_ref[r0 // seq_len:r0 // seq_len + nb]
    return jnp.broadcast_to(m, (nb, seq_len, d)).reshape(nrows, d)


def _prenorm_kernel(x_ref, g_ref, sc_ref, sh_ref, h_ref, *, seq_len):
    tm = x_ref.shape[0]
    sc = _rows_of(sc_ref, 0, tm, seq_len)
    sh = _rows_of(sh_ref, 0, tm, seq_len)
    h_ref[...] = (_rms(x_ref[...], g_ref[...]) * (1.0 + sc) + sh).astype(BF16)


def _prenorm(x, g_pre, mod, l, k_sc, k_sh, seq_len, tm):
    rows, d = x.shape
    act = pl.BlockSpec((tm, d), lambda i: (i, 0))
    return pl.pallas_call(
        functools.partial(_prenorm_kernel, seq_len=seq_len), grid=(rows // tm,),
        in_specs=[act, pl.BlockSpec((None, 1, d), lambda i: (l, 0, 0)),
                  _mod_spec(l, k_sc, tm, seq_len, d, 1), _mod_spec(l, k_sh, tm, seq_len, d, 1)],
        out_specs=act,
        out_shape=jax.ShapeDtypeStruct((rows, d), BF16),
        compiler_params=_params(1), name="prenorm",
    )(x, _vec3(g_pre), mod, mod)


def _cast_kernel(w_ref, o_ref):
    o_ref[...] = w_ref[...].astype(BF16)


def _cast_bf16(w, tk=512):
    depth, k, n = w.shape
    spec = pl.BlockSpec((None, tk, n), lambda l, i: (l, i, 0))
    return pl.pallas_call(
        _cast_kernel, grid=(depth, k // tk), in_specs=[spec], out_specs=spec,
        out_shape=jax.ShapeDtypeStruct(w.shape, BF16),
        compiler_params=_params(2), name="cast_bf16",
    )(w)


def _mm_post_kernel(m_ref, w_ref, x_ref, gpost_ref, gate_ref, *rest, tm, rc, seq_len, has_next, cast):
    rest = list(rest)
    if has_next:
        gpre_ref, sc_ref, sh_ref = rest[:3]
        rest = rest[3:]
    xo_ref = rest.pop(0)
    h_ref = rest.pop(0) if has_next else None
    if cast:
        wb_ref = rest.pop(0)

        @pl.when(pl.program_id(0) == 0)
        def _():
            wb_ref[...] = w_ref[...].astype(BF16)
    else:
        wb_ref = w_ref

    for c in range(tm // rc):
        r0 = c * rc
        rows = slice(r0, r0 + rc)
        t = jnp.dot(m_ref[rows, :], wb_ref[...], preferred_element_type=F32)
        x = x_ref[rows, :] + _rows_of(gate_ref, r0, rc, seq_len) * _rms(t, gpost_ref[...])
        xo_ref[rows, :] = x
        if has_next:
            h = (_rms(x, gpre_ref[...]) * (1.0 + _rows_of(sc_ref, r0, rc, seq_len))
                 + _rows_of(sh_ref, r0, rc, seq_len))
            h_ref[rows, :] = h.astype(BF16)


def _matmul_post(m, w, x, g_post, mod, l, k_gate, seq_len, tm, nxt):
    rows, k = m.shape
    d = w.shape[2]
    cast = w.dtype != BF16
    has_next = nxt is not None
    act = pl.BlockSpec((tm, d), lambda i: (i, 0))

    def vec():
        return pl.BlockSpec((None, 1, d), lambda i: (l, 0, 0))
    in_specs = [pl.BlockSpec((tm, k), lambda i: (i, 0)),
                pl.BlockSpec((None, k, d), lambda i: (l, 0, 0), pipeline_mode=pl.Buffered(1)),
                act, vec(), _mod_spec(l, k_gate, tm, seq_len, d, 1)]
    args = [m, w, x, _vec3(g_post), mod]
    out_specs, out_shape = [act], [jax.ShapeDtypeStruct((rows, d), F32)]
    if has_next:
        g_pre, ln, k_sc, k_sh = nxt
        in_specs += [pl.BlockSpec((None, 1, d), lambda i: (ln, 0, 0)),
                     _mod_spec(ln, k_sc, tm, seq_len, d, 1), _mod_spec(ln, k_sh, tm, seq_len, d, 1)]
        args += [_vec3(g_pre), mod, mod]
        out_specs.append(act)
        out_shape.append(jax.ShapeDtypeStruct((rows, d), BF16))
    kern = functools.partial(_mm_post_kernel, tm=tm, rc=ROW_CHUNK, seq_len=seq_len, has_next=has_next, cast=cast)
    out = pl.pallas_call(
        kern, grid=(rows // tm,), in_specs=in_specs, out_specs=out_specs, out_shape=out_shape,
        scratch_shapes=[pltpu.VMEM((k, d), BF16)] if cast else [],
        compiler_params=_params(1), name="matmul_post",
    )(*args)
    return (out[0], out[1]) if has_next else (out[0], None)


IN_TILE = 1024
J_U, J_V, J_Q, J_K, J_VR, J_GR, J_GA, J_GB = 0, 1, 2, 3, 4, 5, 6, 8
IN_CHUNKS_PER_GROUP = 4


def _inproj_kernel(h_ref, w_ref, lng_ref, lnb_ref, o_ref, wb_ref, *, tm, rc, unroll, seq_len, pos0, head_dim):
    j = pl.program_id(0)
    i = pl.program_id(1)

    @pl.when(i == 0)
    def _():
        wb_ref[...] = w_ref[...].astype(BF16)

    def run(epilogue):
        def group(gi, carry):
            for u in range(unroll):
                r0 = pl.multiple_of((gi * unroll + u) * rc, rc)
                acc = jnp.dot(h_ref[pl.ds(r0, rc), :], wb_ref[...], preferred_element_type=F32)
                o_ref[pl.ds(r0, rc), :] = epilogue(acc, r0).astype(BF16)
            return carry
        lax.fori_loop(0, tm // (rc * unroll), group, 0)

    def layer_norm_gelu(acc, r0):
        v = jax.nn.gelu(acc)
        mu = jnp.mean(v, axis=-1, keepdims=True)
        var = jnp.mean(jnp.square(v - mu), axis=-1, keepdims=True)
        return (v - mu) * lax.rsqrt(var + EPS) * lng_ref[...] + lnb_ref[...]

    def rope(acc, r0):
        half = head_dim // 2
        lane = lax.broadcasted_iota(jnp.int32, (rc, head_dim), 1)
        row = lax.broadcasted_iota(jnp.int32, (rc, head_dim), 0)
        pos = pos0 + ((i * tm + r0 + row) & (seq_len - 1))
        inv = jnp.exp((lane & (half - 1)).astype(F32) * (-math.log(ROPE_BASE) / half))
        ang = pos.astype(F32) * inv
        scale = jnp.where(j == J_K, head_dim ** -0.5, 1.0).astype(F32)
        cos = jnp.cos(ang) * scale
        sin = jnp.where(lane < half, -jnp.sin(ang), jnp.sin(ang)) * scale
        heads = [acc[:, h * head_dim:(h + 1) * head_dim] for h in range(IN_TILE // head_dim)]
        return jnp.concatenate([xh * cos + pltpu.roll(xh, half, axis=1) * sin for xh in heads], axis=1)

    pl.when(j == J_U)(lambda: run(lambda acc, c: jax.nn.gelu(acc)))
    pl.when(j == J_V)(lambda: run(layer_norm_gelu))
    pl.when((j == J_Q) | (j == J_K))(lambda: run(rope))
    pl.when(j == J_VR)(lambda: run(lambda acc, c: acc))
    pl.when(j == J_GR)(lambda: run(lambda acc, c: _silu(acc)))
    pl.when(j >= J_GA)(lambda: run(lambda acc, c: jax.nn.sigmoid(acc)))


def _inproj(h, w_in, ln_g, ln_b, l, tm, rc, seq_len, pos0):
    rows, k = h.shape
    n = w_in.shape[2]
    assert seq_len & (seq_len - 1) == 0 and n == 10 * IN_TILE
    unroll = min(IN_CHUNKS_PER_GROUP, tm // rc)
    kern = functools.partial(_inproj_kernel, tm=tm, rc=rc, unroll=unroll, seq_len=seq_len, pos0=pos0,
                             head_dim=IN_TILE // RET_HEADS)
    vec = pl.BlockSpec((None, 1, IN_TILE), lambda j, i: (l, 0, 0))
    return pl.pallas_call(
        kern, grid=(n // IN_TILE, rows // tm),
        in_specs=[pl.BlockSpec((tm, k), lambda j, i: (i, 0)),
                  pl.BlockSpec((None, k, IN_TILE), lambda j, i: (l, 0, j)),
                  vec, vec],
        out_specs=pl.BlockSpec((tm, IN_TILE), lambda j, i: (i, j)),
        out_shape=jax.ShapeDtypeStruct((rows, n), BF16),
        scratch_shapes=[pltpu.VMEM((k, IN_TILE), BF16)],
        compiler_params=_params(2), name="inproj",
    )(h, w_in, _vec3(ln_g), _vec3(ln_b))


def _sgu_kernel(u_ref, v_ref, w_ref, b_ref, o_ref, *vo_ref, tm, chunk):
    if vo_ref:
        vo_ref[0][...] = v_ref[...].astype(F32)
    r = lax.broadcasted_iota(jnp.int32, (SGU_CHUNK, SGU_CHUNK), 0)
    c = lax.broadcasted_iota(jnp.int32, (SGU_CHUNK, SGU_CHUNK), 1)
    mask = (c <= r) & ((r & -chunk) == (c & -chunk))
    gd = IN_TILE // SGU_GROUPS
    for g in range(SGU_GROUPS):
        w = jnp.where(mask, w_ref[g], 0.0).astype(BF16)
        bias = b_ref[:, g:g + 1]
        for t in range(tm // SGU_CHUNK):
            rs = slice(t * SGU_CHUNK, (t + 1) * SGU_CHUNK)
            cs = slice(g * gd, (g + 1) * gd)
            z = jnp.dot(w, v_ref[rs, cs], preferred_element_type=F32) + bias
            o_ref[rs, cs] = (u_ref[rs, cs].astype(F32) * z).astype(BF16)


def _sgu_tables(w_s, b_s, seq_len):
    chunk = min(SGU_CHUNK, seq_len)
    reps = SGU_CHUNK // chunk
    w_t = jnp.tile(w_s[:, :, :chunk, :chunk], (1, 1, reps, reps))
    b_t = jnp.tile(jnp.swapaxes(b_s[:, :, :chunk], 1, 2), (1, reps, 1))
    return w_t, b_t


def _sgu(proj, w_t, b_t, l, depth, seq_len, tm, emit_v, prev):
    rows = proj.shape[0]
    kern = functools.partial(_sgu_kernel, tm=tm, chunk=min(SGU_CHUNK, seq_len))
    in_specs = [pl.BlockSpec((tm, IN_TILE), lambda i: (i, J_U)),
                pl.BlockSpec((tm, IN_TILE), lambda i: (i, J_V)),
                pl.BlockSpec((None, SGU_GROUPS, SGU_CHUNK, SGU_CHUNK), lambda i: (l, 0, 0, 0)),
                pl.BlockSpec((None, SGU_CHUNK, SGU_GROUPS), lambda i: (l, 0, 0))]
    out_specs = [pl.BlockSpec((tm, IN_TILE), lambda i: (i, 0))]
    out_shape = [jax.ShapeDtypeStruct((rows, IN_TILE), BF16)]
    cspecs, cargs, aliases = [], [], {}
    if emit_v:
        kern, cspecs, cargs, aliases = _stacked(kern, 4, prev, depth, l)
        out_specs.append(_stacked_spec(prev, depth, l, (tm, IN_TILE), lambda i: (i, 0)))
        out_shape.append(jax.ShapeDtypeStruct((depth, rows, IN_TILE), F32))
    out = pl.pallas_call(
        kern, grid=(rows // tm,),
        in_specs=in_specs + cspecs, out_specs=out_specs, out_shape=out_shape,
        input_output_aliases=aliases,
        compiler_params=_params(1), name="sgu",
    )(proj, proj, w_t, b_t, *cargs)
    return (out[0], out[1]) if emit_v else (out[0], None)


def _log_gamma(h):
    return math.log(1.0 - 2.0 ** (-5.0 - h))


def _ret_decays(h, c):
    lg = _log_gamma(h)
    ri = lax.broadcasted_iota(jnp.int32, (c, c), 0)
    ci = lax.broadcasted_iota(jnp.int32, (c, c), 1)
    diff = (ri - ci).astype(F32)
    decay = jnp.where(ri >= ci, jnp.exp(lg * jnp.maximum(diff, 0.0)), 0.0)
    idx = lax.broadcasted_iota(jnp.int32, (c, 1), 0).astype(F32)
    return decay, jnp.exp(lg * (idx + 1.0)), jnp.exp(lg * (c - 1.0 - idx))


def _ret_chunk(qh, kh, vh, grh, gnh, s, h, c, decays):
    decay, q_dec, k_dec = decays
    scores = lax.dot_general(qh, kh, (((1,), (1,)), ((), ())), preferred_element_type=F32) * decay
    inner = jnp.dot(scores.astype(BF16), vh, preferred_element_type=F32)
    cross = jnp.dot(qh, s.astype(BF16), preferred_element_type=F32) * q_dec
    kd = (kh.astype(F32) * k_dec).astype(BF16)
    s_new = (s * math.exp(_log_gamma(h) * c)
             + lax.dot_general(kd, vh, (((0,), (0,)), ((), ())), preferred_element_type=F32))
    o = inner + cross
    mu = jnp.mean(o, axis=-1, keepdims=True)
    var = jnp.mean(jnp.square(o - mu), axis=-1, keepdims=True)
    y = (o - mu) * lax.rsqrt(var + EPS) * gnh
    return grh * y, s_new


def _ret_prompt_kernel(q_ref, k_ref, v_ref, gr_ref, gn_ref, o_ref, so_ref, s_ref, dec_ref, *, hd):
    n = pl.program_id(1)

    @pl.when(n == 0)
    def _():
        s_ref[...] = jnp.zeros_like(s_ref)

    @pl.when((n == 0) & (pl.program_id(0) == 0))
    def _():
        for h in range(RET_HEADS):
            dec_ref[h] = _ret_decays(h, RET_CHUNK)[0]

    for h in range(RET_HEADS):
        cs = slice(h * hd, (h + 1) * hd)
        _, q_dec, k_dec = _ret_decays(h, RET_CHUNK)
        y, s_new = _ret_chunk(q_ref[:, cs], k_ref[:, cs], v_ref[:, cs], gr_ref[:, cs].astype(F32),
                              gn_ref[:, cs], s_ref[h], h, RET_CHUNK, (dec_ref[h], q_dec, k_dec))
        o_ref[:, cs] = y.astype(BF16)
        s_ref[h] = s_new

    @pl.when(n == pl.num_programs(1) - 1)
    def _():
        so_ref[...] = s_ref[...]


def _ret_prompt(proj, gn_g, l, depth, nseq, seq_len, prev):
    hd = IN_TILE // RET_HEADS
    nc = seq_len // RET_CHUNK

    def col(jc):
        return pl.BlockSpec((RET_CHUNK, IN_TILE), lambda b, n: (b * nc + n, jc))
    kern, cspecs, cargs, aliases = _stacked(functools.partial(_ret_prompt_kernel, hd=hd), 5, prev, depth, l)
    return pl.pallas_call(
        kern, grid=(nseq, nc),
        in_specs=[col(J_Q), col(J_K), col(J_VR), col(J_GR),
                  pl.BlockSpec((None, 1, IN_TILE), lambda b, n: (l, 0, 0))] + cspecs,
        out_specs=[pl.BlockSpec((RET_CHUNK, IN_TILE), lambda b, n: (b * nc + n, 0)),
                   _stacked_spec(prev, depth, l, (None, RET_HEADS, hd, hd), lambda b, n: (b, 0, 0, 0))],
        out_shape=[jax.ShapeDtypeStruct((nseq * seq_len, IN_TILE), BF16),
                   jax.ShapeDtypeStruct((depth, nseq, RET_HEADS, hd, hd), F32)],
        scratch_shapes=[pltpu.VMEM((RET_HEADS, hd, hd), F32),
                        pltpu.VMEM((RET_HEADS, RET_CHUNK, RET_CHUNK), F32)],
        input_output_aliases=aliases,
        compiler_params=_params(2), name="ret_prompt",
    )(proj, proj, proj, proj, _vec3(gn_g), *cargs)


def _ret_sample_kernel(q_ref, k_ref, v_ref, gr_ref, gn_ref, s_ref, o_ref, so_ref, y_ref, *, hd, bs, seq_len):
    q = q_ref[...].astype(F32)
    k = k_ref[...].astype(F32)
    v = v_ref[...].astype(F32)
    gr = gr_ref[...].astype(F32)
    for h in range(RET_HEADS):
        cs = slice(h * hd, (h + 1) * hd)
        decays = _ret_decays(h, seq_len)
        for b in range(bs):
            rs = slice(b * seq_len, (b + 1) * seq_len)
            y, s_new = _ret_chunk(q[rs, cs].astype(BF16), k[rs, cs].astype(BF16), v[rs, cs].astype(BF16),
                                  gr[rs, cs], gn_ref[:, cs], s_ref[b, h], h, seq_len, decays)
            y_ref[rs, cs] = y
            so_ref[b, h] = s_new
    o_ref[...] = y_ref[...].astype(BF16)


def _ret_sample(proj, gn_g, state, l, seq_len, bs, prev):
    depth, nseq = state.shape[:2]
    hd = IN_TILE // RET_HEADS

    def col(jc):
        return pl.BlockSpec((bs * seq_len, IN_TILE), lambda i: (i, jc))
    kern, cspecs, cargs, aliases = _stacked(
        functools.partial(_ret_sample_kernel, hd=hd, bs=bs, seq_len=seq_len), 6, prev, depth, l)
    return pl.pallas_call(
        kern, grid=(nseq // bs,),
        in_specs=[col(J_Q), col(J_K), col(J_VR), col(J_GR),
                  pl.BlockSpec((None, 1, IN_TILE), lambda i: (l, 0, 0)),
                  pl.BlockSpec((None, bs, RET_HEADS, hd, hd), lambda i: (l, i, 0, 0, 0))] + cspecs,
        out_specs=[pl.BlockSpec((bs * seq_len, IN_TILE), lambda i: (i, 0)),
                   _stacked_spec(prev, depth, l, (bs, RET_HEADS, hd, hd), lambda i: (i, 0, 0, 0))],
        out_shape=[jax.ShapeDtypeStruct((nseq * seq_len, IN_TILE), BF16),
                   jax.ShapeDtypeStruct(state.shape, F32)],
        scratch_shapes=[pltpu.VMEM((bs * seq_len, IN_TILE), F32)],
        input_output_aliases=aliases,
        compiler_params=_params(1), name="ret_sample",
    )(proj, proj, proj, proj, _vec3(gn_g), state, *cargs)


def _merge_kernel(ya_ref, yb_ref, ga_ref, gb_ref, wa_ref, wb_ref, o_ref, wab_ref, wbb_ref, *, rc):
    @pl.when(pl.program_id(1) == 0)
    def _():
        wab_ref[...] = wa_ref[...].astype(BF16)
        wbb_ref[...] = wb_ref[...].astype(BF16)
    for c in range(o_ref.shape[0] // rc):
        rows = slice(c * rc, (c + 1) * rc)
        a = jnp.dot(ya_ref[rows, :], wab_ref[...], preferred_element_type=F32)
        b = jnp.dot(yb_ref[rows, :], wbb_ref[...], preferred_element_type=F32)
        o_ref[rows, :] = (ga_ref[rows, :].astype(F32) * a + gb_ref[rows, :].astype(F32) * b).astype(BF16)


def _merge(ya, yb, proj, w_a, w_b, l, tm, rc):
    rows, k = ya.shape
    n = w_a.shape[2]
    tn = IN_TILE
    row = pl.BlockSpec((tm, k), lambda j, i: (i, 0))
    wsp = pl.BlockSpec((None, k, tn), lambda j, i: (l, 0, j))
    return pl.pallas_call(
        functools.partial(_merge_kernel, rc=rc), grid=(n // tn, rows // tm),
        in_specs=[row, row,
                  pl.BlockSpec((tm, tn), lambda j, i: (i, J_GA + j)),
                  pl.BlockSpec((tm, tn), lambda j, i: (i, J_GB + j)),
                  wsp, wsp],
        out_specs=pl.BlockSpec((tm, tn), lambda j, i: (i, j)),
        out_shape=jax.ShapeDtypeStruct((rows, n), BF16),
        scratch_shapes=[pltpu.VMEM((k, tn), BF16), pltpu.VMEM((k, tn), BF16)],
        compiler_params=_params(2), name="merge",
    )(ya, yb, proj, proj, w_a, w_b)


def _ffn_up_kernel(h_ref, wg_ref, wu_ref, cw_ref, cb_ref, *rest, tm, rc, seq_len, has_state):
    if has_state:
        st_ref, y_ref, so_ref, wgb_ref, wub_ref, gbuf_ref = rest
    else:
        y_ref, so_ref, wgb_ref, wub_ref, gbuf_ref = rest
    i = pl.program_id(1)
    pad = SUBLANES
    keep = CONV_W - 1

    @pl.when(i == 0)
    def _():
        wgb_ref[...] = wg_ref[...].astype(BF16)
        wub_ref[...] = wu_ref[...].astype(BF16)

    tn = y_ref.shape[1]
    if has_state:
        gbuf_ref[0:pad, :] = jnp.zeros((pad, tn), F32)
    else:
        @pl.when(((i * tm) & (seq_len - 1)) == 0)
        def _():
            gbuf_ref[0:pad, :] = jnp.zeros((pad, tn), F32)
    for c in range(tm // rc):
        r0 = c * rc
        h = h_ref[r0:r0 + rc, :]
        g = jnp.dot(h, wgb_ref[...], preferred_element_type=F32)
        up = jnp.dot(h, wub_ref[...], preferred_element_type=F32)
        gbuf_ref[pad + r0:pad + r0 + rc, :] = g
        g1 = gbuf_ref[pad - 1 + r0:pad - 1 + r0 + rc, :]
        g2 = gbuf_ref[pad - 2 + r0:pad - 2 + r0 + rc, :]
        if has_state:
            bs = rc // seq_len
            st = st_ref[c * bs:(c + 1) * bs]
            s0 = jnp.broadcast_to(st[:, 0:1, :], (bs, seq_len, tn)).reshape(rc, tn)
            s1 = jnp.broadcast_to(st[:, 1:2, :], (bs, seq_len, tn)).reshape(rc, tn)
            t = lax.broadcasted_iota(jnp.int32, (rc, tn), 0) & (seq_len - 1)
            g1 = jnp.where(t == 0, s1, g1)
            g2 = jnp.where(t == 0, s0, jnp.where(t == 1, s1, g2))
            so_ref[c * bs:(c + 1) * bs] = g.reshape(bs, seq_len, tn)[:, seq_len - keep:, :]
        conv = g2 * cw_ref[0:1, :] + g1 * cw_ref[1:2, :] + g * cw_ref[2:3, :] + cb_ref[...]
        y_ref[r0:r0 + rc, :] = (jax.nn.gelu(conv) * up).astype(BF16)
    if not has_state:
        last = gbuf_ref[pad + tm - keep:pad + tm, :]
        gbuf_ref[pad - keep:pad, :] = last
        so_ref[...] = last


def _ffn_up(h, w_gate, w_up, conv_w, conv_b, l, depth, nseq, seq_len, tm, rc, tn, state, prev):
    rows, k = h.shape
    n = w_gate.shape[2]
    keep = CONV_W - 1
    has_state = state is not None
    assert seq_len & (seq_len - 1) == 0 and seq_len >= CONV_W
    kern = functools.partial(_ffn_up_kernel, tm=tm, rc=rc, seq_len=seq_len, has_state=has_state)
    wsp = pl.BlockSpec((None, k, tn), lambda j, i: (l, 0, j))
    in_specs = [pl.BlockSpec((tm, k), lambda j, i: (i, 0)), wsp, wsp,
                pl.BlockSpec((None, CONV_W, tn), lambda j, i: (l, 0, j)),
                pl.BlockSpec((None, 1, tn), lambda j, i: (l, 0, j))]
    args = [h, w_gate, w_up, conv_w, _vec3(conv_b)]
    if has_state:
        bs = tm // seq_len
        in_specs.append(pl.BlockSpec((None, bs, keep, tn), lambda j, i: (l, i, 0, j)))
        args.append(state)
        so_spec = _stacked_spec(prev, depth, l, (bs, keep, tn), lambda j, i: (i, 0, j))
    else:
        tps = seq_len // tm
        so_spec = _stacked_spec(prev, depth, l, (None, keep, tn), lambda j, i: (i // tps, 0, j))
    kern, cspecs, cargs, aliases = _stacked(kern, len(args), prev, depth, l)
    return pl.pallas_call(
        kern, grid=(n // tn, rows // tm),
        in_specs=in_specs + cspecs,
        out_specs=[pl.BlockSpec((tm, tn), lambda j, i: (i, j)), so_spec],
        out_shape=[jax.ShapeDtypeStruct((rows, n), BF16),
                   jax.ShapeDtypeStruct((depth, nseq, keep, n), F32)],
        scratch_shapes=[pltpu.VMEM((k, tn), BF16), pltpu.VMEM((k, tn), BF16),
                        pltpu.VMEM((tm + SUBLANES, tn), F32)],
        input_output_aliases=aliases,
        compiler_params=_params(2), name="ffn_up",
    )(*args, *cargs)


def _layer(l, depth, x, h, mod, w, cfg, s_ret, conv_prev, outs):
    nseq, seq_len, tm, tm_post = cfg["nseq"], cfg["seq_len"], cfg["tm"], cfg["tm_post"]
    ret_out, conv_out, v_out = outs

    proj = _inproj(h, w["w_in"], w["sgu_ln_g"], w["sgu_ln_b"], l, cfg["tm_in"], cfg["rc_in"], seq_len,
                   cfg["pos0"])
    ya, v_out = _sgu(proj, cfg["sgu_w"], cfg["sgu_b"], l, depth, seq_len, tm, cfg["emit_v"], v_out)
    if s_ret is None:
        yb, ret_out = _ret_prompt(proj, w["ret_gn_g"], l, depth, nseq, seq_len, ret_out)
    else:
        yb, ret_out = _ret_sample(proj, w["ret_gn_g"], s_ret, l, seq_len, cfg["ret_bs"], ret_out)
    merged = _merge(ya, yb, proj, w["w_branch_a"], w["w_branch_b"], l, tm, cfg["rc_merge"])
    x, h2 = _matmul_post(merged, w["w_out"], x, w["norm_post1"], mod, l, K_G1, seq_len, cfg["tm_out"],
                         (w["norm_pre2"], l, K_SC2, K_SH2))

    y, conv_out = _ffn_up(h2, w["ffn_w_gate"], w["ffn_w_up"], w["ffn_conv_w"], w["ffn_conv_b"], l, depth,
                          nseq, seq_len, cfg["tm_ffn"], ROW_CHUNK, 512, conv_prev, conv_out)
    nxt = None if l == depth - 1 else (w["norm_pre1"], l + 1, K_SC1, K_SH1)
    x, h_next = _matmul_post(y, w["ffn_w_down_bf16"], x, w["norm_post2"], mod, l, K_G2, seq_len, tm_post, nxt)
    return x, h_next, (ret_out, conv_out, v_out)


def kernel(x_prompt, x_sample, state_ret, state_conv, c_prompt, c_sample, w_ada, b_ada, norm_pre1, norm_post1, norm_pre2, norm_post2, w_in, sgu_w_s, sgu_b_s, sgu_ln_g, sgu_ln_b, ret_gn_g, w_branch_a, w_branch_b, w_out, ffn_w_gate, ffn_w_up, ffn_conv_w, ffn_conv_b, ffn_w_down):
    w = dict(norm_pre1=norm_pre1, norm_post1=norm_post1, norm_pre2=norm_pre2, norm_post2=norm_post2, w_in=w_in,
             sgu_ln_g=sgu_ln_g, sgu_ln_b=sgu_ln_b, ret_gn_g=ret_gn_g, w_branch_a=w_branch_a,
             w_branch_b=w_branch_b, w_out=w_out, ffn_w_gate=ffn_w_gate, ffn_w_up=ffn_w_up,
             ffn_conv_w=ffn_conv_w, ffn_conv_b=ffn_conv_b, ffn_w_down_bf16=_cast_bf16(ffn_w_down))
    depth = w_ada.shape[0]
    nb, seq, d = x_prompt.shape
    ndb, dseq, _ = x_sample.shape

    mod_p, mod_s = _modulation(c_prompt, c_sample, w_ada, b_ada)

    sgu_wp, sgu_bp = _sgu_tables(sgu_w_s, sgu_b_s, seq)
    sgu_ws, sgu_bs = _sgu_tables(sgu_w_s, sgu_b_s, dseq)
    cfg_p = dict(nseq=nb, seq_len=seq, tm=1024, tm_in=2048, rc_in=128, rc_merge=256, tm_ffn=2048, tm_out=512,
                 tm_post=256, pos0=0, sgu_w=sgu_wp, sgu_b=sgu_bp, emit_v=False)
    cfg_s = dict(nseq=ndb, seq_len=dseq, tm=512, tm_in=512, rc_in=128, rc_merge=128, tm_ffn=512, tm_out=512,
                 tm_post=256, pos0=PAST_LEN, ret_bs=8, sgu_w=sgu_ws, sgu_b=sgu_bs, emit_v=True)

    xp = x_prompt.reshape(nb * seq, d)
    xs = x_sample.reshape(ndb * dseq, d)
    hp = _prenorm(xp, norm_pre1, mod_p, 0, K_SC1, K_SH1, seq, 512)
    hs = _prenorm(xs, norm_pre1, mod_s, 0, K_SC1, K_SH1, dseq, 512)
    outs_p = outs_s = (None, None, None)
    for l in range(depth):
        xp, hp, outs_p = _layer(l, depth, xp, hp, mod_p, w, cfg_p, None, None, outs_p)
        xs, hs, outs_s = _layer(l, depth, xs, hs, mod_s, w, cfg_s, state_ret, state_conv, outs_s)
    ret_p, conv_p, _ = outs_p
    ret_s, conv_s, v_s = outs_s
    return (xp.reshape(nb, seq, d), xs.reshape(ndb, dseq, d), ret_p, ret_s, conv_p, conv_s,
            v_s.reshape(depth, ndb, dseq, -1))
```

```python
import functools
import math

import jax
import jax.numpy as jnp
from jax import lax
from jax.experimental import pallas as pl
from jax.experimental.pallas import tpu as pltpu

F32 = jnp.float32
BF16 = jnp.bfloat16

EPS = 1e-6
ROPE_BASE = 10000.0
PAST_LEN = 16384
SGU_GROUPS = 8
SGU_CHUNK = 128
RET_HEADS = 8
RET_CHUNK = 128
CONV_W = 3

SUBLANES = 8
BF16_ROWS = 16
ROW_CHUNK = 128
VMEM_LIMIT_BYTES = 56 * 1024 * 1024


def _params(n_axes):
    return pltpu.CompilerParams(dimension_semantics=("arbitrary",) * n_axes,
                                vmem_limit_bytes=VMEM_LIMIT_BYTES)


def _stacked(kernel, n_in, prev, depth, l):
    if prev is None:
        def first(*refs):
            refs = list(refs)
            so = refs[n_in + 1]
            for k in range(depth):
                if k != l:
                    so[k] = jnp.zeros(so.shape[1:], so.dtype)
            refs[n_in + 1] = so.at[l]
            kernel(*refs)
        return first, [], [], {}

    def later(*refs):
        kernel(*refs[:n_in], *refs[n_in + 1:])
    return later, [pl.BlockSpec(memory_space=pl.ANY)], [prev], {n_in: 1}


def _stacked_spec(prev, depth, l, block, index_fn):
    if prev is None:
        return pl.BlockSpec((depth,) + block, lambda *g: (0,) + index_fn(*g))
    return pl.BlockSpec((None,) + block, lambda *g: (l,) + index_fn(*g))


def _rms(x, g):
    return x * lax.rsqrt(jnp.mean(x * x, axis=-1, keepdims=True) + EPS) * g


def _silu(x):
    return x * jax.nn.sigmoid(x)


def _vec3(p):
    return p.reshape(p.shape[0], 1, p.shape[1])


def _mod_kernel(c_ref, w_ref, b_ref, op_ref, os_ref, *, n_p, s0):
    a = _silu(c_ref[...]).astype(BF16)
    m = jnp.dot(a, w_ref[...].astype(BF16), preferred_element_type=F32) + b_ref[...]
    op_ref[...] = m[0:n_p][:, None, :]
    os_ref[...] = m[s0:][:, None, :]


def _modulation(c_prompt, c_sample, w_ada, b_ada, tn=1024):
    depth, d, n = w_ada.shape
    n_p, n_s = c_prompt.shape[0], c_sample.shape[0]
    s0 = -(-n_p // BF16_ROWS) * BF16_ROWS
    c_all = jnp.concatenate([c_prompt, jnp.zeros((s0 - n_p, d), F32), c_sample], axis=0)
    rows = c_all.shape[0]
    assert rows % BF16_ROWS == 0
    return pl.pallas_call(
        functools.partial(_mod_kernel, n_p=n_p, s0=s0),
        grid=(depth, n // tn),
        in_specs=[pl.BlockSpec((rows, d), lambda l, j: (0, 0)),
                  pl.BlockSpec((None, d, tn), lambda l, j: (l, 0, j)),
                  pl.BlockSpec((None, 1, tn), lambda l, j: (l, 0, j))],
        out_specs=[pl.BlockSpec((None, n_p, 1, tn), lambda l, j: (l, 0, 0, j)),
                   pl.BlockSpec((None, n_s, 1, tn), lambda l, j: (l, 0, 0, j))],
        out_shape=[jax.ShapeDtypeStruct((depth, n_p, 1, n), F32),
                   jax.ShapeDtypeStruct((depth, n_s, 1, n), F32)],
        compiler_params=_params(2),
        name="adaln_mod",
    )(c_all, w_ada, b_ada.reshape(depth, 1, n))


K_SH1, K_SC1, K_G1, K_SH2, K_SC2, K_G2 = range(6)


def _mod_spec(l, k, tm, seq_len, d, grid_rank):
    if seq_len >= tm:
        tps = seq_len // tm
        return pl.BlockSpec((None, 1, 1, d), lambda *g: (l, g[grid_rank - 1] // tps, 0, k))
    return pl.BlockSpec((None, tm // seq_len, 1, d), lambda *g: (l, g[grid_rank - 1], 0, k))


def _rows_of(m_ref, r0, nrows, seq_len):
    d = m_ref.shape[-1]
    if m_ref.shape[0] == 1:
        return m_ref[0]
    nb = nrows // seq_len
    m = m_ref[r0 // seq_len:r0 // seq_len + nb]
    return jnp.broadcast_to(m, (nb, seq_len, d)).reshape(nrows, d)


def _prenorm_kernel(x_ref, g_ref, sc_ref, sh_ref, h_ref, *, seq_len):
    tm = x_ref.shape[0]
    sc = _rows_of(sc_ref, 0, tm, seq_len)
    sh = _rows_of(sh_ref, 0, tm, seq_len)
    h_ref[...] = (_rms(x_ref[...], g_ref[...]) * (1.0 + sc) + sh).astype(BF16)


def _prenorm(x, g_pre, mod, l, k_sc, k_sh, seq_len, tm):
    rows, d = x.shape
    act = pl.BlockSpec((tm, d), lambda i: (i, 0))
    return pl.pallas_call(
        functools.partial(_prenorm_kernel, seq_len=seq_len), grid=(rows // tm,),
        in_specs=[act, pl.BlockSpec((None, 1, d), lambda i: (l, 0, 0)),
                  _mod_spec(l, k_sc, tm, seq_len, d, 1), _mod_spec(l, k_sh, tm, seq_len, d, 1)],
        out_specs=act,
        out_shape=jax.ShapeDtypeStruct((rows, d), BF16),
        compiler_params=_params(1), name="prenorm",
    )(x, _vec3(g_pre), mod, mod)


def _cast_kernel(w_ref, o_ref):
    o_ref[...] = w_ref[...].astype(BF16)


def _cast_bf16(w, tk=512):
    depth, k, n = w.shape
    spec = pl.BlockSpec((None, tk, n), lambda l, i: (l, i, 0))
    return pl.pallas_call(
        _cast_kernel, grid=(depth, k // tk), in_specs=[spec], out_specs=spec,
        out_shape=jax.ShapeDtypeStruct(w.shape, BF16),
        compiler_params=_params(2), name="cast_bf16",
    )(w)


def _mm_post_kernel(m_ref, w_ref, x_ref, gpost_ref, gate_ref, *rest, tm, rc, seq_len, has_next, cast):
    rest = list(rest)
    if has_next:
        gpre_ref, sc_ref, sh_ref = rest[:3]
        rest = rest[3:]
    xo_ref = rest.pop(0)
    h_ref = rest.pop(0) if has_next else None
    if cast:
        wb_ref = rest.pop(0)

        @pl.when(pl.program_id(0) == 0)
        def _():
            wb_ref[...] = w_ref[...].astype(BF16)
    else:
        wb_ref = w_ref

    nk, _, tk = m_ref.shape
    for c in range(tm // rc):
        r0 = c * rc
        rows = slice(r0, r0 + rc)
        t = jnp.dot(m_ref[0, rows, :], wb_ref[0:tk, :], preferred_element_type=F32)
        for kk in range(1, nk):
            t += jnp.dot(m_ref[kk, rows, :], wb_ref[kk * tk:(kk + 1) * tk, :], preferred_element_type=F32)
        x = x_ref[rows, :] + _rows_of(gate_ref, r0, rc, seq_len) * _rms(t, gpost_ref[...])
        xo_ref[rows, :] = x
        if has_next:
            h = (_rms(x, gpre_ref[...]) * (1.0 + _rows_of(sc_ref, r0, rc, seq_len))
                 + _rows_of(sh_ref, r0, rc, seq_len))
            h_ref[rows, :] = h.astype(BF16)


def _matmul_post(m, w, x, g_post, mod, l, k_gate, seq_len, tm, nxt):
    nk, rows, tk = m.shape
    k = nk * tk
    d = w.shape[2]
    assert k == w.shape[1]
    cast = w.dtype != BF16
    has_next = nxt is not None
    act = pl.BlockSpec((tm, d), lambda i: (i, 0))

    def vec():
        return pl.BlockSpec((None, 1, d), lambda i: (l, 0, 0))
    in_specs = [pl.BlockSpec((nk, tm, tk), lambda i: (0, i, 0)),
                pl.BlockSpec((None, k, d), lambda i: (l, 0, 0), pipeline_mode=pl.Buffered(1)),
                act, vec(), _mod_spec(l, k_gate, tm, seq_len, d, 1)]
    args = [m, w, x, _vec3(g_post), mod]
    out_specs, out_shape = [act], [jax.ShapeDtypeStruct((rows, d), F32)]
    if has_next:
        g_pre, ln, k_sc, k_sh = nxt
        in_specs += [pl.BlockSpec((None, 1, d), lambda i: (ln, 0, 0)),
                     _mod_spec(ln, k_sc, tm, seq_len, d, 1), _mod_spec(ln, k_sh, tm, seq_len, d, 1)]
        args += [_vec3(g_pre), mod, mod]
        out_specs.append(act)
        out_shape.append(jax.ShapeDtypeStruct((rows, d), BF16))
    kern = functools.partial(_mm_post_kernel, tm=tm, rc=ROW_CHUNK, seq_len=seq_len, has_next=has_next, cast=cast)
    out = pl.pallas_call(
        kern, grid=(rows // tm,), in_specs=in_specs, out_specs=out_specs, out_shape=out_shape,
        scratch_shapes=[pltpu.VMEM((k, d), BF16)] if cast else [],
        compiler_params=_params(1), name="matmul_post",
    )(*args)
    return (out[0], out[1]) if has_next else (out[0], None)


IN_TILE = 1024
J_U, J_V, J_Q, J_K, J_VR, J_GR, J_GA, J_GB = 0, 1, 2, 3, 4, 5, 6, 8
IN_CHUNKS_PER_GROUP = 4


def _inproj_kernel(h_ref, w_ref, lng_ref, lnb_ref, o_ref, wb_ref, *, tm, rc, unroll, seq_len, pos0, head_dim):
    j = pl.program_id(0)
    i = pl.program_id(1)

    @pl.when(i == 0)
    def _():
        wb_ref[...] = w_ref[...].astype(BF16)

    def run(epilogue):
        def group(gi, carry):
            for u in range(unroll):
                r0 = pl.multiple_of((gi * unroll + u) * rc, rc)
                acc = jnp.dot(h_ref[pl.ds(r0, rc), :], wb_ref[...], preferred_element_type=F32)
                o_ref[pl.ds(r0, rc), :] = epilogue(acc, r0).astype(BF16)
            return carry
        lax.fori_loop(0, tm // (rc * unroll), group, 0)

    def layer_norm_gelu(acc, r0):
        v = jax.nn.gelu(acc)
        mu = jnp.mean(v, axis=-1, keepdims=True)
        var = jnp.mean(jnp.square(v - mu), axis=-1, keepdims=True)
        return (v - mu) * lax.rsqrt(var + EPS) * lng_ref[...] + lnb_ref[...]

    def rope(acc, r0):
        half = head_dim // 2
        lane = lax.broadcasted_iota(jnp.int32, (rc, head_dim), 1)
        row = lax.broadcasted_iota(jnp.int32, (rc, head_dim), 0)
        pos = pos0 + ((i * tm + r0 + row) & (seq_len - 1))
        inv = jnp.exp((lane & (half - 1)).astype(F32) * (-math.log(ROPE_BASE) / half))
        ang = pos.astype(F32) * inv
        scale = jnp.where(j == J_K, head_dim ** -0.5, 1.0).astype(F32)
        cos = jnp.cos(ang) * scale
        sin = jnp.where(lane < half, -jnp.sin(ang), jnp.sin(ang)) * scale
        heads = [acc[:, h * head_dim:(h + 1) * head_dim] for h in range(IN_TILE // head_dim)]
        return jnp.concatenate([xh * cos + pltpu.roll(xh, half, axis=1) * sin for xh in heads], axis=1)

    pl.when(j == J_U)(lambda: run(lambda acc, c: jax.nn.gelu(acc)))
    pl.when(j == J_V)(lambda: run(layer_norm_gelu))
    pl.when((j == J_Q) | (j == J_K))(lambda: run(rope))
    pl.when(j == J_VR)(lambda: run(lambda acc, c: acc))
    pl.when(j == J_GR)(lambda: run(lambda acc, c: _silu(acc)))
    pl.when(j >= J_GA)(lambda: run(lambda acc, c: jax.nn.sigmoid(acc)))


def _inproj(h, w_in, ln_g, ln_b, l, tm, rc, seq_len, pos0):
    rows, k = h.shape
    n = w_in.shape[2]
    assert seq_len & (seq_len - 1) == 0 and n == 10 * IN_TILE
    unroll = min(IN_CHUNKS_PER_GROUP, tm // rc)
    kern = functools.partial(_inproj_kernel, tm=tm, rc=rc, unroll=unroll, seq_len=seq_len, pos0=pos0,
                             head_dim=IN_TILE // RET_HEADS)
    vec = pl.BlockSpec((None, 1, IN_TILE), lambda j, i: (l, 0, 0))
    return pl.pallas_call(
        kern, grid=(n // IN_TILE, rows // tm),
        in_specs=[pl.BlockSpec((tm, k), lambda j, i: (i, 0)),
                  pl.BlockSpec((None, k, IN_TILE), lambda j, i: (l, 0, j)),
                  vec, vec],
        out_specs=pl.BlockSpec((None, tm, IN_TILE), lambda j, i: (j, i, 0)),
        out_shape=jax.ShapeDtypeStruct((n // IN_TILE, rows, IN_TILE), BF16),
        scratch_shapes=[pltpu.VMEM((k, IN_TILE), BF16)],
        compiler_params=_params(2), name="inproj",
    )(h, w_in, _vec3(ln_g), _vec3(ln_b))


def _sgu_kernel(u_ref, v_ref, w_ref, b_ref, o_ref, *vo_ref, tm, chunk):
    if vo_ref:
        vo_ref[0][...] = v_ref[...].astype(F32)
    r = lax.broadcasted_iota(jnp.int32, (SGU_CHUNK, SGU_CHUNK), 0)
    c = lax.broadcasted_iota(jnp.int32, (SGU_CHUNK, SGU_CHUNK), 1)
    mask = (c <= r) & ((r & -chunk) == (c & -chunk))
    gd = IN_TILE // SGU_GROUPS
    for g in range(SGU_GROUPS):
        w = jnp.where(mask, w_ref[g], 0.0).astype(BF16)
        bias = b_ref[:, g:g + 1]
        for t in range(tm // SGU_CHUNK):
            rs = slice(t * SGU_CHUNK, (t + 1) * SGU_CHUNK)
            cs = slice(g * gd, (g + 1) * gd)
            z = jnp.dot(w, v_ref[rs, cs], preferred_element_type=F32) + bias
            o_ref[rs, cs] = (u_ref[rs, cs].astype(F32) * z).astype(BF16)


def _sgu_tables(w_s, b_s, seq_len):
    chunk = min(SGU_CHUNK, seq_len)
    reps = SGU_CHUNK // chunk
    w_t = jnp.tile(w_s[:, :, :chunk, :chunk], (1, 1, reps, reps))
    b_t = jnp.tile(jnp.swapaxes(b_s[:, :, :chunk], 1, 2), (1, reps, 1))
    return w_t, b_t


def _sgu(proj, w_t, b_t, l, depth, seq_len, tm, emit_v, prev):
    rows = proj.shape[1]
    kern = functools.partial(_sgu_kernel, tm=tm, chunk=min(SGU_CHUNK, seq_len))
    in_specs = [pl.BlockSpec((None, tm, IN_TILE), lambda i: (J_U, i, 0)),
                pl.BlockSpec((None, tm, IN_TILE), lambda i: (J_V, i, 0)),
                pl.BlockSpec((None, SGU_GROUPS, SGU_CHUNK, SGU_CHUNK), lambda i: (l, 0, 0, 0)),
                pl.BlockSpec((None, SGU_CHUNK, SGU_GROUPS), lambda i: (l, 0, 0))]
    out_specs = [pl.BlockSpec((tm, IN_TILE), lambda i: (i, 0))]
    out_shape = [jax.ShapeDtypeStruct((rows, IN_TILE), BF16)]
    cspecs, cargs, aliases = [], [], {}
    if emit_v:
        kern, cspecs, cargs, aliases = _stacked(kern, 4, prev, depth, l)
        out_specs.append(_stacked_spec(prev, depth, l, (tm, IN_TILE), lambda i: (i, 0)))
        out_shape.append(jax.ShapeDtypeStruct((depth, rows, IN_TILE), F32))
    out = pl.pallas_call(
        kern, grid=(rows // tm,),
        in_specs=in_specs + cspecs, out_specs=out_specs, out_shape=out_shape,
        input_output_aliases=aliases,
        compiler_params=_params(1), name="sgu",
    )(proj, proj, w_t, b_t, *cargs)
    return (out[0], out[1]) if emit_v else (out[0], None)


def _log_gamma(h):
    return math.log(1.0 - 2.0 ** (-5.0 - h))


def _ret_decays(h, c):
    lg = _log_gamma(h)
    ri = lax.broadcasted_iota(jnp.int32, (c, c), 0)
    ci = lax.broadcasted_iota(jnp.int32, (c, c), 1)
    diff = (ri - ci).astype(F32)
    decay = jnp.where(ri >= ci, jnp.exp(lg * jnp.maximum(diff, 0.0)), 0.0)
    idx = lax.broadcasted_iota(jnp.int32, (c, 1), 0).astype(F32)
    return decay, jnp.exp(lg * (idx + 1.0)), jnp.exp(lg * (c - 1.0 - idx))


def _ret_chunk(qh, kh, vh, grh, gnh, s, h, c, decays):
    decay, q_dec, k_dec = decays
    scores = lax.dot_general(qh, kh, (((1,), (1,)), ((), ())), preferred_element_type=F32) * decay
    inner = jnp.dot(scores.astype(BF16), vh, preferred_element_type=F32)
    cross = jnp.dot(qh, s.astype(BF16), preferred_element_type=F32) * q_dec
    kd = (kh.astype(F32) * k_dec).astype(BF16)
    s_new = (s * math.exp(_log_gamma(h) * c)
             + lax.dot_general(kd, vh, (((0,), (0,)), ((), ())), preferred_element_type=F32))
    o = inner + cross
    mu = jnp.mean(o, axis=-1, keepdims=True)
    var = jnp.mean(jnp.square(o - mu), axis=-1, keepdims=True)
    y = (o - mu) * lax.rsqrt(var + EPS) * gnh
    return grh * y, s_new


def _ret_prompt_kernel(q_ref, k_ref, v_ref, gr_ref, gn_ref, o_ref, so_ref, s_ref, dec_ref, *, hd):
    n = pl.program_id(1)

    @pl.when(n == 0)
    def _():
        s_ref[...] = jnp.zeros_like(s_ref)

    @pl.when((n == 0) & (pl.program_id(0) == 0))
    def _():
        for h in range(RET_HEADS):
            dec_ref[h] = _ret_decays(h, RET_CHUNK)[0]

    for h in range(RET_HEADS):
        cs = slice(h * hd, (h + 1) * hd)
        _, q_dec, k_dec = _ret_decays(h, RET_CHUNK)
        y, s_new = _ret_chunk(q_ref[:, cs], k_ref[:, cs], v_ref[:, cs], gr_ref[:, cs].astype(F32),
                              gn_ref[:, cs], s_ref[h], h, RET_CHUNK, (dec_ref[h], q_dec, k_dec))
        o_ref[:, cs] = y.astype(BF16)
        s_ref[h] = s_new

    @pl.when(n == pl.num_programs(1) - 1)
    def _():
        so_ref[...] = s_ref[...]


def _ret_prompt(proj, gn_g, l, depth, nseq, seq_len, prev):
    hd = IN_TILE // RET_HEADS
    nc = seq_len // RET_CHUNK

    def col(jc):
        return pl.BlockSpec((None, RET_CHUNK, IN_TILE), lambda b, n: (jc, b * nc + n, 0))
    kern, cspecs, cargs, aliases = _stacked(functools.partial(_ret_prompt_kernel, hd=hd), 5, prev, depth, l)
    return pl.pallas_call(
        kern, grid=(nseq, nc),
        in_specs=[col(J_Q), col(J_K), col(J_VR), col(J_GR),
                  pl.BlockSpec((None, 1, IN_TILE), lambda b, n: (l, 0, 0))] + cspecs,
        out_specs=[pl.BlockSpec((RET_CHUNK, IN_TILE), lambda b, n: (b * nc + n, 0)),
                   _stacked_spec(prev, depth, l, (None, RET_HEADS, hd, hd), lambda b, n: (b, 0, 0, 0))],
        out_shape=[jax.ShapeDtypeStruct((nseq * seq_len, IN_TILE), BF16),
                   jax.ShapeDtypeStruct((depth, nseq, RET_HEADS, hd, hd), F32)],
        scratch_shapes=[pltpu.VMEM((RET_HEADS, hd, hd), F32),
                        pltpu.VMEM((RET_HEADS, RET_CHUNK, RET_CHUNK), F32)],
        input_output_aliases=aliases,
        compiler_params=_params(2), name="ret_prompt",
    )(proj, proj, proj, proj, _vec3(gn_g), *cargs)


def _ret_sample_kernel(q_ref, k_ref, v_ref, gr_ref, gn_ref, s_ref, o_ref, so_ref, y_ref, *, hd, bs, seq_len):
    q = q_ref[...].astype(F32)
    k = k_ref[...].astype(F32)
    v = v_ref[...].astype(F32)
    gr = gr_ref[...].astype(F32)
    for h in range(RET_HEADS):
        cs = slice(h * hd, (h + 1) * hd)
        decays = _ret_decays(h, seq_len)
        for b in range(bs):
            rs = slice(b * seq_len, (b + 1) * seq_len)
            y, s_new = _ret_chunk(q[rs, cs].astype(BF16), k[rs, cs].astype(BF16), v[rs, cs].astype(BF16),
                                  gr[rs, cs], gn_ref[:, cs], s_ref[b, h], h, seq_len, decays)
            y_ref[rs, cs] = y
            so_ref[b, h] = s_new
    o_ref[...] = y_ref[...].astype(BF16)


def _ret_sample(proj, gn_g, state, l, seq_len, bs, prev):
    depth, nseq = state.shape[:2]
    hd = IN_TILE // RET_HEADS

    def col(jc):
        return pl.BlockSpec((None, bs * seq_len, IN_TILE), lambda i: (jc, i, 0))
    kern, cspecs, cargs, aliases = _stacked(
        functools.partial(_ret_sample_kernel, hd=hd, bs=bs, seq_len=seq_len), 6, prev, depth, l)
    return pl.pallas_call(
        kern, grid=(nseq // bs,),
        in_specs=[col(J_Q), col(J_K), col(J_VR), col(J_GR),
                  pl.BlockSpec((None, 1, IN_TILE), lambda i: (l, 0, 0)),
                  pl.BlockSpec((None, bs, RET_HEADS, hd, hd), lambda i: (l, i, 0, 0, 0))] + cspecs,
        out_specs=[pl.BlockSpec((bs * seq_len, IN_TILE), lambda i: (i, 0)),
                   _stacked_spec(prev, depth, l, (bs, RET_HEADS, hd, hd), lambda i: (i, 0, 0, 0))],
        out_shape=[jax.ShapeDtypeStruct((nseq * seq_len, IN_TILE), BF16),
                   jax.ShapeDtypeStruct(state.shape, F32)],
        scratch_shapes=[pltpu.VMEM((bs * seq_len, IN_TILE), F32)],
        input_output_aliases=aliases,
        compiler_params=_params(1), name="ret_sample",
    )(proj, proj, proj, proj, _vec3(gn_g), state, *cargs)


def _merge_kernel(ya_ref, yb_ref, ga_ref, gb_ref, wa_ref, wb_ref, o_ref, wab_ref, wbb_ref, *, rc):
    @pl.when(pl.program_id(1) == 0)
    def _():
        wab_ref[...] = wa_ref[...].astype(BF16)
        wbb_ref[...] = wb_ref[...].astype(BF16)
    for c in range(o_ref.shape[0] // rc):
        rows = slice(c * rc, (c + 1) * rc)
        a = jnp.dot(ya_ref[rows, :], wab_ref[...], preferred_element_type=F32)
        b = jnp.dot(yb_ref[rows, :], wbb_ref[...], preferred_element_type=F32)
        o_ref[rows, :] = (ga_ref[rows, :].astype(F32) * a + gb_ref[rows, :].astype(F32) * b).astype(BF16)


def _merge(ya, yb, proj, w_a, w_b, l, tm, rc):
    rows, k = ya.shape
    n = w_a.shape[2]
    tn = IN_TILE
    row = pl.BlockSpec((tm, k), lambda j, i: (i, 0))
    wsp = pl.BlockSpec((None, k, tn), lambda j, i: (l, 0, j))
    return pl.pallas_call(
        functools.partial(_merge_kernel, rc=rc), grid=(n // tn, rows // tm),
        in_specs=[row, row,
                  pl.BlockSpec((None, tm, tn), lambda j, i: (J_GA + j, i, 0)),
                  pl.BlockSpec((None, tm, tn), lambda j, i: (J_GB + j, i, 0)),
                  wsp, wsp],
        out_specs=pl.BlockSpec((None, tm, tn), lambda j, i: (j, i, 0)),
        out_shape=jax.ShapeDtypeStruct((n // tn, rows, tn), BF16),
        scratch_shapes=[pltpu.VMEM((k, tn), BF16), pltpu.VMEM((k, tn), BF16)],
        compiler_params=_params(2), name="merge",
    )(ya, yb, proj, proj, w_a, w_b)


def _ffn_up_kernel(h_ref, wg_ref, wu_ref, cw_ref, cb_ref, *rest, tm, rc, seq_len, has_state):
    if has_state:
        st_ref, y_ref, so_ref, wgb_ref, wub_ref, gbuf_ref = rest
    else:
        y_ref, so_ref, wgb_ref, wub_ref, gbuf_ref = rest
    i = pl.program_id(1)
    pad = SUBLANES
    keep = CONV_W - 1

    @pl.when(i == 0)
    def _():
        wgb_ref[...] = wg_ref[...].astype(BF16)
        wub_ref[...] = wu_ref[...].astype(BF16)

    tn = y_ref.shape[1]
    if has_state:
        gbuf_ref[0:pad, :] = jnp.zeros((pad, tn), F32)
    else:
        @pl.when(((i * tm) & (seq_len - 1)) == 0)
        def _():
            gbuf_ref[0:pad, :] = jnp.zeros((pad, tn), F32)
    for c in range(tm // rc):
        r0 = c * rc
        h = h_ref[r0:r0 + rc, :]
        g = jnp.dot(h, wgb_ref[...], preferred_element_type=F32)
        up = jnp.dot(h, wub_ref[...], preferred_element_type=F32)
        gbuf_ref[pad + r0:pad + r0 + rc, :] = g
        g1 = gbuf_ref[pad - 1 + r0:pad - 1 + r0 + rc, :]
        g2 = gbuf_ref[pad - 2 + r0:pad - 2 + r0 + rc, :]
        if has_state:
            bs = rc // seq_len
            st = st_ref[c * bs:(c + 1) * bs]
            s0 = jnp.broadcast_to(st[:, 0:1, :], (bs, seq_len, tn)).reshape(rc, tn)
            s1 = jnp.broadcast_to(st[:, 1:2, :], (bs, seq_len, tn)).reshape(rc, tn)
            t = lax.broadcasted_iota(jnp.int32, (rc, tn), 0) & (seq_len - 1)
            g1 = jnp.where(t == 0, s1, g1)
            g2 = jnp.where(t == 0, s0, jnp.where(t == 1, s1, g2))
            so_ref[c * bs:(c + 1) * bs] = g.reshape(bs, seq_len, tn)[:, seq_len - keep:, :]
        conv = g2 * cw_ref[0:1, :] + g1 * cw_ref[1:2, :] + g * cw_ref[2:3, :] + cb_ref[...]
        y_ref[r0:r0 + rc, :] = (jax.nn.gelu(conv) * up).astype(BF16)
    if not has_state:
        last = gbuf_ref[pad + tm - keep:pad + tm, :]
        gbuf_ref[pad - keep:pad, :] = last
        so_ref[...] = last


def _ffn_up(h, w_gate, w_up, conv_w, conv_b, l, depth, nseq, seq_len, tm, rc, tn, state, prev):
    rows, k = h.shape
    n = w_gate.shape[2]
    keep = CONV_W - 1
    has_state = state is not None
    assert seq_len & (seq_len - 1) == 0 and seq_len >= CONV_W
    kern = functools.partial(_ffn_up_kernel, tm=tm, rc=rc, seq_len=seq_len, has_state=has_state)
    wsp = pl.BlockSpec((None, k, tn), lambda j, i: (l, 0, j))
    in_specs = [pl.BlockSpec((tm, k), lambda j, i: (i, 0)), wsp, wsp,
                pl.BlockSpec((None, CONV_W, tn), lambda j, i: (l, 0, j)),
                pl.BlockSpec((None, 1, tn), lambda j, i: (l, 0, j))]
    args = [h, w_gate, w_up, conv_w, _vec3(conv_b)]
    if has_state:
        bs = tm // seq_len
        in_specs.append(pl.BlockSpec((None, bs, keep, tn), lambda j, i: (l, i, 0, j)))
        args.append(state)
        so_spec = _stacked_spec(prev, depth, l, (bs, keep, tn), lambda j, i: (i, 0, j))
    else:
        tps = seq_len // tm
        so_spec = _stacked_spec(prev, depth, l, (None, keep, tn), lambda j, i: (i // tps, 0, j))
    kern, cspecs, cargs, aliases = _stacked(kern, len(args), prev, depth, l)
    return pl.pallas_call(
        kern, grid=(n // tn, rows // tm),
        in_specs=in_specs + cspecs,
        out_specs=[pl.BlockSpec((None, tm, tn), lambda j, i: (j, i, 0)), so_spec],
        out_shape=[jax.ShapeDtypeStruct((n // tn, rows, tn), BF16),
                   jax.ShapeDtypeStruct((depth, nseq, keep, n), F32)],
        scratch_shapes=[pltpu.VMEM((k, tn), BF16), pltpu.VMEM((k, tn), BF16),
                        pltpu.VMEM((tm + SUBLANES, tn), F32)],
        input_output_aliases=aliases,
        compiler_params=_params(2), name="ffn_up",
    )(*args, *cargs)


def _layer(l, depth, x, h, mod, w, cfg, s_ret, conv_prev, outs):
    nseq, seq_len, tm, tm_post = cfg["nseq"], cfg["seq_len"], cfg["tm"], cfg["tm_post"]
    ret_out, conv_out, v_out = outs

    proj = _inproj(h, w["w_in"], w["sgu_ln_g"], w["sgu_ln_b"], l, cfg["tm_in"], cfg["rc_in"], seq_len,
                   cfg["pos0"])
    ya, v_out = _sgu(proj, cfg["sgu_w"], cfg["sgu_b"], l, depth, seq_len, tm, cfg["emit_v"], v_out)
    if s_ret is None:
        yb, ret_out = _ret_prompt(proj, w["ret_gn_g"], l, depth, nseq, seq_len, ret_out)
    else:
        yb, ret_out = _ret_sample(proj, w["ret_gn_g"], s_ret, l, seq_len, cfg["ret_bs"], ret_out)
    merged = _merge(ya, yb, proj, w["w_branch_a"], w["w_branch_b"], l, tm, cfg["rc_merge"])
    x, h2 = _matmul_post(merged, w["w_out"], x, w["norm_post1"], mod, l, K_G1, seq_len, cfg["tm_out"],
                         (w["norm_pre2"], l, K_SC2, K_SH2))

    y, conv_out = _ffn_up(h2, w["ffn_w_gate"], w["ffn_w_up"], w["ffn_conv_w"], w["ffn_conv_b"], l, depth,
                          nseq, seq_len, cfg["tm_ffn"], ROW_CHUNK, 512, conv_prev, conv_out)
    nxt = None if l == depth - 1 else (w["norm_pre1"], l + 1, K_SC1, K_SH1)
    x, h_next = _matmul_post(y, w["ffn_w_down_bf16"], x, w["norm_post2"], mod, l, K_G2, seq_len, tm_post, nxt)
    return x, h_next, (ret_out, conv_out, v_out)


def kernel(x_prompt, x_sample, state_ret, state_conv, c_prompt, c_sample, w_ada, b_ada, norm_pre1, norm_post1, norm_pre2, norm_post2, w_in, sgu_w_s, sgu_b_s, sgu_ln_g, sgu_ln_b, ret_gn_g, w_branch_a, w_branch_b, w_out, ffn_w_gate, ffn_w_up, ffn_conv_w, ffn_conv_b, ffn_w_down):
    w = dict(norm_pre1=norm_pre1, norm_post1=norm_post1, norm_pre2=norm_pre2, norm_post2=norm_post2, w_in=w_in,
             sgu_ln_g=sgu_ln_g, sgu_ln_b=sgu_ln_b, ret_gn_g=ret_gn_g, w_branch_a=w_branch_a,
             w_branch_b=w_branch_b, w_out=w_out, ffn_w_gate=ffn_w_gate, ffn_w_up=ffn_w_up,
             ffn_conv_w=ffn_conv_w, ffn_conv_b=ffn_conv_b, ffn_w_down_bf16=_cast_bf16(ffn_w_down))
    depth = w_ada.shape[0]
    nb, seq, d = x_prompt.shape
    ndb, dseq, _ = x_sample.shape

    mod_p, mod_s = _modulation(c_prompt, c_sample, w_ada, b_ada)

    sgu_wp, sgu_bp = _sgu_tables(sgu_w_s, sgu_b_s, seq)
    sgu_ws, sgu_bs = _sgu_tables(sgu_w_s, sgu_b_s, dseq)
    cfg_p = dict(nseq=nb, seq_len=seq, tm=1024, tm_in=2048, rc_in=128, rc_merge=256, tm_ffn=2048, tm_out=512,
                 tm_post=256, pos0=0, sgu_w=sgu_wp, sgu_b=sgu_bp, emit_v=False)
    cfg_s = dict(nseq=ndb, seq_len=dseq, tm=1024, tm_in=1024, rc_in=128, rc_merge=256, tm_ffn=1024, tm_out=512,
                 tm_post=256, pos0=PAST_LEN, ret_bs=8, sgu_w=sgu_ws, sgu_b=sgu_bs, emit_v=True)

    xp = x_prompt.reshape(nb * seq, d)
    xs = x_sample.reshape(ndb * dseq, d)
    hp = _prenorm(xp, norm_pre1, mod_p, 0, K_SC1, K_SH1, seq, 512)
    hs = _prenorm(xs, norm_pre1, mod_s, 0, K_SC1, K_SH1, dseq, 512)
    outs_p = outs_s = (None, None, None)
    for l in range(depth):
        xp, hp, outs_p = _layer(l, depth, xp, hp, mod_p, w, cfg_p, None, None, outs_p)
        xs, hs, outs_s = _layer(l, depth, xs, hs, mod_s, w, cfg_s, state_ret, state_conv, outs_s)
    ret_p, conv_p, _ = outs_p
    ret_s, conv_s, v_s = outs_s
    return (xp.reshape(nb, seq, d), xs.reshape(ndb, dseq, d), ret_p, ret_s, conv_p, conv_s,
            v_s.reshape(depth, ndb, dseq, -1))
```

```python
import functools
import math

import jax
import jax.numpy as jnp
from jax import lax
from jax.experimental import pallas as pl
from jax.experimental.pallas import tpu as pltpu

F32 = jnp.float32
BF16 = jnp.bfloat16

EPS = 1e-6
ROPE_BASE = 10000.0
PAST_LEN = 16384
SGU_GROUPS = 8
SGU_CHUNK = 128
RET_HEADS = 8
RET_CHUNK = 128
CONV_W = 3

SUBLANES = 8
BF16_ROWS = 16
ROW_CHUNK = 128
VMEM_LIMIT_BYTES = 56 * 1024 * 1024


def _params(n_axes):
    return pltpu.CompilerParams(dimension_semantics=("arbitrary",) * n_axes,
                                vmem_limit_bytes=VMEM_LIMIT_BYTES)


def _stacked(kernel, n_in, prev, depth, l):
    if prev is None:
        def first(*refs):
            refs = list(refs)
            so = refs[n_in + 1]
            for k in range(depth):
                if k != l:
                    so[k] = jnp.zeros(so.shape[1:], so.dtype)
            refs[n_in + 1] = so.at[l]
            kernel(*refs)
        return first, [], [], {}

    def later(*refs):
        kernel(*refs[:n_in], *refs[n_in + 1:])
    return later, [pl.BlockSpec(memory_space=pl.ANY)], [prev], {n_in: 1}


def _stacked_spec(prev, depth, l, block, index_fn):
    if prev is None:
        return pl.BlockSpec((depth,) + block, lambda *g: (0,) + index_fn(*g))
    return pl.BlockSpec((None,) + block, lambda *g: (l,) + index_fn(*g))


def _rms(x, g):
    return x * lax.rsqrt(jnp.mean(x * x, axis=-1, keepdims=True) + EPS) * g


def _silu(x):
    return x * jax.nn.sigmoid(x)


def _vec3(p):
    return p.reshape(p.shape[0], 1, p.shape[1])


def _mod_kernel(c_ref, w_ref, b_ref, op_ref, os_ref, *, n_p, s0):
    a = _silu(c_ref[...]).astype(BF16)
    m = jnp.dot(a, w_ref[...].astype(BF16), preferred_element_type=F32) + b_ref[...]
    op_ref[...] = m[0:n_p][:, None, :]
    os_ref[...] = m[s0:][:, None, :]


def _modulation(c_prompt, c_sample, w_ada, b_ada, tn=1024):
    depth, d, n = w_ada.shape
    n_p, n_s = c_prompt.shape[0], c_sample.shape[0]
    s0 = -(-n_p // BF16_ROWS) * BF16_ROWS
    c_all = jnp.concatenate([c_prompt, jnp.zeros((s0 - n_p, d), F32), c_sample], axis=0)
    rows = c_all.shape[0]
    assert rows % BF16_ROWS == 0
    return pl.pallas_call(
        functools.partial(_mod_kernel, n_p=n_p, s0=s0),
        grid=(depth, n // tn),
        in_specs=[pl.BlockSpec((rows, d), lambda l, j: (0, 0)),
                  pl.BlockSpec((None, d, tn), lambda l, j: (l, 0, j)),
                  pl.BlockSpec((None, 1, tn), lambda l, j: (l, 0, j))],
        out_specs=[pl.BlockSpec((None, n_p, 1, tn), lambda l, j: (l, 0, 0, j)),
                   pl.BlockSpec((None, n_s, 1, tn), lambda l, j: (l, 0, 0, j))],
        out_shape=[jax.ShapeDtypeStruct((depth, n_p, 1, n), F32),
                   jax.ShapeDtypeStruct((depth, n_s, 1, n), F32)],
        compiler_params=_params(2),
        name="adaln_mod",
    )(c_all, w_ada, b_ada.reshape(depth, 1, n))


K_SH1, K_SC1, K_G1, K_SH2, K_SC2, K_G2 = range(6)


def _mod_spec(l, k, tm, seq_len, d, grid_rank):
    if seq_len >= tm:
        tps = seq_len // tm
        return pl.BlockSpec((None, 1, 1, d), lambda *g: (l, g[grid_rank - 1] // tps, 0, k))
    return pl.BlockSpec((None, tm // seq_len, 1, d), lambda *g: (l, g[grid_rank - 1], 0, k))


def _rows_of(m_ref, r0, nrows, seq_len):
    d = m_ref.shape[-1]
    if m_ref.shape[0] == 1:
        return m_ref[0]
    nb = nrows // seq_len
    m = m_ref[r0 // seq_len:r0 // seq_len + nb]
    return jnp.broadcast_to(m, (nb, seq_len, d)).reshape(nrows, d)


def _prenorm_kernel(x_ref, g_ref, sc_ref, sh_ref, h_ref, *, seq_len):
    tm = x_ref.shape[0]
    sc = _rows_of(sc_ref, 0, tm, seq_len)
    sh = _rows_of(sh_ref, 0, tm, seq_len)
    h_ref[...] = (_rms(x_ref[...], g_ref[...]) * (1.0 + sc) + sh).astype(BF16)


def _prenorm(x, g_pre, mod, l, k_sc, k_sh, seq_len, tm):
    rows, d = x.shape
    act = pl.BlockSpec((tm, d), lambda i: (i, 0))
    return pl.pallas_call(
        functools.partial(_prenorm_kernel, seq_len=seq_len), grid=(rows // tm,),
        in_specs=[act, pl.BlockSpec((None, 1, d), lambda i: (l, 0, 0)),
                  _mod_spec(l, k_sc, tm, seq_len, d, 1), _mod_spec(l, k_sh, tm, seq_len, d, 1)],
        out_specs=act,
        out_shape=jax.ShapeDtypeStruct((rows, d), BF16),
        compiler_params=_params(1), name="prenorm",
    )(x, _vec3(g_pre), mod, mod)


def _cast_kernel(w_ref, o_ref):
    o_ref[...] = w_ref[...].astype(BF16)


def _cast_bf16(w, tk=512):
    depth, k, n = w.shape
    spec = pl.BlockSpec((None, tk, n), lambda l, i: (l, i, 0))
    return pl.pallas_call(
        _cast_kernel, grid=(depth, k // tk), in_specs=[spec], out_specs=spec,
        out_shape=jax.ShapeDtypeStruct(w.shape, BF16),
        compiler_params=_params(2), name="cast_bf16",
    )(w)


def _mm_post_kernel(m_ref, w_ref, x_ref, gpost_ref, gate_ref, *rest, tm, rc, seq_len, has_next, cast):
    rest = list(rest)
    if has_next:
        gpre_ref, sc_ref, sh_ref = rest[:3]
        rest = rest[3:]
    xo_ref = rest.pop(0)
    h_ref = rest.pop(0) if has_next else None
    if cast:
        wb_ref = rest.pop(0)

        @pl.when(pl.program_id(0) == 0)
        def _():
            wb_ref[...] = w_ref[...].astype(BF16)
    else:
        wb_ref = w_ref

    nk, _, tk = m_ref.shape
    for c in range(tm // rc):
        r0 = c * rc
        rows = slice(r0, r0 + rc)
        t = jnp.dot(m_ref[0, rows, :], wb_ref[0:tk, :], preferred_element_type=F32)
        for kk in range(1, nk):
            t += jnp.dot(m_ref[kk, rows, :], wb_ref[kk * tk:(kk + 1) * tk, :], preferred_element_type=F32)
        x = x_ref[rows, :] + _rows_of(gate_ref, r0, rc, seq_len) * _rms(t, gpost_ref[...])
        xo_ref[rows, :] = x
        if has_next:
            h = (_rms(x, gpre_ref[...]) * (1.0 + _rows_of(sc_ref, r0, rc, seq_len))
                 + _rows_of(sh_ref, r0, rc, seq_len))
            h_ref[rows, :] = h.astype(BF16)


def _matmul_post(m, w, x, g_post, mod, l, k_gate, seq_len, tm, nxt, rc=ROW_CHUNK):
    nk, rows, tk = m.shape
    k = nk * tk
    d = w.shape[2]
    assert k == w.shape[1]
    cast = w.dtype != BF16
    has_next = nxt is not None
    act = pl.BlockSpec((tm, d), lambda i: (i, 0))

    def vec():
        return pl.BlockSpec((None, 1, d), lambda i: (l, 0, 0))
    in_specs = [pl.BlockSpec((nk, tm, tk), lambda i: (0, i, 0)),
                pl.BlockSpec((None, k, d), lambda i: (l, 0, 0), pipeline_mode=pl.Buffered(1)),
                act, vec(), _mod_spec(l, k_gate, tm, seq_len, d, 1)]
    args = [m, w, x, _vec3(g_post), mod]
    out_specs, out_shape = [act], [jax.ShapeDtypeStruct((rows, d), F32)]
    if has_next:
        g_pre, ln, k_sc, k_sh = nxt
        in_specs += [pl.BlockSpec((None, 1, d), lambda i: (ln, 0, 0)),
                     _mod_spec(ln, k_sc, tm, seq_len, d, 1), _mod_spec(ln, k_sh, tm, seq_len, d, 1)]
        args += [_vec3(g_pre), mod, mod]
        out_specs.append(act)
        out_shape.append(jax.ShapeDtypeStruct((rows, d), BF16))
    kern = functools.partial(_mm_post_kernel, tm=tm, rc=rc, seq_len=seq_len, has_next=has_next, cast=cast)
    out = pl.pallas_call(
        kern, grid=(rows // tm,), in_specs=in_specs, out_specs=out_specs, out_shape=out_shape,
        scratch_shapes=[pltpu.VMEM((k, d), BF16)] if cast else [],
        compiler_params=_params(1), name="matmul_post",
    )(*args)
    return (out[0], out[1]) if has_next else (out[0], None)


IN_TILE = 1024
J_U, J_V, J_Q, J_K, J_VR, J_GR, J_GA, J_GB = 0, 1, 2, 3, 4, 5, 6, 8
IN_GROUP_ROWS = 512


def _inproj_kernel(h_ref, w_ref, lng_ref, lnb_ref, o_ref, wb_ref, *, tm, rc, unroll, nsplit, seq_len, pos0,
                   head_dim):
    j = pl.program_id(0)
    i = pl.program_id(1)

    @pl.when(i == 0)
    def _():
        wb_ref[...] = w_ref[...].astype(BF16)

    def run(epilogue):
        def group(gi, carry):
            for u in range(unroll):
                r0 = pl.multiple_of((gi * unroll + u) * rc, rc)
                hs = h_ref[pl.ds(r0, rc), :]
                wn = IN_TILE // nsplit
                acc = jnp.concatenate([jnp.dot(hs, wb_ref[:, s * wn:(s + 1) * wn], preferred_element_type=F32)
                                       for s in range(nsplit)], axis=1)
                o_ref[pl.ds(r0, rc), :] = epilogue(acc, r0).astype(BF16)
            return carry
        lax.fori_loop(0, tm // (rc * unroll), group, 0)

    def layer_norm_gelu(acc, r0):
        v = jax.nn.gelu(acc)
        mu = jnp.mean(v, axis=-1, keepdims=True)
        var = jnp.mean(jnp.square(v - mu), axis=-1, keepdims=True)
        return (v - mu) * lax.rsqrt(var + EPS) * lng_ref[...] + lnb_ref[...]

    def rope(acc, r0):
        half = head_dim // 2
        lane = lax.broadcasted_iota(jnp.int32, (rc, head_dim), 1)
        row = lax.broadcasted_iota(jnp.int32, (rc, head_dim), 0)
        pos = pos0 + ((i * tm + r0 + row) & (seq_len - 1))
        inv = jnp.exp((lane & (half - 1)).astype(F32) * (-math.log(ROPE_BASE) / half))
        ang = pos.astype(F32) * inv
        scale = jnp.where(j == J_K, head_dim ** -0.5, 1.0).astype(F32)
        cos = jnp.cos(ang) * scale
        sin = jnp.where(lane < half, -jnp.sin(ang), jnp.sin(ang)) * scale
        heads = [acc[:, h * head_dim:(h + 1) * head_dim] for h in range(IN_TILE // head_dim)]
        return jnp.concatenate([xh * cos + pltpu.roll(xh, half, axis=1) * sin for xh in heads], axis=1)

    pl.when(j == J_U)(lambda: run(lambda acc, c: jax.nn.gelu(acc)))
    pl.when(j == J_V)(lambda: run(layer_norm_gelu))
    pl.when((j == J_Q) | (j == J_K))(lambda: run(rope))
    pl.when(j == J_VR)(lambda: run(lambda acc, c: acc))
    pl.when(j == J_GR)(lambda: run(lambda acc, c: _silu(acc)))
    pl.when(j >= J_GA)(lambda: run(lambda acc, c: jax.nn.sigmoid(acc)))


def _inproj(h, w_in, ln_g, ln_b, l, tm, rc, nsplit, seq_len, pos0):
    rows, k = h.shape
    n = w_in.shape[2]
    assert seq_len & (seq_len - 1) == 0 and n == 10 * IN_TILE
    unroll = min(IN_GROUP_ROWS // rc, tm // rc)
    kern = functools.partial(_inproj_kernel, tm=tm, rc=rc, unroll=unroll, nsplit=nsplit, seq_len=seq_len,
                             pos0=pos0, head_dim=IN_TILE // RET_HEADS)
    vec = pl.BlockSpec((None, 1, IN_TILE), lambda j, i: (l, 0, 0))
    return pl.pallas_call(
        kern, grid=(n // IN_TILE, rows // tm),
        in_specs=[pl.BlockSpec((tm, k), lambda j, i: (i, 0)),
                  pl.BlockSpec((None, k, IN_TILE), lambda j, i: (l, 0, j)),
                  vec, vec],
        out_specs=pl.BlockSpec((None, tm, IN_TILE), lambda j, i: (j, i, 0)),
        out_shape=jax.ShapeDtypeStruct((n // IN_TILE, rows, IN_TILE), BF16),
        scratch_shapes=[pltpu.VMEM((k, IN_TILE), BF16)],
        compiler_params=_params(2), name="inproj",
    )(h, w_in, _vec3(ln_g), _vec3(ln_b))


def _sgu_kernel(u_ref, v_ref, w_ref, b_ref, o_ref, *vo_ref, tm, chunk):
    if vo_ref:
        vo_ref[0][...] = v_ref[...].astype(F32)
    r = lax.broadcasted_iota(jnp.int32, (SGU_CHUNK, SGU_CHUNK), 0)
    c = lax.broadcasted_iota(jnp.int32, (SGU_CHUNK, SGU_CHUNK), 1)
    mask = (c <= r) & ((r & -chunk) == (c & -chunk))
    gd = IN_TILE // SGU_GROUPS
    for g in range(SGU_GROUPS):
        w = jnp.where(mask, w_ref[g], 0.0).astype(BF16)
        bias = b_ref[:, g:g + 1]
        for t in range(tm // SGU_CHUNK):
            rs = slice(t * SGU_CHUNK, (t + 1) * SGU_CHUNK)
            cs = slice(g * gd, (g + 1) * gd)
            z = jnp.dot(w, v_ref[rs, cs], preferred_element_type=F32) + bias
            o_ref[rs, cs] = (u_ref[rs, cs].astype(F32) * z).astype(BF16)


def _sgu_tables(w_s, b_s, seq_len):
    chunk = min(SGU_CHUNK, seq_len)
    reps = SGU_CHUNK // chunk
    w_t = jnp.tile(w_s[:, :, :chunk, :chunk], (1, 1, reps, reps))
    b_t = jnp.tile(jnp.swapaxes(b_s[:, :, :chunk], 1, 2), (1, reps, 1))
    return w_t, b_t


def _sgu(proj, w_t, b_t, l, depth, seq_len, tm, emit_v, prev):
    rows = proj.shape[1]
    kern = functools.partial(_sgu_kernel, tm=tm, chunk=min(SGU_CHUNK, seq_len))
    in_specs = [pl.BlockSpec((None, tm, IN_TILE), lambda i: (J_U, i, 0)),
                pl.BlockSpec((None, tm, IN_TILE), lambda i: (J_V, i, 0)),
                pl.BlockSpec((None, SGU_GROUPS, SGU_CHUNK, SGU_CHUNK), lambda i: (l, 0, 0, 0)),
                pl.BlockSpec((None, SGU_CHUNK, SGU_GROUPS), lambda i: (l, 0, 0))]
    out_specs = [pl.BlockSpec((tm, IN_TILE), lambda i: (i, 0))]
    out_shape = [jax.ShapeDtypeStruct((rows, IN_TILE), BF16)]
    cspecs, cargs, aliases = [], [], {}
    if emit_v:
        kern, cspecs, cargs, aliases = _stacked(kern, 4, prev, depth, l)
        out_specs.append(_stacked_spec(prev, depth, l, (tm, IN_TILE), lambda i: (i, 0)))
        out_shape.append(jax.ShapeDtypeStruct((depth, rows, IN_TILE), F32))
    out = pl.pallas_call(
        kern, grid=(rows // tm,),
        in_specs=in_specs + cspecs, out_specs=out_specs, out_shape=out_shape,
        input_output_aliases=aliases,
        compiler_params=_params(1), name="sgu",
    )(proj, proj, w_t, b_t, *cargs)
    return (out[0], out[1]) if emit_v else (out[0], None)


def _log_gamma(h):
    return math.log(1.0 - 2.0 ** (-5.0 - h))


def _ret_decays(h, c):
    lg = _log_gamma(h)
    ri = lax.broadcasted_iota(jnp.int32, (c, c), 0)
    ci = lax.broadcasted_iota(jnp.int32, (c, c), 1)
    diff = (ri - ci).astype(F32)
    decay = jnp.where(ri >= ci, jnp.exp(lg * jnp.maximum(diff, 0.0)), 0.0)
    idx = lax.broadcasted_iota(jnp.int32, (c, 1), 0).astype(F32)
    return decay, jnp.exp(lg * (idx + 1.0)), jnp.exp(lg * (c - 1.0 - idx))


def _ret_chunk(qh, kh, vh, grh, gnh, s, h, c, decays, fuse_cross=False):
    decay, q_dec, k_dec = decays
    scores = lax.dot_general(qh, kh, (((1,), (1,)), ((), ())), preferred_element_type=F32) * decay
    kd = (kh.astype(F32) * k_dec).astype(BF16)
    s_new = (s * math.exp(_log_gamma(h) * c)
             + lax.dot_general(kd, vh, (((0,), (0,)), ((), ())), preferred_element_type=F32))
    if fuse_cross:
        lhs = jnp.concatenate([scores.astype(BF16), (qh.astype(F32) * q_dec).astype(BF16)], axis=1)
        rhs = jnp.concatenate([vh, s.astype(BF16)], axis=0)
        o = jnp.dot(lhs, rhs, preferred_element_type=F32)
    else:
        inner = jnp.dot(scores.astype(BF16), vh, preferred_element_type=F32)
        cross = jnp.dot(qh, s.astype(BF16), preferred_element_type=F32) * q_dec
        o = inner + cross
    mu = jnp.mean(o, axis=-1, keepdims=True)
    var = jnp.mean(jnp.square(o - mu), axis=-1, keepdims=True)
    y = (o - mu) * lax.rsqrt(var + EPS) * gnh
    return grh * y, s_new


def _ret_prompt_kernel(q_ref, k_ref, v_ref, gr_ref, gn_ref, o_ref, so_ref, s_ref, dec_ref, *, hd, fuse_cross):
    n = pl.program_id(1)

    @pl.when(n == 0)
    def _():
        s_ref[...] = jnp.zeros_like(s_ref)

    @pl.when((n == 0) & (pl.program_id(0) == 0))
    def _():
        for h in range(RET_HEADS):
            dec_ref[h] = _ret_decays(h, RET_CHUNK)[0]

    for h in range(RET_HEADS):
        cs = slice(h * hd, (h + 1) * hd)
        _, q_dec, k_dec = _ret_decays(h, RET_CHUNK)
        y, s_new = _ret_chunk(q_ref[:, cs], k_ref[:, cs], v_ref[:, cs], gr_ref[:, cs].astype(F32),
                              gn_ref[:, cs], s_ref[h], h, RET_CHUNK, (dec_ref[h], q_dec, k_dec), fuse_cross)
        o_ref[:, cs] = y.astype(BF16)
        s_ref[h] = s_new

    @pl.when(n == pl.num_programs(1) - 1)
    def _():
        so_ref[...] = s_ref[...]


def _ret_prompt(proj, gn_g, l, depth, nseq, seq_len, prev):
    hd = IN_TILE // RET_HEADS
    nc = seq_len // RET_CHUNK

    def col(jc):
        return pl.BlockSpec((None, RET_CHUNK, IN_TILE), lambda b, n: (jc, b * nc + n, 0))
    kern, cspecs, cargs, aliases = _stacked(
        functools.partial(_ret_prompt_kernel, hd=hd, fuse_cross=(l % 2 == 1)), 5, prev, depth, l)
    return pl.pallas_call(
        kern, grid=(nseq, nc),
        in_specs=[col(J_Q), col(J_K), col(J_VR), col(J_GR),
                  pl.BlockSpec((None, 1, IN_TILE), lambda b, n: (l, 0, 0))] + cspecs,
        out_specs=[pl.BlockSpec((RET_CHUNK, IN_TILE), lambda b, n: (b * nc + n, 0)),
                   _stacked_spec(prev, depth, l, (None, RET_HEADS, hd, hd), lambda b, n: (b, 0, 0, 0))],
        out_shape=[jax.ShapeDtypeStruct((nseq * seq_len, IN_TILE), BF16),
                   jax.ShapeDtypeStruct((depth, nseq, RET_HEADS, hd, hd), F32)],
        scratch_shapes=[pltpu.VMEM((RET_HEADS, hd, hd), F32),
                        pltpu.VMEM((RET_HEADS, RET_CHUNK, RET_CHUNK), F32)],
        input_output_aliases=aliases,
        compiler_params=_params(2), name="ret_prompt",
    )(proj, proj, proj, proj, _vec3(gn_g), *cargs)


def _ret_sample_kernel(q_ref, k_ref, v_ref, gr_ref, gn_ref, s_ref, o_ref, so_ref, y_ref, *, hd, bs, seq_len):
    q = q_ref[...].astype(F32)
    k = k_ref[...].astype(F32)
    v = v_ref[...].astype(F32)
    gr = gr_ref[...].astype(F32)
    for h in range(RET_HEADS):
        cs = slice(h * hd, (h + 1) * hd)
        decays = _ret_decays(h, seq_len)
        for b in range(bs):
            rs = slice(b * seq_len, (b + 1) * seq_len)
            y, s_new = _ret_chunk(q[rs, cs].astype(BF16), k[rs, cs].astype(BF16), v[rs, cs].astype(BF16),
                                  gr[rs, cs], gn_ref[:, cs], s_ref[b, h], h, seq_len, decays)
            y_ref[rs, cs] = y
            so_ref[b, h] = s_new
    o_ref[...] = y_ref[...].astype(BF16)


def _ret_sample(proj, gn_g, state, l, seq_len, bs, prev):
    depth, nseq = state.shape[:2]
    hd = IN_TILE // RET_HEADS

    def col(jc):
        return pl.BlockSpec((None, bs * seq_len, IN_TILE), lambda i: (jc, i, 0))
    kern, cspecs, cargs, aliases = _stacked(
        functools.partial(_ret_sample_kernel, hd=hd, bs=bs, seq_len=seq_len), 6, prev, depth, l)
    return pl.pallas_call(
        kern, grid=(nseq // bs,),
        in_specs=[col(J_Q), col(J_K), col(J_VR), col(J_GR),
                  pl.BlockSpec((None, 1, IN_TILE), lambda i: (l, 0, 0)),
                  pl.BlockSpec((None, bs, RET_HEADS, hd, hd), lambda i: (l, i, 0, 0, 0))] + cspecs,
        out_specs=[pl.BlockSpec((bs * seq_len, IN_TILE), lambda i: (i, 0)),
                   _stacked_spec(prev, depth, l, (bs, RET_HEADS, hd, hd), lambda i: (i, 0, 0, 0))],
        out_shape=[jax.ShapeDtypeStruct((nseq * seq_len, IN_TILE), BF16),
                   jax.ShapeDtypeStruct(state.shape, F32)],
        scratch_shapes=[pltpu.VMEM((bs * seq_len, IN_TILE), F32)],
        input_output_aliases=aliases,
        compiler_params=_params(1), name="ret_sample",
    )(proj, proj, proj, proj, _vec3(gn_g), state, *cargs)


def _merge_kernel(ya_ref, yb_ref, ga_ref, gb_ref, wa_ref, wb_ref, o_ref, wab_ref, wbb_ref, *, rc):
    @pl.when(pl.program_id(1) == 0)
    def _():
        wab_ref[...] = wa_ref[...].astype(BF16)
        wbb_ref[...] = wb_ref[...].astype(BF16)
    for c in range(o_ref.shape[0] // rc):
        rows = slice(c * rc, (c + 1) * rc)
        a = jnp.dot(ya_ref[rows, :], wab_ref[...], preferred_element_type=F32)
        b = jnp.dot(yb_ref[rows, :], wbb_ref[...], preferred_element_type=F32)
        o_ref[rows, :] = (ga_ref[rows, :].astype(F32) * a + gb_ref[rows, :].astype(F32) * b).astype(BF16)


def _merge(ya, yb, proj, w_a, w_b, l, tm, rc):
    rows, k = ya.shape
    n = w_a.shape[2]
    tn = IN_TILE
    row = pl.BlockSpec((tm, k), lambda j, i: (i, 0))
    wsp = pl.BlockSpec((None, k, tn), lambda j, i: (l, 0, j))
    return pl.pallas_call(
        functools.partial(_merge_kernel, rc=rc), grid=(n // tn, rows // tm),
        in_specs=[row, row,
                  pl.BlockSpec((None, tm, tn), lambda j, i: (J_GA + j, i, 0)),
                  pl.BlockSpec((None, tm, tn), lambda j, i: (J_GB + j, i, 0)),
                  wsp, wsp],
        out_specs=pl.BlockSpec((None, tm, tn), lambda j, i: (j, i, 0)),
        out_shape=jax.ShapeDtypeStruct((n // tn, rows, tn), BF16),
        scratch_shapes=[pltpu.VMEM((k, tn), BF16), pltpu.VMEM((k, tn), BF16)],
        compiler_params=_params(2), name="merge",
    )(ya, yb, proj, proj, w_a, w_b)


def _ffn_up_kernel(h_ref, wg_ref, wu_ref, cw_ref, cb_ref, *rest, tm, rc, seq_len, has_state):
    if has_state:
        st_ref, y_ref, so_ref, wgb_ref, wub_ref, gbuf_ref = rest
    else:
        y_ref, so_ref, wgb_ref, wub_ref, gbuf_ref = rest
    i = pl.program_id(1)
    pad = SUBLANES
    keep = CONV_W - 1

    @pl.when(i == 0)
    def _():
        wgb_ref[...] = wg_ref[...].astype(BF16)
        wub_ref[...] = wu_ref[...].astype(BF16)

    tn = y_ref.shape[1]
    if has_state:
        gbuf_ref[0:pad, :] = jnp.zeros((pad, tn), F32)
    else:
        @pl.when(((i * tm) & (seq_len - 1)) == 0)
        def _():
            gbuf_ref[0:pad, :] = jnp.zeros((pad, tn), F32)
    for c in range(tm // rc):
        r0 = c * rc
        h = h_ref[r0:r0 + rc, :]
        g = jnp.dot(h, wgb_ref[...], preferred_element_type=F32)
        up = jnp.dot(h, wub_ref[...], preferred_element_type=F32)
        gbuf_ref[pad + r0:pad + r0 + rc, :] = g
        g1 = gbuf_ref[pad - 1 + r0:pad - 1 + r0 + rc, :]
        g2 = gbuf_ref[pad - 2 + r0:pad - 2 + r0 + rc, :]
        if has_state:
            bs = rc // seq_len
            st = st_ref[c * bs:(c + 1) * bs]
            s0 = jnp.broadcast_to(st[:, 0:1, :], (bs, seq_len, tn)).reshape(rc, tn)
            s1 = jnp.broadcast_to(st[:, 1:2, :], (bs, seq_len, tn)).reshape(rc, tn)
            t = lax.broadcasted_iota(jnp.int32, (rc, tn), 0) & (seq_len - 1)
            g1 = jnp.where(t == 0, s1, g1)
            g2 = jnp.where(t == 0, s0, jnp.where(t == 1, s1, g2))
            so_ref[c * bs:(c + 1) * bs] = g.reshape(bs, seq_len, tn)[:, seq_len - keep:, :]
        conv = g2 * cw_ref[0:1, :] + g1 * cw_ref[1:2, :] + g * cw_ref[2:3, :] + cb_ref[...]
        y_ref[r0:r0 + rc, :] = (jax.nn.gelu(conv) * up).astype(BF16)
    if not has_state:
        last = gbuf_ref[pad + tm - keep:pad + tm, :]
        gbuf_ref[pad - keep:pad, :] = last
        so_ref[...] = last


def _ffn_up(h, w_gate, w_up, conv_w, conv_b, l, depth, nseq, seq_len, tm, rc, tn, state, prev):
    rows, k = h.shape
    n = w_gate.shape[2]
    keep = CONV_W - 1
    has_state = state is not None
    assert seq_len & (seq_len - 1) == 0 and seq_len >= CONV_W
    kern = functools.partial(_ffn_up_kernel, tm=tm, rc=rc, seq_len=seq_len, has_state=has_state)
    wsp = pl.BlockSpec((None, k, tn), lambda j, i: (l, 0, j))
    in_specs = [pl.BlockSpec((tm, k), lambda j, i: (i, 0)), wsp, wsp,
                pl.BlockSpec((None, CONV_W, tn), lambda j, i: (l, 0, j)),
                pl.BlockSpec((None, 1, tn), lambda j, i: (l, 0, j))]
    args = [h, w_gate, w_up, conv_w, _vec3(conv_b)]
    if has_state:
        bs = tm // seq_len
        in_specs.append(pl.BlockSpec((None, bs, keep, tn), lambda j, i: (l, i, 0, j)))
        args.append(state)
        so_spec = _stacked_spec(prev, depth, l, (bs, keep, tn), lambda j, i: (i, 0, j))
    else:
        tps = seq_len // tm
        so_spec = _stacked_spec(prev, depth, l, (None, keep, tn), lambda j, i: (i // tps, 0, j))
    kern, cspecs, cargs, aliases = _stacked(kern, len(args), prev, depth, l)
    return pl.pallas_call(
        kern, grid=(n // tn, rows // tm),
        in_specs=in_specs + cspecs,
        out_specs=[pl.BlockSpec((None, tm, tn), lambda j, i: (j, i, 0)), so_spec],
        out_shape=[jax.ShapeDtypeStruct((n // tn, rows, tn), BF16),
                   jax.ShapeDtypeStruct((depth, nseq, keep, n), F32)],
        scratch_shapes=[pltpu.VMEM((k, tn), BF16), pltpu.VMEM((k, tn), BF16),
                        pltpu.VMEM((tm + SUBLANES, tn), F32)],
        input_output_aliases=aliases,
        compiler_params=_params(2), name="ffn_up",
    )(*args, *cargs)


def _layer(l, depth, x, h, mod, w, cfg, s_ret, conv_prev, outs):
    nseq, seq_len, tm, tm_post = cfg["nseq"], cfg["seq_len"], cfg["tm"], cfg["tm_post"]
    ret_out, conv_out, v_out = outs

    proj = _inproj(h, w["w_in"], w["sgu_ln_g"], w["sgu_ln_b"], l, cfg["tm_in"], cfg["rc_in"][l],
                   cfg["nsplit_in"][l], seq_len, cfg["pos0"])
    ya, v_out = _sgu(proj, cfg["sgu_w"], cfg["sgu_b"], l, depth, seq_len, tm, cfg["emit_v"], v_out)
    if s_ret is None:
        yb, ret_out = _ret_prompt(proj, w["ret_gn_g"], l, depth, nseq, seq_len, ret_out)
    else:
        yb, ret_out = _ret_sample(proj, w["ret_gn_g"], s_ret, l, seq_len, cfg["ret_bs"], ret_out)
    merged = _merge(ya, yb, proj, w["w_branch_a"], w["w_branch_b"], l, tm, cfg["rc_merge"])
    x, h2 = _matmul_post(merged, w["w_out"], x, w["norm_post1"], mod, l, K_G1, seq_len, cfg["tm_out"],
                         (w["norm_pre2"], l, K_SC2, K_SH2), rc=cfg["rc_out"][l])

    y, conv_out = _ffn_up(h2, w["ffn_w_gate"], w["ffn_w_up"], w["ffn_conv_w"], w["ffn_conv_b"], l, depth,
                          nseq, seq_len, cfg["tm_ffn"], ROW_CHUNK, 512, conv_prev, conv_out)
    nxt = None if l == depth - 1 else (w["norm_pre1"], l + 1, K_SC1, K_SH1)
    x, h_next = _matmul_post(y, w["ffn_w_down_bf16"], x, w["norm_post2"], mod, l, K_G2, seq_len, tm_post, nxt)
    return x, h_next, (ret_out, conv_out, v_out)


def kernel(x_prompt, x_sample, state_ret, state_conv, c_prompt, c_sample, w_ada, b_ada, norm_pre1, norm_post1, norm_pre2, norm_post2, w_in, sgu_w_s, sgu_b_s, sgu_ln_g, sgu_ln_b, ret_gn_g, w_branch_a, w_branch_b, w_out, ffn_w_gate, ffn_w_up, ffn_conv_w, ffn_conv_b, ffn_w_down):
    w = dict(norm_pre1=norm_pre1, norm_post1=norm_post1, norm_pre2=norm_pre2, norm_post2=norm_post2, w_in=w_in,
             sgu_ln_g=sgu_ln_g, sgu_ln_b=sgu_ln_b, ret_gn_g=ret_gn_g, w_branch_a=w_branch_a,
             w_branch_b=w_branch_b, w_out=w_out, ffn_w_gate=ffn_w_gate, ffn_w_up=ffn_w_up,
             ffn_conv_w=ffn_conv_w, ffn_conv_b=ffn_conv_b, ffn_w_down_bf16=_cast_bf16(ffn_w_down))
    depth = w_ada.shape[0]
    nb, seq, d = x_prompt.shape
    ndb, dseq, _ = x_sample.shape

    mod_p, mod_s = _modulation(c_prompt, c_sample, w_ada, b_ada)

    sgu_wp, sgu_bp = _sgu_tables(sgu_w_s, sgu_b_s, seq)
    sgu_ws, sgu_bs = _sgu_tables(sgu_w_s, sgu_b_s, dseq)
    cfg_p = dict(nseq=nb, seq_len=seq, tm=1024, tm_in=2048, rc_in=(256, 128), nsplit_in=(1, 2), rc_merge=256,
                 tm_ffn=2048, tm_out=512, rc_out=(128, 256), tm_post=256, pos0=0, sgu_w=sgu_wp, sgu_b=sgu_bp,
                 emit_v=False)
    cfg_s = dict(nseq=ndb, seq_len=dseq, tm=1024, tm_in=1024, rc_in=(128, 128), nsplit_in=(1, 1), rc_merge=256,
                 tm_ffn=1024, tm_out=512, rc_out=(128, 128), tm_post=256, pos0=PAST_LEN, ret_bs=8, sgu_w=sgu_ws,
                 sgu_b=sgu_bs,
                 emit_v=True)

    xp = x_prompt.reshape(nb * seq, d)
    xs = x_sample.reshape(ndb * dseq, d)
    hp = _prenorm(xp, norm_pre1, mod_p, 0, K_SC1, K_SH1, seq, 512)
    hs = _prenorm(xs, norm_pre1, mod_s, 0, K_SC1, K_SH1, dseq, 512)
    outs_p = outs_s = (None, None, None)
    for l in range(depth):
        xp, hp, outs_p = _layer(l, depth, xp, hp, mod_p, w, cfg_p, None, None, outs_p)
        xs, hs, outs_s = _layer(l, depth, xs, hs, mod_s, w, cfg_s, state_ret, state_conv, outs_s)
    ret_p, conv_p, _ = outs_p
    ret_s, conv_s, v_s = outs_s
    return (xp.reshape(nb, seq, d), xs.reshape(ndb, dseq, d), ret_p, ret_s, conv_p, conv_s,
            v_s.reshape(depth, ndb, dseq, -1))
```

```python
import functools
import math

import jax
import jax.numpy as jnp
from jax import lax
from jax.experimental import pallas as pl
from jax.experimental.pallas import tpu as pltpu

F32 = jnp.float32
BF16 = jnp.bfloat16

EPS = 1e-6
ROPE_BASE = 10000.0
PAST_LEN = 16384
SGU_GROUPS = 8
SGU_CHUNK = 128
RET_HEADS = 8
RET_CHUNK = 128
CONV_W = 3

SUBLANES = 8
BF16_ROWS = 16
ROW_CHUNK = 128
VMEM_LIMIT_BYTES = 56 * 1024 * 1024


def _params(n_axes):
    return pltpu.CompilerParams(dimension_semantics=("arbitrary",) * n_axes,
                                vmem_limit_bytes=VMEM_LIMIT_BYTES)


def _stacked(kernel, n_in, prev, depth, l):
    if prev is None:
        def first(*refs):
            refs = list(refs)
            so = refs[n_in + 1]
            for k in range(depth):
                if k != l:
                    so[k] = jnp.zeros(so.shape[1:], so.dtype)
            refs[n_in + 1] = so.at[l]
            kernel(*refs)
        return first, [], [], {}

    def later(*refs):
        kernel(*refs[:n_in], *refs[n_in + 1:])
    return later, [pl.BlockSpec(memory_space=pl.ANY)], [prev], {n_in: 1}


def _stacked_spec(prev, depth, l, block, index_fn):
    if prev is None:
        return pl.BlockSpec((depth,) + block, lambda *g: (0,) + index_fn(*g))
    return pl.BlockSpec((None,) + block, lambda *g: (l,) + index_fn(*g))


def _rms(x, g):
    return x * lax.rsqrt(jnp.mean(x * x, axis=-1, keepdims=True) + EPS) * g


def _silu(x):
    return x * jax.nn.sigmoid(x)


def _vec3(p):
    return p.reshape(p.shape[0], 1, p.shape[1])


def _mod_kernel(c_ref, w_ref, b_ref, op_ref, os_ref, *, n_p, s0):
    a = _silu(c_ref[...]).astype(BF16)
    m = jnp.dot(a, w_ref[...].astype(BF16), preferred_element_type=F32) + b_ref[...]
    op_ref[...] = m[0:n_p][:, None, :]
    os_ref[...] = m[s0:][:, None, :]


def _modulation(c_prompt, c_sample, w_ada, b_ada, tn=1024):
    depth, d, n = w_ada.shape
    n_p, n_s = c_prompt.shape[0], c_sample.shape[0]
    s0 = -(-n_p // BF16_ROWS) * BF16_ROWS
    c_all = jnp.concatenate([c_prompt, jnp.zeros((s0 - n_p, d), F32), c_sample], axis=0)
    rows = c_all.shape[0]
    assert rows % BF16_ROWS == 0
    return pl.pallas_call(
        functools.partial(_mod_kernel, n_p=n_p, s0=s0),
        grid=(depth, n // tn),
        in_specs=[pl.BlockSpec((rows, d), lambda l, j: (0, 0)),
                  pl.BlockSpec((None, d, tn), lambda l, j: (l, 0, j)),
                  pl.BlockSpec((None, 1, tn), lambda l, j: (l, 0, j))],
        out_specs=[pl.BlockSpec((None, n_p, 1, tn), lambda l, j: (l, 0, 0, j)),
                   pl.BlockSpec((None, n_s, 1, tn), lambda l, j: (l, 0, 0, j))],
        out_shape=[jax.ShapeDtypeStruct((depth, n_p, 1, n), F32),
                   jax.ShapeDtypeStruct((depth, n_s, 1, n), F32)],
        compiler_params=_params(2),
        name="adaln_mod",
    )(c_all, w_ada, b_ada.reshape(depth, 1, n))


K_SH1, K_SC1, K_G1, K_SH2, K_SC2, K_G2 = range(6)


def _mod_spec(l, k, tm, seq_len, d, grid_rank):
    if seq_len >= tm:
        tps = seq_len // tm
        return pl.BlockSpec((None, 1, 1, d), lambda *g: (l, g[grid_rank - 1] // tps, 0, k))
    return pl.BlockSpec((None, tm // seq_len, 1, d), lambda *g: (l, g[grid_rank - 1], 0, k))


def _rows_of(m_ref, r0, nrows, seq_len):
    d = m_ref.shape[-1]
    if m_ref.shape[0] == 1:
        return m_ref[0]
    nb = nrows // seq_len
    m = m_ref[r0 // seq_len:r0 // seq_len + nb]
    return jnp.broadcast_to(m, (nb, seq_len, d)).reshape(nrows, d)


def _prenorm_kernel(x_ref, g_ref, sc_ref, sh_ref, h_ref, *, seq_len):
    tm = x_ref.shape[0]
    sc = _rows_of(sc_ref, 0, tm, seq_len)
    sh = _rows_of(sh_ref, 0, tm, seq_len)
    h_ref[...] = (_rms(x_ref[...], g_ref[...]) * (1.0 + sc) + sh).astype(BF16)


def _prenorm(x, g_pre, mod, l, k_sc, k_sh, seq_len, tm):
    rows, d = x.shape
    act = pl.BlockSpec((tm, d), lambda i: (i, 0))
    return pl.pallas_call(
        functools.partial(_prenorm_kernel, seq_len=seq_len), grid=(rows // tm,),
        in_specs=[act, pl.BlockSpec((None, 1, d), lambda i: (l, 0, 0)),
                  _mod_spec(l, k_sc, tm, seq_len, d, 1), _mod_spec(l, k_sh, tm, seq_len, d, 1)],
        out_specs=act,
        out_shape=jax.ShapeDtypeStruct((rows, d), BF16),
        compiler_params=_params(1), name="prenorm",
    )(x, _vec3(g_pre), mod, mod)


def _cast_kernel(w_ref, o_ref):
    o_ref[...] = w_ref[...].astype(BF16)


def _cast_bf16(w, tk=512):
    depth, k, n = w.shape
    spec = pl.BlockSpec((None, tk, n), lambda l, i: (l, i, 0))
    return pl.pallas_call(
        _cast_kernel, grid=(depth, k // tk), in_specs=[spec], out_specs=spec,
        out_shape=jax.ShapeDtypeStruct(w.shape, BF16),
        compiler_params=_params(2), name="cast_bf16",
    )(w)


def _mm_post_kernel(m_ref, w_ref, x_ref, gpost_ref, gate_ref, *rest, tm, rc, seq_len, has_next, cast):
    rest = list(rest)
    if has_next:
        gpre_ref, sc_ref, sh_ref = rest[:3]
        rest = rest[3:]
    xo_ref = rest.pop(0)
    h_ref = rest.pop(0) if has_next else None
    if cast:
        wb_ref = rest.pop(0)

        @pl.when(pl.program_id(0) == 0)
        def _():
            wb_ref[...] = w_ref[...].astype(BF16)
    else:
        wb_ref = w_ref

    nk, _, tk = m_ref.shape
    for c in range(tm // rc):
        r0 = c * rc
        rows = slice(r0, r0 + rc)
        t = jnp.dot(m_ref[0, rows, :], wb_ref[0:tk, :], preferred_element_type=F32)
        for kk in range(1, nk):
            t += jnp.dot(m_ref[kk, rows, :], wb_ref[kk * tk:(kk + 1) * tk, :], preferred_element_type=F32)
        x = x_ref[rows, :] + _rows_of(gate_ref, r0, rc, seq_len) * _rms(t, gpost_ref[...])
        xo_ref[rows, :] = x
        if has_next:
            h = (_rms(x, gpre_ref[...]) * (1.0 + _rows_of(sc_ref, r0, rc, seq_len))
                 + _rows_of(sh_ref, r0, rc, seq_len))
            h_ref[rows, :] = h.astype(BF16)


def _matmul_post(m, w, x, g_post, mod, l, k_gate, seq_len, tm, nxt, rc=ROW_CHUNK):
    nk, rows, tk = m.shape
    k = nk * tk
    d = w.shape[2]
    assert k == w.shape[1]
    cast = w.dtype != BF16
    has_next = nxt is not None
    act = pl.BlockSpec((tm, d), lambda i: (i, 0))

    def vec():
        return pl.BlockSpec((None, 1, d), lambda i: (l, 0, 0))
    in_specs = [pl.BlockSpec((nk, tm, tk), lambda i: (0, i, 0)),
                pl.BlockSpec((None, k, d), lambda i: (l, 0, 0), pipeline_mode=pl.Buffered(1)),
                act, vec(), _mod_spec(l, k_gate, tm, seq_len, d, 1)]
    args = [m, w, x, _vec3(g_post), mod]
    out_specs, out_shape = [act], [jax.ShapeDtypeStruct((rows, d), F32)]
    if has_next:
        g_pre, ln, k_sc, k_sh = nxt
        in_specs += [pl.BlockSpec((None, 1, d), lambda i: (ln, 0, 0)),
                     _mod_spec(ln, k_sc, tm, seq_len, d, 1), _mod_spec(ln, k_sh, tm, seq_len, d, 1)]
        args += [_vec3(g_pre), mod, mod]
        out_specs.append(act)
        out_shape.append(jax.ShapeDtypeStruct((rows, d), BF16))
    kern = functools.partial(_mm_post_kernel, tm=tm, rc=rc, seq_len=seq_len, has_next=has_next, cast=cast)
    out = pl.pallas_call(
        kern, grid=(rows // tm,), in_specs=in_specs, out_specs=out_specs, out_shape=out_shape,
        scratch_shapes=[pltpu.VMEM((k, d), BF16)] if cast else [],
        compiler_params=_params(1), name="matmul_post",
    )(*args)
    return (out[0], out[1]) if has_next else (out[0], None)


IN_TILE = 1024
J_U, J_V, J_Q, J_K, J_VR, J_GR, J_GA, J_GB = 0, 1, 2, 3, 4, 5, 6, 8
IN_GROUP_ROWS = 512


def _inproj_kernel(h_ref, w_ref, lng_ref, lnb_ref, o_ref, wb_ref, *, tm, rc, unroll, seq_len, pos0, head_dim):
    j = pl.program_id(0)
    i = pl.program_id(1)

    @pl.when(i == 0)
    def _():
        wb_ref[...] = w_ref[...].astype(BF16)

    def run(epilogue):
        def group(gi, carry):
            for u in range(unroll):
                r0 = pl.multiple_of((gi * unroll + u) * rc, rc)
                acc = jnp.dot(h_ref[pl.ds(r0, rc), :], wb_ref[...], preferred_element_type=F32)
                o_ref[pl.ds(r0, rc), :] = epilogue(acc, r0).astype(BF16)
            return carry
        lax.fori_loop(0, tm // (rc * unroll), group, 0)

    def layer_norm_gelu(acc, r0):
        v = jax.nn.gelu(acc)
        mu = jnp.mean(v, axis=-1, keepdims=True)
        var = jnp.mean(jnp.square(v - mu), axis=-1, keepdims=True)
        return (v - mu) * lax.rsqrt(var + EPS) * lng_ref[...] + lnb_ref[...]

    def rope(acc, r0):
        half = head_dim // 2
        lane = lax.broadcasted_iota(jnp.int32, (rc, head_dim), 1)
        row = lax.broadcasted_iota(jnp.int32, (rc, head_dim), 0)
        pos = pos0 + ((i * tm + r0 + row) & (seq_len - 1))
        inv = jnp.exp((lane & (half - 1)).astype(F32) * (-math.log(ROPE_BASE) / half))
        ang = pos.astype(F32) * inv
        scale = jnp.where(j == J_K, head_dim ** -0.5, 1.0).astype(F32)
        cos = jnp.cos(ang) * scale
        sin = jnp.where(lane < half, -jnp.sin(ang), jnp.sin(ang)) * scale
        heads = [acc[:, h * head_dim:(h + 1) * head_dim] for h in range(IN_TILE // head_dim)]
        return jnp.concatenate([xh * cos + pltpu.roll(xh, half, axis=1) * sin for xh in heads], axis=1)

    pl.when(j == J_U)(lambda: run(lambda acc, c: jax.nn.gelu(acc)))
    pl.when(j == J_V)(lambda: run(layer_norm_gelu))
    pl.when((j == J_Q) | (j == J_K))(lambda: run(rope))
    pl.when(j == J_VR)(lambda: run(lambda acc, c: acc))
    pl.when(j == J_GR)(lambda: run(lambda acc, c: _silu(acc)))
    pl.when(j >= J_GA)(lambda: run(lambda acc, c: jax.nn.sigmoid(acc)))


def _inproj(h, w_in, ln_g, ln_b, l, tm, rc, seq_len, pos0):
    rows, k = h.shape
    n = w_in.shape[2]
    assert seq_len & (seq_len - 1) == 0 and n == 10 * IN_TILE
    unroll = min(IN_GROUP_ROWS // rc, tm // rc)
    kern = functools.partial(_inproj_kernel, tm=tm, rc=rc, unroll=unroll, seq_len=seq_len, pos0=pos0,
                             head_dim=IN_TILE // RET_HEADS)
    vec = pl.BlockSpec((None, 1, IN_TILE), lambda j, i: (l, 0, 0))
    return pl.pallas_call(
        kern, grid=(n // IN_TILE, rows // tm),
        in_specs=[pl.BlockSpec((tm, k), lambda j, i: (i, 0)),
                  pl.BlockSpec((None, k, IN_TILE), lambda j, i: (l, 0, j)),
                  vec, vec],
        out_specs=pl.BlockSpec((None, tm, IN_TILE), lambda j, i: (j, i, 0)),
        out_shape=jax.ShapeDtypeStruct((n // IN_TILE, rows, IN_TILE), BF16),
        scratch_shapes=[pltpu.VMEM((k, IN_TILE), BF16)],
        compiler_params=_params(2), name="inproj",
    )(h, w_in, _vec3(ln_g), _vec3(ln_b))


def _sgu_kernel(u_ref, v_ref, w_ref, b_ref, o_ref, *vo_ref, tm, chunk):
    if vo_ref:
        vo_ref[0][...] = v_ref[...].astype(F32)
    r = lax.broadcasted_iota(jnp.int32, (SGU_CHUNK, SGU_CHUNK), 0)
    c = lax.broadcasted_iota(jnp.int32, (SGU_CHUNK, SGU_CHUNK), 1)
    mask = (c <= r) & ((r & -chunk) == (c & -chunk))
    gd = IN_TILE // SGU_GROUPS
    for g in range(SGU_GROUPS):
        w = jnp.where(mask, w_ref[g], 0.0).astype(BF16)
        bias = b_ref[:, g:g + 1]
        for t in range(tm // SGU_CHUNK):
            rs = slice(t * SGU_CHUNK, (t + 1) * SGU_CHUNK)
            cs = slice(g * gd, (g + 1) * gd)
            z = jnp.dot(w, v_ref[rs, cs], preferred_element_type=F32) + bias
            o_ref[rs, cs] = (u_ref[rs, cs].astype(F32) * z).astype(BF16)


def _sgu_tables(w_s, b_s, seq_len):
    chunk = min(SGU_CHUNK, seq_len)
    reps = SGU_CHUNK // chunk
    w_t = jnp.tile(w_s[:, :, :chunk, :chunk], (1, 1, reps, reps))
    b_t = jnp.tile(jnp.swapaxes(b_s[:, :, :chunk], 1, 2), (1, reps, 1))
    return w_t, b_t


def _sgu(proj, w_t, b_t, l, depth, seq_len, tm, emit_v, prev):
    rows = proj.shape[1]
    kern = functools.partial(_sgu_kernel, tm=tm, chunk=min(SGU_CHUNK, seq_len))
    in_specs = [pl.BlockSpec((None, tm, IN_TILE), lambda i: (J_U, i, 0)),
                pl.BlockSpec((None, tm, IN_TILE), lambda i: (J_V, i, 0)),
                pl.BlockSpec((None, SGU_GROUPS, SGU_CHUNK, SGU_CHUNK), lambda i: (l, 0, 0, 0)),
                pl.BlockSpec((None, SGU_CHUNK, SGU_GROUPS), lambda i: (l, 0, 0))]
    out_specs = [pl.BlockSpec((tm, IN_TILE), lambda i: (i, 0))]
    out_shape = [jax.ShapeDtypeStruct((rows, IN_TILE), BF16)]
    cspecs, cargs, aliases = [], [], {}
    if emit_v:
        kern, cspecs, cargs, aliases = _stacked(kern, 4, prev, depth, l)
        out_specs.append(_stacked_spec(prev, depth, l, (tm, IN_TILE), lambda i: (i, 0)))
        out_shape.append(jax.ShapeDtypeStruct((depth, rows, IN_TILE), F32))
    out = pl.pallas_call(
        kern, grid=(rows // tm,),
        in_specs=in_specs + cspecs, out_specs=out_specs, out_shape=out_shape,
        input_output_aliases=aliases,
        compiler_params=_params(1), name="sgu",
    )(proj, proj, w_t, b_t, *cargs)
    return (out[0], out[1]) if emit_v else (out[0], None)


def _log_gamma(h):
    return math.log(1.0 - 2.0 ** (-5.0 - h))


def _ret_decays(h, c):
    lg = _log_gamma(h)
    ri = lax.broadcasted_iota(jnp.int32, (c, c), 0)
    ci = lax.broadcasted_iota(jnp.int32, (c, c), 1)
    diff = (ri - ci).astype(F32)
    decay = jnp.where(ri >= ci, jnp.exp(lg * jnp.maximum(diff, 0.0)), 0.0)
    idx = lax.broadcasted_iota(jnp.int32, (c, 1), 0).astype(F32)
    return decay, jnp.exp(lg * (idx + 1.0)), jnp.exp(lg * (c - 1.0 - idx))


def _ret_chunk(qh, kh, vh, grh, gnh, s, h, c, decays):
    decay, q_dec, k_dec = decays
    scores = lax.dot_general(qh, kh, (((1,), (1,)), ((), ())), preferred_element_type=F32) * decay
    inner = jnp.dot(scores.astype(BF16), vh, preferred_element_type=F32)
    cross = jnp.dot(qh, s.astype(BF16), preferred_element_type=F32) * q_dec
    kd = (kh.astype(F32) * k_dec).astype(BF16)
    s_new = (s * math.exp(_log_gamma(h) * c)
             + lax.dot_general(kd, vh, (((0,), (0,)), ((), ())), preferred_element_type=F32))
    o = inner + cross
    mu = jnp.mean(o, axis=-1, keepdims=True)
    var = jnp.mean(jnp.square(o - mu), axis=-1, keepdims=True)
    y = (o - mu) * lax.rsqrt(var + EPS) * gnh
    return grh * y, s_new


def _ret_prompt_kernel(q_ref, k_ref, v_ref, gr_ref, gn_ref, o_ref, so_ref, s_ref, dec_ref, *, hd):
    n = pl.program_id(1)

    @pl.when(n == 0)
    def _():
        s_ref[...] = jnp.zeros_like(s_ref)

    @pl.when((n == 0) & (pl.program_id(0) == 0))
    def _():
        for h in range(RET_HEADS):
            dec_ref[h] = _ret_decays(h, RET_CHUNK)[0]

    for h in range(RET_HEADS):
        cs = slice(h * hd, (h + 1) * hd)
        _, q_dec, k_dec = _ret_decays(h, RET_CHUNK)
        y, s_new = _ret_chunk(q_ref[:, cs], k_ref[:, cs], v_ref[:, cs], gr_ref[:, cs].astype(F32),
                              gn_ref[:, cs], s_ref[h], h, RET_CHUNK, (dec_ref[h], q_dec, k_dec))
        o_ref[:, cs] = y.astype(BF16)
        s_ref[h] = s_new

    @pl.when(n == pl.num_programs(1) - 1)
    def _():
        so_ref[...] = s_ref[...]


def _ret_prompt(proj, gn_g, l, depth, nseq, seq_len, prev):
    hd = IN_TILE // RET_HEADS
    nc = seq_len // RET_CHUNK

    def col(jc):
        return pl.BlockSpec((None, RET_CHUNK, IN_TILE), lambda b, n: (jc, b * nc + n, 0))
    kern, cspecs, cargs, aliases = _stacked(functools.partial(_ret_prompt_kernel, hd=hd), 5, prev, depth, l)
    return pl.pallas_call(
        kern, grid=(nseq, nc),
        in_specs=[col(J_Q), col(J_K), col(J_VR), col(J_GR),
                  pl.BlockSpec((None, 1, IN_TILE), lambda b, n: (l, 0, 0))] + cspecs,
        out_specs=[pl.BlockSpec((RET_CHUNK, IN_TILE), lambda b, n: (b * nc + n, 0)),
                   _stacked_spec(prev, depth, l, (None, RET_HEADS, hd, hd), lambda b, n: (b, 0, 0, 0))],
        out_shape=[jax.ShapeDtypeStruct((nseq * seq_len, IN_TILE), BF16),
                   jax.ShapeDtypeStruct((depth, nseq, RET_HEADS, hd, hd), F32)],
        scratch_shapes=[pltpu.VMEM((RET_HEADS, hd, hd), F32),
                        pltpu.VMEM((RET_HEADS, RET_CHUNK, RET_CHUNK), F32)],
        input_output_aliases=aliases,
        compiler_params=_params(2), name="ret_prompt",
    )(proj, proj, proj, proj, _vec3(gn_g), *cargs)


def _ret_sample_kernel(q_ref, k_ref, v_ref, gr_ref, gn_ref, s_ref, o_ref, so_ref, y_ref, *, hd, bs, seq_len):
    q = q_ref[...].astype(F32)
    k = k_ref[...].astype(F32)
    v = v_ref[...].astype(F32)
    gr = gr_ref[...].astype(F32)
    for h in range(RET_HEADS):
        cs = slice(h * hd, (h + 1) * hd)
        decays = _ret_decays(h, seq_len)
        for b in range(bs):
            rs = slice(b * seq_len, (b + 1) * seq_len)
            y, s_new = _ret_chunk(q[rs, cs].astype(BF16), k[rs, cs].astype(BF16), v[rs, cs].astype(BF16),
                                  gr[rs, cs], gn_ref[:, cs], s_ref[b, h], h, seq_len, decays)
            y_ref[rs, cs] = y
            so_ref[b, h] = s_new
    o_ref[...] = y_ref[...].astype(BF16)


def _ret_sample(proj, gn_g, state, l, seq_len, bs, prev):
    depth, nseq = state.shape[:2]
    hd = IN_TILE // RET_HEADS

    def col(jc):
        return pl.BlockSpec((None, bs * seq_len, IN_TILE), lambda i: (jc, i, 0))
    kern, cspecs, cargs, aliases = _stacked(
        functools.partial(_ret_sample_kernel, hd=hd, bs=bs, seq_len=seq_len), 6, prev, depth, l)
    return pl.pallas_call(
        kern, grid=(nseq // bs,),
        in_specs=[col(J_Q), col(J_K), col(J_VR), col(J_GR),
                  pl.BlockSpec((None, 1, IN_TILE), lambda i: (l, 0, 0)),
                  pl.BlockSpec((None, bs, RET_HEADS, hd, hd), lambda i: (l, i, 0, 0, 0))] + cspecs,
        out_specs=[pl.BlockSpec((bs * seq_len, IN_TILE), lambda i: (i, 0)),
                   _stacked_spec(prev, depth, l, (bs, RET_HEADS, hd, hd), lambda i: (i, 0, 0, 0))],
        out_shape=[jax.ShapeDtypeStruct((nseq * seq_len, IN_TILE), BF16),
                   jax.ShapeDtypeStruct(state.shape, F32)],
        scratch_shapes=[pltpu.VMEM((bs * seq_len, IN_TILE), F32)],
        input_output_aliases=aliases,
        compiler_params=_params(1), name="ret_sample",
    )(proj, proj, proj, proj, _vec3(gn_g), state, *cargs)


def _merge_kernel(ya_ref, yb_ref, ga_ref, gb_ref, wa_ref, wb_ref, o_ref, wab_ref, wbb_ref, *, rc):
    @pl.when(pl.program_id(1) == 0)
    def _():
        wab_ref[...] = wa_ref[...].astype(BF16)
        wbb_ref[...] = wb_ref[...].astype(BF16)
    for c in range(o_ref.shape[0] // rc):
        rows = slice(c * rc, (c + 1) * rc)
        a = jnp.dot(ya_ref[rows, :], wab_ref[...], preferred_element_type=F32)
        b = jnp.dot(yb_ref[rows, :], wbb_ref[...], preferred_element_type=F32)
        o_ref[rows, :] = (ga_ref[rows, :].astype(F32) * a + gb_ref[rows, :].astype(F32) * b).astype(BF16)


def _merge(ya, yb, proj, w_a, w_b, l, tm, rc):
    rows, k = ya.shape
    n = w_a.shape[2]
    tn = IN_TILE
    row = pl.BlockSpec((tm, k), lambda j, i: (i, 0))
    wsp = pl.BlockSpec((None, k, tn), lambda j, i: (l, 0, j))
    return pl.pallas_call(
        functools.partial(_merge_kernel, rc=rc), grid=(n // tn, rows // tm),
        in_specs=[row, row,
                  pl.BlockSpec((None, tm, tn), lambda j, i: (J_GA + j, i, 0)),
                  pl.BlockSpec((None, tm, tn), lambda j, i: (J_GB + j, i, 0)),
                  wsp, wsp],
        out_specs=pl.BlockSpec((None, tm, tn), lambda j, i: (j, i, 0)),
        out_shape=jax.ShapeDtypeStruct((n // tn, rows, tn), BF16),
        scratch_shapes=[pltpu.VMEM((k, tn), BF16), pltpu.VMEM((k, tn), BF16)],
        compiler_params=_params(2), name="merge",
    )(ya, yb, proj, proj, w_a, w_b)


def _ffn_up_kernel(h_ref, wg_ref, wu_ref, cw_ref, cb_ref, *rest, tm, rc, seq_len, has_state):
    if has_state:
        st_ref, y_ref, so_ref, wgb_ref, wub_ref, gbuf_ref = rest
    else:
        y_ref, so_ref, wgb_ref, wub_ref, gbuf_ref = rest
    i = pl.program_id(1)
    pad = SUBLANES
    keep = CONV_W - 1

    @pl.when(i == 0)
    def _():
        wgb_ref[...] = wg_ref[...].astype(BF16)
        wub_ref[...] = wu_ref[...].astype(BF16)

    tn = y_ref.shape[1]
    if has_state:
        gbuf_ref[0:pad, :] = jnp.zeros((pad, tn), F32)
    else:
        @pl.when(((i * tm) & (seq_len - 1)) == 0)
        def _():
            gbuf_ref[0:pad, :] = jnp.zeros((pad, tn), F32)
    for c in range(tm // rc):
        r0 = c * rc
        h = h_ref[r0:r0 + rc, :]
        g = jnp.dot(h, wgb_ref[...], preferred_element_type=F32)
        up = jnp.dot(h, wub_ref[...], preferred_element_type=F32)
        gbuf_ref[pad + r0:pad + r0 + rc, :] = g
        g1 = gbuf_ref[pad - 1 + r0:pad - 1 + r0 + rc, :]
        g2 = gbuf_ref[pad - 2 + r0:pad - 2 + r0 + rc, :]
        if has_state:
            bs = rc // seq_len
            st = st_ref[c * bs:(c + 1) * bs]
            s0 = jnp.broadcast_to(st[:, 0:1, :], (bs, seq_len, tn)).reshape(rc, tn)
            s1 = jnp.broadcast_to(st[:, 1:2, :], (bs, seq_len, tn)).reshape(rc, tn)
            t = lax.broadcasted_iota(jnp.int32, (rc, tn), 0) & (seq_len - 1)
            g1 = jnp.where(t == 0, s1, g1)
            g2 = jnp.where(t == 0, s0, jnp.where(t == 1, s1, g2))
            so_ref[c * bs:(c + 1) * bs] = g.reshape(bs, seq_len, tn)[:, seq_len - keep:, :]
        conv = g2 * cw_ref[0:1, :] + g1 * cw_ref[1:2, :] + g * cw_ref[2:3, :] + cb_ref[...]
        y_ref[r0:r0 + rc, :] = (jax.nn.gelu(conv) * up).astype(BF16)
    if not has_state:
        last = gbuf_ref[pad + tm - keep:pad + tm, :]
        gbuf_ref[pad - keep:pad, :] = last
        so_ref[...] = last


def _ffn_up(h, w_gate, w_up, conv_w, conv_b, l, depth, nseq, seq_len, tm, rc, tn, state, prev):
    rows, k = h.shape
    n = w_gate.shape[2]
    keep = CONV_W - 1
    has_state = state is not None
    assert seq_len & (seq_len - 1) == 0 and seq_len >= CONV_W
    kern = functools.partial(_ffn_up_kernel, tm=tm, rc=rc, seq_len=seq_len, has_state=has_state)
    wsp = pl.BlockSpec((None, k, tn), lambda j, i: (l, 0, j))
    in_specs = [pl.BlockSpec((tm, k), lambda j, i: (i, 0)), wsp, wsp,
                pl.BlockSpec((None, CONV_W, tn), lambda j, i: (l, 0, j)),
                pl.BlockSpec((None, 1, tn), lambda j, i: (l, 0, j))]
    args = [h, w_gate, w_up, conv_w, _vec3(conv_b)]
    if has_state:
        bs = tm // seq_len
        in_specs.append(pl.BlockSpec((None, bs, keep, tn), lambda j, i: (l, i, 0, j)))
        args.append(state)
        so_spec = _stacked_spec(prev, depth, l, (bs, keep, tn), lambda j, i: (i, 0, j))
    else:
        tps = seq_len // tm
        so_spec = _stacked_spec(prev, depth, l, (None, keep, tn), lambda j, i: (i // tps, 0, j))
    kern, cspecs, cargs, aliases = _stacked(kern, len(args), prev, depth, l)
    return pl.pallas_call(
        kern, grid=(n // tn, rows // tm),
        in_specs=in_specs + cspecs,
        out_specs=[pl.BlockSpec((None, tm, tn), lambda j, i: (j, i, 0)), so_spec],
        out_shape=[jax.ShapeDtypeStruct((n // tn, rows, tn), BF16),
                   jax.ShapeDtypeStruct((depth, nseq, keep, n), F32)],
        scratch_shapes=[pltpu.VMEM((k, tn), BF16), pltpu.VMEM((k, tn), BF16),
                        pltpu.VMEM((tm + SUBLANES, tn), F32)],
        input_output_aliases=aliases,
        compiler_params=_params(2), name="ffn_up",
    )(*args, *cargs)


def _layer(l, depth, x, h, mod, w, cfg, s_ret, conv_prev, outs):
    nseq, seq_len, tm, tm_post = cfg["nseq"], cfg["seq_len"], cfg["tm"], cfg["tm_post"]
    ret_out, conv_out, v_out = outs

    proj = _inproj(h, w["w_in"], w["sgu_ln_g"], w["sgu_ln_b"], l, cfg["tm_in"], cfg["rc_in"][l], seq_len,
                   cfg["pos0"])
    ya, v_out = _sgu(proj, cfg["sgu_w"], cfg["sgu_b"], l, depth, seq_len, tm, cfg["emit_v"], v_out)
    if s_ret is None:
        yb, ret_out = _ret_prompt(proj, w["ret_gn_g"], l, depth, nseq, seq_len, ret_out)
    else:
        yb, ret_out = _ret_sample(proj, w["ret_gn_g"], s_ret, l, seq_len, cfg["ret_bs"], ret_out)
    merged = _merge(ya, yb, proj, w["w_branch_a"], w["w_branch_b"], l, tm, cfg["rc_merge"])
    x, h2 = _matmul_post(merged, w["w_out"], x, w["norm_post1"], mod, l, K_G1, seq_len, cfg["tm_out"],
                         (w["norm_pre2"], l, K_SC2, K_SH2), rc=cfg["rc_out"][l])

    y, conv_out = _ffn_up(h2, w["ffn_w_gate"], w["ffn_w_up"], w["ffn_conv_w"], w["ffn_conv_b"], l, depth,
                          nseq, seq_len, cfg["tm_ffn"], cfg["rc_ffn"][l], 512, conv_prev, conv_out)
    nxt = None if l == depth - 1 else (w["norm_pre1"], l + 1, K_SC1, K_SH1)
    x, h_next = _matmul_post(y, w["ffn_w_down_bf16"], x, w["norm_post2"], mod, l, K_G2, seq_len, tm_post, nxt,
                             rc=cfg["rc_post"][l])
    return x, h_next, (ret_out, conv_out, v_out)


def kernel(x_prompt, x_sample, state_ret, state_conv, c_prompt, c_sample, w_ada, b_ada, norm_pre1, norm_post1, norm_pre2, norm_post2, w_in, sgu_w_s, sgu_b_s, sgu_ln_g, sgu_ln_b, ret_gn_g, w_branch_a, w_branch_b, w_out, ffn_w_gate, ffn_w_up, ffn_conv_w, ffn_conv_b, ffn_w_down):
    w = dict(norm_pre1=norm_pre1, norm_post1=norm_post1, norm_pre2=norm_pre2, norm_post2=norm_post2, w_in=w_in,
             sgu_ln_g=sgu_ln_g, sgu_ln_b=sgu_ln_b, ret_gn_g=ret_gn_g, w_branch_a=w_branch_a,
             w_branch_b=w_branch_b, w_out=w_out, ffn_w_gate=ffn_w_gate, ffn_w_up=ffn_w_up,
             ffn_conv_w=ffn_conv_w, ffn_conv_b=ffn_conv_b, ffn_w_down_bf16=_cast_bf16(ffn_w_down))
    depth = w_ada.shape[0]
    nb, seq, d = x_prompt.shape
    ndb, dseq, _ = x_sample.shape

    mod_p, mod_s = _modulation(c_prompt, c_sample, w_ada, b_ada)

    sgu_wp, sgu_bp = _sgu_tables(sgu_w_s, sgu_b_s, seq)
    sgu_ws, sgu_bs = _sgu_tables(sgu_w_s, sgu_b_s, dseq)
    cfg_p = dict(nseq=nb, seq_len=seq, tm=1024, tm_in=2048, rc_in=(256, 256), rc_merge=256, tm_ffn=2048,
                 rc_ffn=(128, 256), tm_out=512, rc_out=(256, 256), tm_post=256, rc_post=(128, 256), pos0=0,
                 sgu_w=sgu_wp, sgu_b=sgu_bp, emit_v=False)
    cfg_s = dict(nseq=ndb, seq_len=dseq, tm=1024, tm_in=1024, rc_in=(128, 256), rc_merge=256, tm_ffn=1024,
                 rc_ffn=(128, 256), tm_out=512, rc_out=(128, 256), tm_post=256, rc_post=(128, 256), pos0=PAST_LEN,
                 ret_bs=8, sgu_w=sgu_ws, sgu_b=sgu_bs, emit_v=True)

    xp = x_prompt.reshape(nb * seq, d)
    xs = x_sample.reshape(ndb * dseq, d)
    hp = _prenorm(xp, norm_pre1, mod_p, 0, K_SC1, K_SH1, seq, 512)
    hs = _prenorm(xs, norm_pre1, mod_s, 0, K_SC1, K_SH1, dseq, 512)
    outs_p = outs_s = (None, None, None)
    for l in range(depth):
        xp, hp, outs_p = _layer(l, depth, xp, hp, mod_p, w, cfg_p, None, None, outs_p)
        xs, hs, outs_s = _layer(l, depth, xs, hs, mod_s, w, cfg_s, state_ret, state_conv, outs_s)
    ret_p, conv_p, _ = outs_p
    ret_s, conv_s, v_s = outs_s
    return (xp.reshape(nb, seq, d), xs.reshape(ndb, dseq, d), ret_p, ret_s, conv_p, conv_s,
            v_s.reshape(depth, ndb, dseq, -1))
```

```python
import functools
import math

import jax
import jax.numpy as jnp
from jax import lax
from jax.experimental import pallas as pl
from jax.experimental.pallas import tpu as pltpu

F32 = jnp.float32
BF16 = jnp.bfloat16

EPS = 1e-6
ROPE_BASE = 10000.0
PAST_LEN = 16384
SGU_GROUPS = 8
SGU_CHUNK = 128
RET_HEADS = 8
RET_CHUNK = 128
CONV_W = 3

SUBLANES = 8
BF16_ROWS = 16
FFN_TILE = 512
VMEM_LIMIT_BYTES = 56 * 1024 * 1024


def _params(n_axes):
    return pltpu.CompilerParams(dimension_semantics=("arbitrary",) * n_axes,
                                vmem_limit_bytes=VMEM_LIMIT_BYTES)


def _stacked(kernel, n_in, prev, depth, l):
    if prev is None:
        def first(*refs):
            refs = list(refs)
            so = refs[n_in + 1]
            for k in range(depth):
                if k != l:
                    so[k] = jnp.zeros(so.shape[1:], so.dtype)
            refs[n_in + 1] = so.at[l]
            kernel(*refs)
        return first, [], [], {}

    def later(*refs):
        kernel(*refs[:n_in], *refs[n_in + 1:])
    return later, [pl.BlockSpec(memory_space=pl.ANY)], [prev], {n_in: 1}


def _stacked_spec(prev, depth, l, block, index_fn):
    if prev is None:
        return pl.BlockSpec((depth,) + block, lambda *g: (0,) + index_fn(*g))
    return pl.BlockSpec((None,) + block, lambda *g: (l,) + index_fn(*g))


def _rms(x, g):
    return x * lax.rsqrt(jnp.mean(x * x, axis=-1, keepdims=True) + EPS) * g


def _silu(x):
    return x * jax.nn.sigmoid(x)


def _vec3(p):
    return p.reshape(p.shape[0], 1, p.shape[1])


def _mod_kernel(c_ref, w_ref, b_ref, op_ref, os_ref, *, n_p, s0):
    a = _silu(c_ref[...]).astype(BF16)
    m = jnp.dot(a, w_ref[...].astype(BF16), preferred_element_type=F32) + b_ref[...]
    op_ref[...] = m[0:n_p][:, None, :]
    os_ref[...] = m[s0:][:, None, :]


def _modulation(c_prompt, c_sample, w_ada, b_ada, tn=2048):
    depth, d, n = w_ada.shape
    n_p, n_s = c_prompt.shape[0], c_sample.shape[0]
    s0 = -(-n_p // BF16_ROWS) * BF16_ROWS
    c_all = jnp.concatenate([c_prompt, jnp.zeros((s0 - n_p, d), F32), c_sample], axis=0)
    rows = c_all.shape[0]
    assert rows % BF16_ROWS == 0
    return pl.pallas_call(
        functools.partial(_mod_kernel, n_p=n_p, s0=s0),
        grid=(depth, n // tn),
        in_specs=[pl.BlockSpec((rows, d), lambda l, j: (0, 0)),
                  pl.BlockSpec((None, d, tn), lambda l, j: (l, 0, j)),
                  pl.BlockSpec((None, 1, tn), lambda l, j: (l, 0, j))],
        out_specs=[pl.BlockSpec((None, n_p, 1, tn), lambda l, j: (l, 0, 0, j)),
                   pl.BlockSpec((None, n_s, 1, tn), lambda l, j: (l, 0, 0, j))],
        out_shape=[jax.ShapeDtypeStruct((depth, n_p, 1, n), F32),
                   jax.ShapeDtypeStruct((depth, n_s, 1, n), F32)],
        compiler_params=_params(2),
        name="adaln_mod",
    )(c_all, w_ada, b_ada.reshape(depth, 1, n))


K_SH1, K_SC1, K_G1, K_SH2, K_SC2, K_G2 = range(6)


def _mod_spec(l, k, tm, seq_len, d, grid_rank):
    if seq_len >= tm:
        tps = seq_len // tm
        return pl.BlockSpec((None, 1, 1, d), lambda *g: (l, g[grid_rank - 1] // tps, 0, k))
    return pl.BlockSpec((None, tm // seq_len, 1, d), lambda *g: (l, g[grid_rank - 1], 0, k))


def _rows_of(m_ref, r0, nrows, seq_len):
    d = m_ref.shape[-1]
    if m_ref.shape[0] == 1:
        return m_ref[0]
    nb = nrows // seq_len
    m = m_ref[r0 // seq_len:r0 // seq_len + nb]
    return jnp.broadcast_to(m, (nb, seq_len, d)).reshape(nrows, d)


def _prenorm_kernel(x_ref, g_ref, sc_ref, sh_ref, h_ref, *, seq_len):
    tm = x_ref.shape[0]
    sc = _rows_of(sc_ref, 0, tm, seq_len)
    sh = _rows_of(sh_ref, 0, tm, seq_len)
    h_ref[...] = (_rms(x_ref[...], g_ref[...]) * (1.0 + sc) + sh).astype(BF16)


def _prenorm(x, g_pre, mod, l, k_sc, k_sh, seq_len, tm):
    rows, d = x.shape
    act = pl.BlockSpec((tm, d), lambda i: (i, 0))
    return pl.pallas_call(
        functools.partial(_prenorm_kernel, seq_len=seq_len), grid=(rows // tm,),
        in_specs=[act, pl.BlockSpec((None, 1, d), lambda i: (l, 0, 0)),
                  _mod_spec(l, k_sc, tm, seq_len, d, 1), _mod_spec(l, k_sh, tm, seq_len, d, 1)],
        out_specs=act,
        out_shape=jax.ShapeDtypeStruct((rows, d), BF16),
        compiler_params=_params(1), name="prenorm",
    )(x, _vec3(g_pre), mod, mod)


def _cast_kernel(w_ref, o_ref):
    o_ref[...] = w_ref[...].astype(BF16)


def _cast_bf16(w, steps=4):
    depth, k, n = w.shape
    tk = k // steps
    assert tk * steps == k and tk % BF16_ROWS == 0
    spec = pl.BlockSpec((None, tk, n), lambda l, i: (l, i, 0))
    return pl.pallas_call(
        _cast_kernel, grid=(depth, k // tk), in_specs=[spec], out_specs=spec,
        out_shape=jax.ShapeDtypeStruct(w.shape, BF16),
        compiler_params=_params(2), name="cast_bf16",
    )(w)


def _mm_post_kernel(m_ref, w_ref, x_ref, gpost_ref, gate_ref, *rest, tm, rc, seq_len, has_next, cast):
    rest = list(rest)
    if has_next:
        gpre_ref, sc_ref, sh_ref = rest[:3]
        rest = rest[3:]
    xo_ref = rest.pop(0)
    h_ref = rest.pop(0) if has_next else None
    if cast:
        wb_ref = rest.pop(0)

        @pl.when(pl.program_id(0) == 0)
        def _():
            wb_ref[...] = w_ref[...].astype(BF16)
    else:
        wb_ref = w_ref

    nk, _, tk = m_ref.shape
    for c in range(tm // rc):
        r0 = c * rc
        rows = slice(r0, r0 + rc)
        t = jnp.dot(m_ref[0, rows, :], wb_ref[0:tk, :], preferred_element_type=F32)
        for kk in range(1, nk):
            t += jnp.dot(m_ref[kk, rows, :], wb_ref[kk * tk:(kk + 1) * tk, :], preferred_element_type=F32)
        x = x_ref[rows, :] + _rows_of(gate_ref, r0, rc, seq_len) * _rms(t, gpost_ref[...])
        xo_ref[rows, :] = x
        if has_next:
            h = (_rms(x, gpre_ref[...]) * (1.0 + _rows_of(sc_ref, r0, rc, seq_len))
                 + _rows_of(sh_ref, r0, rc, seq_len))
            h_ref[rows, :] = h.astype(BF16)


def _matmul_post(m, w, x, g_post, mod, l, k_gate, seq_len, tm, nxt, rc):
    nk, rows, tk = m.shape
    k = nk * tk
    d = w.shape[2]
    assert k == w.shape[1]
    cast = w.dtype != BF16
    has_next = nxt is not None
    act = pl.BlockSpec((tm, d), lambda i: (i, 0))

    def vec():
        return pl.BlockSpec((None, 1, d), lambda i: (l, 0, 0))
    in_specs = [pl.BlockSpec((nk, tm, tk), lambda i: (0, i, 0)),
                pl.BlockSpec((None, k, d), lambda i: (l, 0, 0), pipeline_mode=pl.Buffered(1)),
                act, vec(), _mod_spec(l, k_gate, tm, seq_len, d, 1)]
    args = [m, w, x, _vec3(g_post), mod]
    out_specs, out_shape = [act], [jax.ShapeDtypeStruct((rows, d), F32)]
    if has_next:
        g_pre, ln, k_sc, k_sh = nxt
        in_specs += [pl.BlockSpec((None, 1, d), lambda i: (ln, 0, 0)),
                     _mod_spec(ln, k_sc, tm, seq_len, d, 1), _mod_spec(ln, k_sh, tm, seq_len, d, 1)]
        args += [_vec3(g_pre), mod, mod]
        out_specs.append(act)
        out_shape.append(jax.ShapeDtypeStruct((rows, d), BF16))
    kern = functools.partial(_mm_post_kernel, tm=tm, rc=rc, seq_len=seq_len, has_next=has_next, cast=cast)
    out = pl.pallas_call(
        kern, grid=(rows // tm,), in_specs=in_specs, out_specs=out_specs, out_shape=out_shape,
        scratch_shapes=[pltpu.VMEM((k, d), BF16)] if cast else [],
        compiler_params=_params(1), name="matmul_post",
    )(*args)
    return (out[0], out[1]) if has_next else (out[0], None)


IN_TILE = 1024
J_U, J_V, J_Q, J_K, J_VR, J_GR, J_GA, J_GB = 0, 1, 2, 3, 4, 5, 6, 8
IN_GROUP_ROWS = 512


def _inproj_kernel(h_ref, w_ref, lng_ref, lnb_ref, o_ref, wb_ref, *, tm, rc, unroll, seq_len, pos0, head_dim):
    j = pl.program_id(0)
    i = pl.program_id(1)

    @pl.when(i == 0)
    def _():
        wb_ref[...] = w_ref[...].astype(BF16)

    def run(epilogue):
        def group(gi, carry):
            for u in range(unroll):
                r0 = pl.multiple_of((gi * unroll + u) * rc, rc)
                acc = jnp.dot(h_ref[pl.ds(r0, rc), :], wb_ref[...], preferred_element_type=F32)
                o_ref[pl.ds(r0, rc), :] = epilogue(acc, r0).astype(BF16)
            return carry
        lax.fori_loop(0, tm // (rc * unroll), group, 0)

    def layer_norm_gelu(acc, r0):
        v = jax.nn.gelu(acc)
        mu = jnp.mean(v, axis=-1, keepdims=True)
        var = jnp.mean(jnp.square(v - mu), axis=-1, keepdims=True)
        return (v - mu) * lax.rsqrt(var + EPS) * lng_ref[...] + lnb_ref[...]

    def rope(acc, r0):
        half = head_dim // 2
        lane = lax.broadcasted_iota(jnp.int32, (rc, head_dim), 1)
        row = lax.broadcasted_iota(jnp.int32, (rc, head_dim), 0)
        pos = pos0 + ((i * tm + r0 + row) & (seq_len - 1))
        inv = jnp.exp((lane & (half - 1)).astype(F32) * (-math.log(ROPE_BASE) / half))
        ang = pos.astype(F32) * inv
        scale = jnp.where(j == J_K, head_dim ** -0.5, 1.0).astype(F32)
        cos = jnp.cos(ang) * scale
        sin = jnp.where(lane < half, -jnp.sin(ang), jnp.sin(ang)) * scale
        heads = [acc[:, h * head_dim:(h + 1) * head_dim] for h in range(IN_TILE // head_dim)]
        return jnp.concatenate([xh * cos + pltpu.roll(xh, half, axis=1) * sin for xh in heads], axis=1)

    pl.when(j == J_U)(lambda: run(lambda acc, c: jax.nn.gelu(acc)))
    pl.when(j == J_V)(lambda: run(layer_norm_gelu))
    pl.when((j == J_Q) | (j == J_K))(lambda: run(rope))
    pl.when(j == J_VR)(lambda: run(lambda acc, c: acc))
    pl.when(j == J_GR)(lambda: run(lambda acc, c: _silu(acc)))
    pl.when(j >= J_GA)(lambda: run(lambda acc, c: jax.nn.sigmoid(acc)))


def _inproj(h, w_in, ln_g, ln_b, l, tm, rc, seq_len, pos0):
    rows, k = h.shape
    n = w_in.shape[2]
    assert seq_len & (seq_len - 1) == 0 and n == 10 * IN_TILE
    unroll = min(IN_GROUP_ROWS // rc, tm // rc)
    kern = functools.partial(_inproj_kernel, tm=tm, rc=rc, unroll=unroll, seq_len=seq_len, pos0=pos0,
                             head_dim=IN_TILE // RET_HEADS)
    vec = pl.BlockSpec((None, 1, IN_TILE), lambda j, i: (l, 0, 0))
    return pl.pallas_call(
        kern, grid=(n // IN_TILE, rows // tm),
        in_specs=[pl.BlockSpec((tm, k), lambda j, i: (i, 0)),
                  pl.BlockSpec((None, k, IN_TILE), lambda j, i: (l, 0, j)),
                  vec, vec],
        out_specs=pl.BlockSpec((None, tm, IN_TILE), lambda j, i: (j, i, 0)),
        out_shape=jax.ShapeDtypeStruct((n // IN_TILE, rows, IN_TILE), BF16),
        scratch_shapes=[pltpu.VMEM((k, IN_TILE), BF16)],
        compiler_params=_params(2), name="inproj",
    )(h, w_in, _vec3(ln_g), _vec3(ln_b))


def _sgu_kernel(u_ref, v_ref, w_ref, b_ref, o_ref, *vo_ref, tm, chunk):
    if vo_ref:
        vo_ref[0][...] = v_ref[...].astype(F32)
    r = lax.broadcasted_iota(jnp.int32, (SGU_CHUNK, SGU_CHUNK), 0)
    c = lax.broadcasted_iota(jnp.int32, (SGU_CHUNK, SGU_CHUNK), 1)
    mask = (c <= r) & ((r & -chunk) == (c & -chunk))
    gd = IN_TILE // SGU_GROUPS
    for g in range(SGU_GROUPS):
        w = jnp.where(mask, w_ref[g], 0.0).astype(BF16)
        bias = b_ref[:, g:g + 1]
        for t in range(tm // SGU_CHUNK):
            rs = slice(t * SGU_CHUNK, (t + 1) * SGU_CHUNK)
            cs = slice(g * gd, (g + 1) * gd)
            z = jnp.dot(w, v_ref[rs, cs], preferred_element_type=F32) + bias
            o_ref[rs, cs] = (u_ref[rs, cs].astype(F32) * z).astype(BF16)


def _sgu_tables(w_s, b_s, seq_len):
    chunk = min(SGU_CHUNK, seq_len)
    reps = SGU_CHUNK // chunk
    w_t = jnp.tile(w_s[:, :, :chunk, :chunk], (1, 1, reps, reps))
    b_t = jnp.tile(jnp.swapaxes(b_s[:, :, :chunk], 1, 2), (1, reps, 1))
    return w_t, b_t


def _sgu(proj, w_t, b_t, l, depth, seq_len, tm, emit_v, prev):
    rows = proj.shape[1]
    kern = functools.partial(_sgu_kernel, tm=tm, chunk=min(SGU_CHUNK, seq_len))
    in_specs = [pl.BlockSpec((None, tm, IN_TILE), lambda i: (J_U, i, 0)),
                pl.BlockSpec((None, tm, IN_TILE), lambda i: (J_V, i, 0)),
                pl.BlockSpec((None, SGU_GROUPS, SGU_CHUNK, SGU_CHUNK), lambda i: (l, 0, 0, 0)),
                pl.BlockSpec((None, SGU_CHUNK, SGU_GROUPS), lambda i: (l, 0, 0))]
    out_specs = [pl.BlockSpec((tm, IN_TILE), lambda i: (i, 0))]
    out_shape = [jax.ShapeDtypeStruct((rows, IN_TILE), BF16)]
    cspecs, cargs, aliases = [], [], {}
    if emit_v:
        kern, cspecs, cargs, aliases = _stacked(kern, 4, prev, depth, l)
        out_specs.append(_stacked_spec(prev, depth, l, (tm, IN_TILE), lambda i: (i, 0)))
        out_shape.append(jax.ShapeDtypeStruct((depth, rows, IN_TILE), F32))
    out = pl.pallas_call(
        kern, grid=(rows // tm,),
        in_specs=in_specs + cspecs, out_specs=out_specs, out_shape=out_shape,
        input_output_aliases=aliases,
        compiler_params=_params(1), name="sgu",
    )(proj, proj, w_t, b_t, *cargs)
    return (out[0], out[1]) if emit_v else (out[0], None)


RET_CHUNKS_PER_STEP = 2


def _log_gamma(h):
    return math.log(1.0 - 2.0 ** (-5.0 - h))


def _ret_decays(h, c):
    lg = _log_gamma(h)
    ri = lax.broadcasted_iota(jnp.int32, (c, c), 0)
    ci = lax.broadcasted_iota(jnp.int32, (c, c), 1)
    diff = (ri - ci).astype(F32)
    decay = jnp.where(ri >= ci, jnp.exp(lg * jnp.maximum(diff, 0.0)), 0.0)
    idx = lax.broadcasted_iota(jnp.int32, (c, 1), 0).astype(F32)
    return decay, jnp.exp(lg * (idx + 1.0)), jnp.exp(lg * (c - 1.0 - idx))


def _ret_chunk(qh, kh, vh, grh, gnh, s, h, c, decays):
    decay, q_dec, k_dec = decays
    scores = lax.dot_general(qh, kh, (((1,), (1,)), ((), ())), preferred_element_type=F32) * decay
    inner = jnp.dot(scores.astype(BF16), vh, preferred_element_type=F32)
    cross = jnp.dot(qh, s.astype(BF16), preferred_element_type=F32) * q_dec
    kd = (kh.astype(F32) * k_dec).astype(BF16)
    s_new = (s * math.exp(_log_gamma(h) * c)
             + lax.dot_general(kd, vh, (((0,), (0,)), ((), ())), preferred_element_type=F32))
    o = inner + cross
    mu = jnp.mean(o, axis=-1, keepdims=True)
    var = jnp.mean(jnp.square(o - mu), axis=-1, keepdims=True)
    y = (o - mu) * lax.rsqrt(var + EPS) * gnh
    return grh * y, s_new


def _ret_prompt_kernel(q_ref, k_ref, v_ref, gr_ref, gn_ref, o_ref, so_ref, s_ref, dec_ref, *, hd):
    n = pl.program_id(1)

    @pl.when(n == 0)
    def _():
        s_ref[...] = jnp.zeros_like(s_ref)

    @pl.when((n == 0) & (pl.program_id(0) == 0))
    def _():
        for h in range(RET_HEADS):
            dec_ref[h] = _ret_decays(h, RET_CHUNK)[0]

    for h in range(RET_HEADS):
        cs = slice(h * hd, (h + 1) * hd)
        _, q_dec, k_dec = _ret_decays(h, RET_CHUNK)
        s = s_ref[h]
        for ci in range(q_ref.shape[0] // RET_CHUNK):
            rs = slice(ci * RET_CHUNK, (ci + 1) * RET_CHUNK)
            y, s = _ret_chunk(q_ref[rs, cs], k_ref[rs, cs], v_ref[rs, cs], gr_ref[rs, cs].astype(F32),
                              gn_ref[:, cs], s, h, RET_CHUNK, (dec_ref[h], q_dec, k_dec))
            o_ref[rs, cs] = y.astype(BF16)
        s_ref[h] = s

    @pl.when(n == pl.num_programs(1) - 1)
    def _():
        so_ref[...] = s_ref[...]


def _ret_prompt(proj, gn_g, l, depth, nseq, seq_len, prev):
    hd = IN_TILE // RET_HEADS
    tr = RET_CHUNKS_PER_STEP * RET_CHUNK
    nc = seq_len // tr

    def col(jc):
        return pl.BlockSpec((None, tr, IN_TILE), lambda b, n: (jc, b * nc + n, 0))
    kern, cspecs, cargs, aliases = _stacked(functools.partial(_ret_prompt_kernel, hd=hd), 5, prev, depth, l)
    return pl.pallas_call(
        kern, grid=(nseq, nc),
        in_specs=[col(J_Q), col(J_K), col(J_VR), col(J_GR),
                  pl.BlockSpec((None, 1, IN_TILE), lambda b, n: (l, 0, 0))] + cspecs,
        out_specs=[pl.BlockSpec((tr, IN_TILE), lambda b, n: (b * nc + n, 0)),
                   _stacked_spec(prev, depth, l, (None, RET_HEADS, hd, hd), lambda b, n: (b, 0, 0, 0))],
        out_shape=[jax.ShapeDtypeStruct((nseq * seq_len, IN_TILE), BF16),
                   jax.ShapeDtypeStruct((depth, nseq, RET_HEADS, hd, hd), F32)],
        scratch_shapes=[pltpu.VMEM((RET_HEADS, hd, hd), F32),
                        pltpu.VMEM((RET_HEADS, RET_CHUNK, RET_CHUNK), F32)],
        input_output_aliases=aliases,
        compiler_params=_params(2), name="ret_prompt",
    )(proj, proj, proj, proj, _vec3(gn_g), *cargs)


def _ret_sample_kernel(q_ref, k_ref, v_ref, gr_ref, gn_ref, s_ref, o_ref, so_ref, y_ref, *, hd, bs, seq_len):
    q = q_ref[...].astype(F32)
    k = k_ref[...].astype(F32)
    v = v_ref[...].astype(F32)
    gr = gr_ref[...].astype(F32)
    for h in range(RET_HEADS):
        cs = slice(h * hd, (h + 1) * hd)
        decays = _ret_decays(h, seq_len)
        for b in range(bs):
            rs = slice(b * seq_len, (b + 1) * seq_len)
            y, s_new = _ret_chunk(q[rs, cs].astype(BF16), k[rs, cs].astype(BF16), v[rs, cs].astype(BF16),
                                  gr[rs, cs], gn_ref[:, cs], s_ref[b, h], h, seq_len, decays)
            y_ref[rs, cs] = y
            so_ref[b, h] = s_new
    o_ref[...] = y_ref[...].astype(BF16)


def _ret_sample(proj, gn_g, state, l, seq_len, bs, prev):
    depth, nseq = state.shape[:2]
    hd = IN_TILE // RET_HEADS

    def col(jc):
        return pl.BlockSpec((None, bs * seq_len, IN_TILE), lambda i: (jc, i, 0))
    kern, cspecs, cargs, aliases = _stacked(
        functools.partial(_ret_sample_kernel, hd=hd, bs=bs, seq_len=seq_len), 6, prev, depth, l)
    return pl.pallas_call(
        kern, grid=(nseq // bs,),
        in_specs=[col(J_Q), col(J_K), col(J_VR), col(J_GR),
                  pl.BlockSpec((None, 1, IN_TILE), lambda i: (l, 0, 0)),
                  pl.BlockSpec((None, bs, RET_HEADS, hd, hd), lambda i: (l, i, 0, 0, 0))] + cspecs,
        out_specs=[pl.BlockSpec((bs * seq_len, IN_TILE), lambda i: (i, 0)),
                   _stacked_spec(prev, depth, l, (bs, RET_HEADS, hd, hd), lambda i: (i, 0, 0, 0))],
        out_shape=[jax.ShapeDtypeStruct((nseq * seq_len, IN_TILE), BF16),
                   jax.ShapeDtypeStruct(state.shape, F32)],
        scratch_shapes=[pltpu.VMEM((bs * seq_len, IN_TILE), F32)],
        input_output_aliases=aliases,
        compiler_params=_params(1), name="ret_sample",
    )(proj, proj, proj, proj, _vec3(gn_g), state, *cargs)


def _merge_kernel(ya_ref, yb_ref, ga_ref, gb_ref, wa_ref, wb_ref, o_ref, wab_ref, wbb_ref, *, rc):
    @pl.when(pl.program_id(1) == 0)
    def _():
        wab_ref[...] = wa_ref[...].astype(BF16)
        wbb_ref[...] = wb_ref[...].astype(BF16)
    for c in range(o_ref.shape[0] // rc):
        rows = slice(c * rc, (c + 1) * rc)
        a = jnp.dot(ya_ref[rows, :], wab_ref[...], preferred_element_type=F32)
        b = jnp.dot(yb_ref[rows, :], wbb_ref[...], preferred_element_type=F32)
        o_ref[rows, :] = (ga_ref[rows, :].astype(F32) * a + gb_ref[rows, :].astype(F32) * b).astype(BF16)


def _merge(ya, yb, proj, w_a, w_b, l, tm, rc):
    rows, k = ya.shape
    n = w_a.shape[2]
    tn = IN_TILE
    row = pl.BlockSpec((tm, k), lambda j, i: (i, 0))
    wsp = pl.BlockSpec((None, k, tn), lambda j, i: (l, 0, j))
    return pl.pallas_call(
        functools.partial(_merge_kernel, rc=rc), grid=(n // tn, rows // tm),
        in_specs=[row, row,
                  pl.BlockSpec((None, tm, tn), lambda j, i: (J_GA + j, i, 0)),
                  pl.BlockSpec((None, tm, tn), lambda j, i: (J_GB + j, i, 0)),
                  wsp, wsp],
        out_specs=pl.BlockSpec((None, tm, tn), lambda j, i: (j, i, 0)),
        out_shape=jax.ShapeDtypeStruct((n // tn, rows, tn), BF16),
        scratch_shapes=[pltpu.VMEM((k, tn), BF16), pltpu.VMEM((k, tn), BF16)],
        compiler_params=_params(2), name="merge",
    )(ya, yb, proj, proj, w_a, w_b)


def _ffn_up_kernel(h_ref, wg_ref, wu_ref, cw_ref, cb_ref, *rest, tm, rc, seq_len, has_state):
    if has_state:
        st_ref, y_ref, so_ref, wgb_ref, wub_ref, gbuf_ref = rest
    else:
        y_ref, so_ref, wgb_ref, wub_ref, gbuf_ref = rest
    i = pl.program_id(1)
    pad = SUBLANES
    keep = CONV_W - 1

    @pl.when(i == 0)
    def _():
        wgb_ref[...] = wg_ref[...].astype(BF16)
        wub_ref[...] = wu_ref[...].astype(BF16)

    tn = y_ref.shape[1]
    if has_state:
        gbuf_ref[0:pad, :] = jnp.zeros((pad, tn), F32)
    else:
        @pl.when(((i * tm) & (seq_len - 1)) == 0)
        def _():
            gbuf_ref[0:pad, :] = jnp.zeros((pad, tn), F32)
    for c in range(tm // rc):
        r0 = c * rc
        h = h_ref[r0:r0 + rc, :]
        g = jnp.dot(h, wgb_ref[...], preferred_element_type=F32)
        up = jnp.dot(h, wub_ref[...], preferred_element_type=F32)
        gbuf_ref[pad + r0:pad + r0 + rc, :] = g
        g1 = gbuf_ref[pad - 1 + r0:pad - 1 + r0 + rc, :]
        g2 = gbuf_ref[pad - 2 + r0:pad - 2 + r0 + rc, :]
        if has_state:
            bs = rc // seq_len
            st = st_ref[c * bs:(c + 1) * bs]
            s0 = jnp.broadcast_to(st[:, 0:1, :], (bs, seq_len, tn)).reshape(rc, tn)
            s1 = jnp.broadcast_to(st[:, 1:2, :], (bs, seq_len, tn)).reshape(rc, tn)
            t = lax.broadcasted_iota(jnp.int32, (rc, tn), 0) & (seq_len - 1)
            g1 = jnp.where(t == 0, s1, g1)
            g2 = jnp.where(t == 0, s0, jnp.where(t == 1, s1, g2))
            so_ref[c * bs:(c + 1) * bs] = g.reshape(bs, seq_len, tn)[:, seq_len - keep:, :]
        conv = g2 * cw_ref[0:1, :] + g1 * cw_ref[1:2, :] + g * cw_ref[2:3, :] + cb_ref[...]
        y_ref[r0:r0 + rc, :] = (jax.nn.gelu(conv) * up).astype(BF16)
    if not has_state:
        last = gbuf_ref[pad + tm - keep:pad + tm, :]
        gbuf_ref[pad - keep:pad, :] = last
        so_ref[...] = last


def _ffn_up(h, w_gate, w_up, conv_w, conv_b, l, depth, nseq, seq_len, tm, rc, tn, state, prev):
    rows, k = h.shape
    n = w_gate.shape[2]
    keep = CONV_W - 1
    has_state = state is not None
    assert seq_len & (seq_len - 1) == 0 and seq_len >= CONV_W
    kern = functools.partial(_ffn_up_kernel, tm=tm, rc=rc, seq_len=seq_len, has_state=has_state)
    wsp = pl.BlockSpec((None, k, tn), lambda j, i: (l, 0, j))
    in_specs = [pl.BlockSpec((tm, k), lambda j, i: (i, 0)), wsp, wsp,
                pl.BlockSpec((None, CONV_W, tn), lambda j, i: (l, 0, j)),
                pl.BlockSpec((None, 1, tn), lambda j, i: (l, 0, j))]
    args = [h, w_gate, w_up, conv_w, _vec3(conv_b)]
    if has_state:
        bs = tm // seq_len
        in_specs.append(pl.BlockSpec((None, bs, keep, tn), lambda j, i: (l, i, 0, j)))
        args.append(state)
        so_spec = _stacked_spec(prev, depth, l, (bs, keep, tn), lambda j, i: (i, 0, j))
    else:
        tps = seq_len // tm
        so_spec = _stacked_spec(prev, depth, l, (None, keep, tn), lambda j, i: (i // tps, 0, j))
    kern, cspecs, cargs, aliases = _stacked(kern, len(args), prev, depth, l)
    return pl.pallas_call(
        kern, grid=(n // tn, rows // tm),
        in_specs=in_specs + cspecs,
        out_specs=[pl.BlockSpec((None, tm, tn), lambda j, i: (j, i, 0)), so_spec],
        out_shape=[jax.ShapeDtypeStruct((n // tn, rows, tn), BF16),
                   jax.ShapeDtypeStruct((depth, nseq, keep, n), F32)],
        scratch_shapes=[pltpu.VMEM((k, tn), BF16), pltpu.VMEM((k, tn), BF16),
                        pltpu.VMEM((tm + SUBLANES, tn), F32)],
        input_output_aliases=aliases,
        compiler_params=_params(2), name="ffn_up",
    )(*args, *cargs)


def _layer(l, depth, x, h, mod, w, cfg, s_ret, conv_prev, outs):
    nseq, seq_len, tm, tm_post = cfg["nseq"], cfg["seq_len"], cfg["tm"], cfg["tm_post"]
    ret_out, conv_out, v_out = outs

    proj = _inproj(h, w["w_in"], w["sgu_ln_g"], w["sgu_ln_b"], l, cfg["tm_in"], cfg["rc_in"], seq_len,
                   cfg["pos0"])
    ya, v_out = _sgu(proj, cfg["sgu_w"], cfg["sgu_b"], l, depth, seq_len, tm, cfg["emit_v"], v_out)
    if s_ret is None:
        yb, ret_out = _ret_prompt(proj, w["ret_gn_g"], l, depth, nseq, seq_len, ret_out)
    else:
        yb, ret_out = _ret_sample(proj, w["ret_gn_g"], s_ret, l, seq_len, cfg["ret_bs"], ret_out)
    merged = _merge(ya, yb, proj, w["w_branch_a"], w["w_branch_b"], l, tm, cfg["rc_merge"])
    x, h2 = _matmul_post(merged, w["w_out"], x, w["norm_post1"], mod, l, K_G1, seq_len, cfg["tm_out"],
                         (w["norm_pre2"], l, K_SC2, K_SH2), cfg["rc_out"])

    y, conv_out = _ffn_up(h2, w["ffn_w_gate"], w["ffn_w_up"], w["ffn_conv_w"], w["ffn_conv_b"], l, depth,
                          nseq, seq_len, cfg["tm_ffn"], cfg["rc_ffn"], FFN_TILE, conv_prev, conv_out)
    nxt = None if l == depth - 1 else (w["norm_pre1"], l + 1, K_SC1, K_SH1)
    x, h_next = _matmul_post(y, w["ffn_w_down_bf16"], x, w["norm_post2"], mod, l, K_G2, seq_len, tm_post, nxt,
                             cfg["rc_post"])
    return x, h_next, (ret_out, conv_out, v_out)


def kernel(x_prompt, x_sample, state_ret, state_conv, c_prompt, c_sample, w_ada, b_ada, norm_pre1, norm_post1, norm_pre2, norm_post2, w_in, sgu_w_s, sgu_b_s, sgu_ln_g, sgu_ln_b, ret_gn_g, w_branch_a, w_branch_b, w_out, ffn_w_gate, ffn_w_up, ffn_conv_w, ffn_conv_b, ffn_w_down):
    w = dict(norm_pre1=norm_pre1, norm_post1=norm_post1, norm_pre2=norm_pre2, norm_post2=norm_post2, w_in=w_in,
             sgu_ln_g=sgu_ln_g, sgu_ln_b=sgu_ln_b, ret_gn_g=ret_gn_g, w_branch_a=w_branch_a,
             w_branch_b=w_branch_b, w_out=w_out, ffn_w_gate=ffn_w_gate, ffn_w_up=ffn_w_up,
             ffn_conv_w=ffn_conv_w, ffn_conv_b=ffn_conv_b, ffn_w_down_bf16=_cast_bf16(ffn_w_down))
    depth = w_ada.shape[0]
    nb, seq, d = x_prompt.shape
    ndb, dseq, _ = x_sample.shape

    mod_p, mod_s = _modulation(c_prompt, c_sample, w_ada, b_ada)

    sgu_wp, sgu_bp = _sgu_tables(sgu_w_s, sgu_b_s, seq)
    sgu_ws, sgu_bs = _sgu_tables(sgu_w_s, sgu_b_s, dseq)
    cfg_p = dict(nseq=nb, seq_len=seq, tm=1024, tm_in=2048, rc_in=256, rc_merge=256, tm_ffn=2048, rc_ffn=128,
                 tm_out=512, rc_out=256, tm_post=256, rc_post=256, pos0=0, sgu_w=sgu_wp, sgu_b=sgu_bp,
                 emit_v=False)
    cfg_s = dict(nseq=ndb, seq_len=dseq, tm=1024, tm_in=1024, rc_in=256, rc_merge=256, tm_ffn=1024, rc_ffn=256,
                 tm_out=512, rc_out=256, tm_post=256, rc_post=256, pos0=PAST_LEN, ret_bs=8, sgu_w=sgu_ws,
                 sgu_b=sgu_bs, emit_v=True)

    xp = x_prompt.reshape(nb * seq, d)
    xs = x_sample.reshape(ndb * dseq, d)
    hp = _prenorm(xp, norm_pre1, mod_p, 0, K_SC1, K_SH1, seq, 512)
    hs = _prenorm(xs, norm_pre1, mod_s, 0, K_SC1, K_SH1, dseq, 512)
    outs_p = outs_s = (None, None, None)
    for l in range(depth):
        xp, hp, outs_p = _layer(l, depth, xp, hp, mod_p, w, cfg_p, None, None, outs_p)
        xs, hs, outs_s = _layer(l, depth, xs, hs, mod_s, w, cfg_s, state_ret, state_conv, outs_s)
    ret_p, conv_p, _ = outs_p
    ret_s, conv_s, v_s = outs_s
    return (xp.reshape(nb, seq, d), xs.reshape(ndb, dseq, d), ret_p, ret_s, conv_p, conv_s,
            v_s.reshape(depth, ndb, dseq, -1))
```

```python
import functools
import math

import jax
import jax.numpy as jnp
from jax import lax
from jax.experimental import pallas as pl
from jax.experimental.pallas import tpu as pltpu

F32 = jnp.float32
BF16 = jnp.bfloat16

EPS = 1e-6
ROPE_BASE = 10000.0
PAST_LEN = 16384
SGU_GROUPS = 8
SGU_CHUNK = 128
RET_HEADS = 8
RET_CHUNK = 128
CONV_W = 3

SUBLANES = 8
BF16_ROWS = 16
FFN_TILE = 512
VMEM_LIMIT_BYTES = 56 * 1024 * 1024


def _params(n_axes):
    return pltpu.CompilerParams(dimension_semantics=("arbitrary",) * n_axes,
                                vmem_limit_bytes=VMEM_LIMIT_BYTES)


def _stacked(kernel, n_in, prev, depth, l):
    if prev is None:
        def first(*refs):
            refs = list(refs)
            so = refs[n_in + 1]
            for k in range(depth):
                if k != l:
                    so[k] = jnp.zeros(so.shape[1:], so.dtype)
            refs[n_in + 1] = so.at[l]
            kernel(*refs)
        return first, [], [], {}

    def later(*refs):
        kernel(*refs[:n_in], *refs[n_in + 1:])
    return later, [pl.BlockSpec(memory_space=pl.ANY)], [prev], {n_in: 1}


def _stacked_spec(prev, depth, l, block, index_fn):
    if prev is None:
        return pl.BlockSpec((depth,) + block, lambda *g: (0,) + index_fn(*g))
    return pl.BlockSpec((None,) + block, lambda *g: (l,) + index_fn(*g))


def _rms(x, g):
    return x * lax.rsqrt(jnp.mean(x * x, axis=-1, keepdims=True) + EPS) * g


def _silu(x):
    return x * jax.nn.sigmoid(x)


def _vec3(p):
    return p.reshape(p.shape[0], 1, p.shape[1])


def _mod_kernel(c_ref, w_ref, b_ref, op_ref, os_ref, *, n_p, s0):
    a = _silu(c_ref[...]).astype(BF16)
    m = jnp.dot(a, w_ref[...].astype(BF16), preferred_element_type=F32) + b_ref[...]
    op_ref[...] = m[0:n_p][:, None, :]
    os_ref[...] = m[s0:][:, None, :]


def _modulation(c_prompt, c_sample, w_ada, b_ada, tn=1024):
    depth, d, n = w_ada.shape
    n_p, n_s = c_prompt.shape[0], c_sample.shape[0]
    s0 = -(-n_p // BF16_ROWS) * BF16_ROWS
    c_all = jnp.concatenate([c_prompt, jnp.zeros((s0 - n_p, d), F32), c_sample], axis=0)
    rows = c_all.shape[0]
    assert rows % BF16_ROWS == 0
    return pl.pallas_call(
        functools.partial(_mod_kernel, n_p=n_p, s0=s0),
        grid=(depth, n // tn),
        in_specs=[pl.BlockSpec((rows, d), lambda l, j: (0, 0)),
                  pl.BlockSpec((None, d, tn), lambda l, j: (l, 0, j)),
                  pl.BlockSpec((None, 1, tn), lambda l, j: (l, 0, j))],
        out_specs=[pl.BlockSpec((None, n_p, 1, tn), lambda l, j: (l, 0, 0, j)),
                   pl.BlockSpec((None, n_s, 1, tn), lambda l, j: (l, 0, 0, j))],
        out_shape=[jax.ShapeDtypeStruct((depth, n_p, 1, n), F32),
                   jax.ShapeDtypeStruct((depth, n_s, 1, n), F32)],
        compiler_params=_params(2),
        name="adaln_mod",
    )(c_all, w_ada, b_ada.reshape(depth, 1, n))


K_SH1, K_SC1, K_G1, K_SH2, K_SC2, K_G2 = range(6)


def _mod_spec(l, k, tm, seq_len, d, grid_rank):
    if seq_len >= tm:
        tps = seq_len // tm
        return pl.BlockSpec((None, 1, 1, d), lambda *g: (l, g[grid_rank - 1] // tps, 0, k))
    return pl.BlockSpec((None, tm // seq_len, 1, d), lambda *g: (l, g[grid_rank - 1], 0, k))


def _rows_of(m_ref, r0, nrows, seq_len, fn=lambda m: m):
    d = m_ref.shape[-1]
    if m_ref.shape[0] == 1:
        return fn(m_ref[0])
    nb = nrows // seq_len
    m = fn(m_ref[r0 // seq_len:r0 // seq_len + nb])
    return jnp.broadcast_to(m, (nb, seq_len, d)).reshape(nrows, d)


def _unit_rms(x):
    return x * lax.rsqrt(jnp.mean(x * x, axis=-1, keepdims=True) + EPS)


def _prenorm_kernel(x_ref, g_ref, sc_ref, sh_ref, h_ref, *, seq_len):
    tm = x_ref.shape[0]
    sc = _rows_of(sc_ref, 0, tm, seq_len)
    sh = _rows_of(sh_ref, 0, tm, seq_len)
    h_ref[...] = (_rms(x_ref[...], g_ref[...]) * (1.0 + sc) + sh).astype(BF16)


def _prenorm(x, g_pre, mod, l, k_sc, k_sh, seq_len, tm):
    rows, d = x.shape
    act = pl.BlockSpec((tm, d), lambda i: (i, 0))
    return pl.pallas_call(
        functools.partial(_prenorm_kernel, seq_len=seq_len), grid=(rows // tm,),
        in_specs=[act, pl.BlockSpec((None, 1, d), lambda i: (l, 0, 0)),
                  _mod_spec(l, k_sc, tm, seq_len, d, 1), _mod_spec(l, k_sh, tm, seq_len, d, 1)],
        out_specs=act,
        out_shape=jax.ShapeDtypeStruct((rows, d), BF16),
        compiler_params=_params(1), name="prenorm",
    )(x, _vec3(g_pre), mod, mod)


def _cast_kernel(w_ref, o_ref):
    o_ref[...] = w_ref[...].astype(BF16)


def _cast_bf16(w, steps=4):
    depth, k, n = w.shape
    tk = k // steps
    assert tk * steps == k and tk % BF16_ROWS == 0
    spec = pl.BlockSpec((None, tk, n), lambda l, i: (l, i, 0))
    return pl.pallas_call(
        _cast_kernel, grid=(depth, k // tk), in_specs=[spec], out_specs=spec,
        out_shape=jax.ShapeDtypeStruct(w.shape, BF16),
        compiler_params=_params(2), name="cast_bf16",
    )(w)


def _mm_post_kernel(m_ref, w_ref, x_ref, gpost_ref, gate_ref, *rest, tm, rc, seq_len, has_next, cast):
    rest = list(rest)
    if has_next:
        gpre_ref, sc_ref, sh_ref = rest[:3]
        rest = rest[3:]
    xo_ref = rest.pop(0)
    h_ref = rest.pop(0) if has_next else None
    if cast:
        wb_ref = rest.pop(0)

        @pl.when(pl.program_id(0) == 0)
        def _():
            wb_ref[...] = w_ref[...].astype(BF16)
    else:
        wb_ref = w_ref

    nk, _, tk = m_ref.shape
    for c in range(tm // rc):
        r0 = c * rc
        rows = slice(r0, r0 + rc)
        t = jnp.dot(m_ref[0, rows, :], wb_ref[0:tk, :], preferred_element_type=F32)
        for kk in range(1, nk):
            t += jnp.dot(m_ref[kk, rows, :], wb_ref[kk * tk:(kk + 1) * tk, :], preferred_element_type=F32)
        gain = _rows_of(gate_ref, r0, rc, seq_len, lambda g: g * gpost_ref[...])
        x = x_ref[rows, :] + _unit_rms(t) * gain
        xo_ref[rows, :] = x
        if has_next:
            scale = _rows_of(sc_ref, r0, rc, seq_len, lambda s: (1.0 + s) * gpre_ref[...])
            h = _unit_rms(x) * scale + _rows_of(sh_ref, r0, rc, seq_len)
            h_ref[rows, :] = h.astype(BF16)


def _matmul_post(m, w, x, g_post, mod, l, k_gate, seq_len, tm, nxt, rc):
    nk, rows, tk = m.shape
    k = nk * tk
    d = w.shape[2]
    assert k == w.shape[1]
    cast = w.dtype != BF16
    has_next = nxt is not None
    act = pl.BlockSpec((tm, d), lambda i: (i, 0))

    def vec():
        return pl.BlockSpec((None, 1, d), lambda i: (l, 0, 0))
    in_specs = [pl.BlockSpec((nk, tm, tk), lambda i: (0, i, 0)),
                pl.BlockSpec((None, k, d), lambda i: (l, 0, 0), pipeline_mode=pl.Buffered(1)),
                act, vec(), _mod_spec(l, k_gate, tm, seq_len, d, 1)]
    args = [m, w, x, _vec3(g_post), mod]
    out_specs, out_shape = [act], [jax.ShapeDtypeStruct((rows, d), F32)]
    if has_next:
        g_pre, ln, k_sc, k_sh = nxt
        in_specs += [pl.BlockSpec((None, 1, d), lambda i: (ln, 0, 0)),
                     _mod_spec(ln, k_sc, tm, seq_len, d, 1), _mod_spec(ln, k_sh, tm, seq_len, d, 1)]
        args += [_vec3(g_pre), mod, mod]
        out_specs.append(act)
        out_shape.append(jax.ShapeDtypeStruct((rows, d), BF16))
    kern = functools.partial(_mm_post_kernel, tm=tm, rc=rc, seq_len=seq_len, has_next=has_next, cast=cast)
    out = pl.pallas_call(
        kern, grid=(rows // tm,), in_specs=in_specs, out_specs=out_specs, out_shape=out_shape,
        scratch_shapes=[pltpu.VMEM((k, d), BF16)] if cast else [],
        compiler_params=_params(1), name="matmul_post",
    )(*args)
    return (out[0], out[1]) if has_next else (out[0], None)


IN_TILE = 1024
J_U, J_V, J_Q, J_K, J_VR, J_GR, J_GA, J_GB = 0, 1, 2, 3, 4, 5, 6, 8
IN_GROUP_ROWS = 512


def _inproj_kernel(h_ref, w_ref, lng_ref, lnb_ref, o_ref, wb_ref, *, tm, rc, unroll, seq_len, pos0, head_dim):
    j = pl.program_id(0)
    i = pl.program_id(1)

    @pl.when(i == 0)
    def _():
        wb_ref[...] = w_ref[...].astype(BF16)

    def run(epilogue):
        def group(gi, carry):
            for u in range(unroll):
                r0 = pl.multiple_of((gi * unroll + u) * rc, rc)
                acc = jnp.dot(h_ref[pl.ds(r0, rc), :], wb_ref[...], preferred_element_type=F32)
                o_ref[pl.ds(r0, rc), :] = epilogue(acc, r0).astype(BF16)
            return carry
        lax.fori_loop(0, tm // (rc * unroll), group, 0)

    def layer_norm_gelu(acc, r0):
        v = jax.nn.gelu(acc)
        mu = jnp.mean(v, axis=-1, keepdims=True)
        var = jnp.mean(jnp.square(v - mu), axis=-1, keepdims=True)
        return (v - mu) * lax.rsqrt(var + EPS) * lng_ref[...] + lnb_ref[...]

    def rope(acc, r0):
        half = head_dim // 2
        lane = lax.broadcasted_iota(jnp.int32, (rc, head_dim), 1)
        row = lax.broadcasted_iota(jnp.int32, (rc, head_dim), 0)
        pos = pos0 + ((i * tm + r0 + row) & (seq_len - 1))
        inv = jnp.exp((lane & (half - 1)).astype(F32) * (-math.log(ROPE_BASE) / half))
        ang = pos.astype(F32) * inv
        scale = jnp.where(j == J_K, head_dim ** -0.5, 1.0).astype(F32)
        cos = jnp.cos(ang) * scale
        sin = jnp.where(lane < half, -jnp.sin(ang), jnp.sin(ang)) * scale
        heads = [acc[:, h * head_dim:(h + 1) * head_dim] for h in range(IN_TILE // head_dim)]
        return jnp.concatenate([xh * cos + pltpu.roll(xh, half, axis=1) * sin for xh in heads], axis=1)

    pl.when(j == J_U)(lambda: run(lambda acc, c: jax.nn.gelu(acc)))
    pl.when(j == J_V)(lambda: run(layer_norm_gelu))
    pl.when((j == J_Q) | (j == J_K))(lambda: run(rope))
    pl.when(j == J_VR)(lambda: run(lambda acc, c: acc))
    pl.when(j == J_GR)(lambda: run(lambda acc, c: _silu(acc)))
    pl.when(j >= J_GA)(lambda: run(lambda acc, c: jax.nn.sigmoid(acc)))


def _inproj(h, w_in, ln_g, ln_b, l, tm, rc, seq_len, pos0):
    rows, k = h.shape
    n = w_in.shape[2]
    assert seq_len & (seq_len - 1) == 0 and n == 10 * IN_TILE
    unroll = min(IN_GROUP_ROWS // rc, tm // rc)
    kern = functools.partial(_inproj_kernel, tm=tm, rc=rc, unroll=unroll, seq_len=seq_len, pos0=pos0,
                             head_dim=IN_TILE // RET_HEADS)
    vec = pl.BlockSpec((None, 1, IN_TILE), lambda j, i: (l, 0, 0))
    return pl.pallas_call(
        kern, grid=(n // IN_TILE, rows // tm),
        in_specs=[pl.BlockSpec((tm, k), lambda j, i: (i, 0)),
                  pl.BlockSpec((None, k, IN_TILE), lambda j, i: (l, 0, j)),
                  vec, vec],
        out_specs=pl.BlockSpec((None, tm, IN_TILE), lambda j, i: (j, i, 0)),
        out_shape=jax.ShapeDtypeStruct((n // IN_TILE, rows, IN_TILE), BF16),
        scratch_shapes=[pltpu.VMEM((k, IN_TILE), BF16)],
        compiler_params=_params(2), name="inproj",
    )(h, w_in, _vec3(ln_g), _vec3(ln_b))


def _sgu_kernel(u_ref, v_ref, w_ref, b_ref, o_ref, *vo_ref, tm, chunk):
    if vo_ref:
        vo_ref[0][...] = v_ref[...].astype(F32)
    r = lax.broadcasted_iota(jnp.int32, (SGU_CHUNK, SGU_CHUNK), 0)
    c = lax.broadcasted_iota(jnp.int32, (SGU_CHUNK, SGU_CHUNK), 1)
    mask = (c <= r) & ((r & -chunk) == (c & -chunk))
    gd = IN_TILE // SGU_GROUPS
    for g in range(SGU_GROUPS):
        w = jnp.where(mask, w_ref[g], 0.0).astype(BF16)
        bias = b_ref[:, g:g + 1]
        for t in range(tm // SGU_CHUNK):
            rs = slice(t * SGU_CHUNK, (t + 1) * SGU_CHUNK)
            cs = slice(g * gd, (g + 1) * gd)
            z = jnp.dot(w, v_ref[rs, cs], preferred_element_type=F32) + bias
            o_ref[rs, cs] = (u_ref[rs, cs].astype(F32) * z).astype(BF16)


def _sgu_tables(w_s, b_s, seq_len):
    chunk = min(SGU_CHUNK, seq_len)
    reps = SGU_CHUNK // chunk
    w_t = jnp.tile(w_s[:, :, :chunk, :chunk], (1, 1, reps, reps))
    b_t = jnp.tile(jnp.swapaxes(b_s[:, :, :chunk], 1, 2), (1, reps, 1))
    return w_t, b_t


def _sgu(proj, w_t, b_t, l, depth, seq_len, tm, emit_v, prev):
    rows = proj.shape[1]
    kern = functools.partial(_sgu_kernel, tm=tm, chunk=min(SGU_CHUNK, seq_len))
    in_specs = [pl.BlockSpec((None, tm, IN_TILE), lambda i: (J_U, i, 0)),
                pl.BlockSpec((None, tm, IN_TILE), lambda i: (J_V, i, 0)),
                pl.BlockSpec((None, SGU_GROUPS, SGU_CHUNK, SGU_CHUNK), lambda i: (l, 0, 0, 0)),
                pl.BlockSpec((None, SGU_CHUNK, SGU_GROUPS), lambda i: (l, 0, 0))]
    out_specs = [pl.BlockSpec((tm, IN_TILE), lambda i: (i, 0))]
    out_shape = [jax.ShapeDtypeStruct((rows, IN_TILE), BF16)]
    cspecs, cargs, aliases = [], [], {}
    if emit_v:
        kern, cspecs, cargs, aliases = _stacked(kern, 4, prev, depth, l)
        out_specs.append(_stacked_spec(prev, depth, l, (tm, IN_TILE), lambda i: (i, 0)))
        out_shape.append(jax.ShapeDtypeStruct((depth, rows, IN_TILE), F32))
    out = pl.pallas_call(
        kern, grid=(rows // tm,),
        in_specs=in_specs + cspecs, out_specs=out_specs, out_shape=out_shape,
        input_output_aliases=aliases,
        compiler_params=_params(1), name="sgu",
    )(proj, proj, w_t, b_t, *cargs)
    return (out[0], out[1]) if emit_v else (out[0], None)


RET_CHUNKS_PER_STEP = 2


def _log_gamma(h):
    return math.log(1.0 - 2.0 ** (-5.0 - h))


def _ret_decays(h, c):
    lg = _log_gamma(h)
    ri = lax.broadcasted_iota(jnp.int32, (c, c), 0)
    ci = lax.broadcasted_iota(jnp.int32, (c, c), 1)
    diff = (ri - ci).astype(F32)
    decay = jnp.where(ri >= ci, jnp.exp(lg * jnp.maximum(diff, 0.0)), 0.0)
    idx = lax.broadcasted_iota(jnp.int32, (c, 1), 0).astype(F32)
    return decay, jnp.exp(lg * (idx + 1.0)), jnp.exp(lg * (c - 1.0 - idx))


def _ret_chunk(qh, kh, vh, grh, gnh, s, h, c, decays):
    decay, q_dec, k_dec = decays
    scores = lax.dot_general(qh, kh, (((1,), (1,)), ((), ())), preferred_element_type=F32) * decay
    inner = jnp.dot(scores.astype(BF16), vh, preferred_element_type=F32)
    cross = jnp.dot(qh, s.astype(BF16), preferred_element_type=F32) * q_dec
    kd = (kh.astype(F32) * k_dec).astype(BF16)
    s_new = (s * math.exp(_log_gamma(h) * c)
             + lax.dot_general(kd, vh, (((0,), (0,)), ((), ())), preferred_element_type=F32))
    o = inner + cross
    mu = jnp.mean(o, axis=-1, keepdims=True)
    var = jnp.mean(jnp.square(o - mu), axis=-1, keepdims=True)
    y = (o - mu) * lax.rsqrt(var + EPS) * gnh
    return grh * y, s_new


def _ret_prompt_kernel(q_ref, k_ref, v_ref, gr_ref, gn_ref, o_ref, so_ref, s_ref, dec_ref, *, hd):
    n = pl.program_id(1)

    @pl.when(n == 0)
    def _():
        s_ref[...] = jnp.zeros_like(s_ref)

    @pl.when((n == 0) & (pl.program_id(0) == 0))
    def _():
        for h in range(RET_HEADS):
            dec_ref[h] = _ret_decays(h, RET_CHUNK)[0]

    for h in range(RET_HEADS):
        cs = slice(h * hd, (h + 1) * hd)
        _, q_dec, k_dec = _ret_decays(h, RET_CHUNK)
        s = s_ref[h]
        for ci in range(q_ref.shape[0] // RET_CHUNK):
            rs = slice(ci * RET_CHUNK, (ci + 1) * RET_CHUNK)
            y, s = _ret_chunk(q_ref[rs, cs], k_ref[rs, cs], v_ref[rs, cs], gr_ref[rs, cs].astype(F32),
                              gn_ref[:, cs], s, h, RET_CHUNK, (dec_ref[h], q_dec, k_dec))
            o_ref[rs, cs] = y.astype(BF16)
        s_ref[h] = s

    @pl.when(n == pl.num_programs(1) - 1)
    def _():
        so_ref[...] = s_ref[...]


def _ret_prompt(proj, gn_g, l, depth, nseq, seq_len, prev):
    hd = IN_TILE // RET_HEADS
    tr = RET_CHUNKS_PER_STEP * RET_CHUNK
    nc = seq_len // tr

    def col(jc):
        return pl.BlockSpec((None, tr, IN_TILE), lambda b, n: (jc, b * nc + n, 0))
    kern, cspecs, cargs, aliases = _stacked(functools.partial(_ret_prompt_kernel, hd=hd), 5, prev, depth, l)
    return pl.pallas_call(
        kern, grid=(nseq, nc),
        in_specs=[col(J_Q), col(J_K), col(J_VR), col(J_GR),
                  pl.BlockSpec((None, 1, IN_TILE), lambda b, n: (l, 0, 0))] + cspecs,
        out_specs=[pl.BlockSpec((tr, IN_TILE), lambda b, n: (b * nc + n, 0)),
                   _stacked_spec(prev, depth, l, (None, RET_HEADS, hd, hd), lambda b, n: (b, 0, 0, 0))],
        out_shape=[jax.ShapeDtypeStruct((nseq * seq_len, IN_TILE), BF16),
                   jax.ShapeDtypeStruct((depth, nseq, RET_HEADS, hd, hd), F32)],
        scratch_shapes=[pltpu.VMEM((RET_HEADS, hd, hd), F32),
                        pltpu.VMEM((RET_HEADS, RET_CHUNK, RET_CHUNK), F32)],
        input_output_aliases=aliases,
        compiler_params=_params(2), name="ret_prompt",
    )(proj, proj, proj, proj, _vec3(gn_g), *cargs)


def _ret_sample_kernel(q_ref, k_ref, v_ref, gr_ref, gn_ref, s_ref, o_ref, so_ref, y_ref, *, hd, bs, seq_len):
    q = q_ref[...].astype(F32)
    k = k_ref[...].astype(F32)
    v = v_ref[...].astype(F32)
    gr = gr_ref[...].astype(F32)
    for h in range(RET_HEADS):
        cs = slice(h * hd, (h + 1) * hd)
        decays = _ret_decays(h, seq_len)
        for b in range(bs):
            rs = slice(b * seq_len, (b + 1) * seq_len)
            y, s_new = _ret_chunk(q[rs, cs].astype(BF16), k[rs, cs].astype(BF16), v[rs, cs].astype(BF16),
                                  gr[rs, cs], gn_ref[:, cs], s_ref[b, h], h, seq_len, decays)
            y_ref[rs, cs] = y
            so_ref[b, h] = s_new
    o_ref[...] = y_ref[...].astype(BF16)


def _ret_sample(proj, gn_g, state, l, seq_len, bs, prev):
    depth, nseq = state.shape[:2]
    hd = IN_TILE // RET_HEADS

    def col(jc):
        return pl.BlockSpec((None, bs * seq_len, IN_TILE), lambda i: (jc, i, 0))
    kern, cspecs, cargs, aliases = _stacked(
        functools.partial(_ret_sample_kernel, hd=hd, bs=bs, seq_len=seq_len), 6, prev, depth, l)
    return pl.pallas_call(
        kern, grid=(nseq // bs,),
        in_specs=[col(J_Q), col(J_K), col(J_VR), col(J_GR),
                  pl.BlockSpec((None, 1, IN_TILE), lambda i: (l, 0, 0)),
                  pl.BlockSpec((None, bs, RET_HEADS, hd, hd), lambda i: (l, i, 0, 0, 0))] + cspecs,
        out_specs=[pl.BlockSpec((bs * seq_len, IN_TILE), lambda i: (i, 0)),
                   _stacked_spec(prev, depth, l, (bs, RET_HEADS, hd, hd), lambda i: (i, 0, 0, 0))],
        out_shape=[jax.ShapeDtypeStruct((nseq * seq_len, IN_TILE), BF16),
                   jax.ShapeDtypeStruct(state.shape, F32)],
        scratch_shapes=[pltpu.VMEM((bs * seq_len, IN_TILE), F32)],
        input_output_aliases=aliases,
        compiler_params=_params(1), name="ret_sample",
    )(proj, proj, proj, proj, _vec3(gn_g), state, *cargs)


def _merge_kernel(ya_ref, yb_ref, ga_ref, gb_ref, wa_ref, wb_ref, o_ref, wab_ref, wbb_ref, *, rc):
    @pl.when(pl.program_id(1) == 0)
    def _():
        wab_ref[...] = wa_ref[...].astype(BF16)
        wbb_ref[...] = wb_ref[...].astype(BF16)
    for c in range(o_ref.shape[0] // rc):
        rows = slice(c * rc, (c + 1) * rc)
        a = jnp.dot(ya_ref[rows, :], wab_ref[...], preferred_element_type=F32)
        b = jnp.dot(yb_ref[rows, :], wbb_ref[...], preferred_element_type=F32)
        o_ref[rows, :] = (ga_ref[rows, :].astype(F32) * a + gb_ref[rows, :].astype(F32) * b).astype(BF16)


def _merge(ya, yb, proj, w_a, w_b, l, tm, rc):
    rows, k = ya.shape
    n = w_a.shape[2]
    tn = IN_TILE
    row = pl.BlockSpec((tm, k), lambda j, i: (i, 0))
    wsp = pl.BlockSpec((None, k, tn), lambda j, i: (l, 0, j))
    return pl.pallas_call(
        functools.partial(_merge_kernel, rc=rc), grid=(n // tn, rows // tm),
        in_specs=[row, row,
                  pl.BlockSpec((None, tm, tn), lambda j, i: (J_GA + j, i, 0)),
                  pl.BlockSpec((None, tm, tn), lambda j, i: (J_GB + j, i, 0)),
                  wsp, wsp],
        out_specs=pl.BlockSpec((None, tm, tn), lambda j, i: (j, i, 0)),
        out_shape=jax.ShapeDtypeStruct((n // tn, rows, tn), BF16),
        scratch_shapes=[pltpu.VMEM((k, tn), BF16), pltpu.VMEM((k, tn), BF16)],
        compiler_params=_params(2), name="merge",
    )(ya, yb, proj, proj, w_a, w_b)


def _ffn_up_kernel(h_ref, wg_ref, wu_ref, cw_ref, cb_ref, *rest, tm, rc, seq_len, has_state):
    if has_state:
        st_ref, y_ref, so_ref, wgb_ref, wub_ref, gbuf_ref = rest
    else:
        y_ref, so_ref, wgb_ref, wub_ref, gbuf_ref = rest
    i = pl.program_id(1)
    pad = SUBLANES
    keep = CONV_W - 1

    @pl.when(i == 0)
    def _():
        wgb_ref[...] = wg_ref[...].astype(BF16)
        wub_ref[...] = wu_ref[...].astype(BF16)

    tn = y_ref.shape[1]
    if has_state:
        gbuf_ref[0:pad, :] = jnp.zeros((pad, tn), F32)
    else:
        @pl.when(((i * tm) & (seq_len - 1)) == 0)
        def _():
            gbuf_ref[0:pad, :] = jnp.zeros((pad, tn), F32)
    for c in range(tm // rc):
        r0 = c * rc
        h = h_ref[r0:r0 + rc, :]
        g = jnp.dot(h, wgb_ref[...], preferred_element_type=F32)
        up = jnp.dot(h, wub_ref[...], preferred_element_type=F32)
        gbuf_ref[pad + r0:pad + r0 + rc, :] = g
        g1 = gbuf_ref[pad - 1 + r0:pad - 1 + r0 + rc, :]
        g2 = gbuf_ref[pad - 2 + r0:pad - 2 + r0 + rc, :]
        if has_state:
            bs = rc // seq_len
            st = st_ref[c * bs:(c + 1) * bs]
            s0 = jnp.broadcast_to(st[:, 0:1, :], (bs, seq_len, tn)).reshape(rc, tn)
            s1 = jnp.broadcast_to(st[:, 1:2, :], (bs, seq_len, tn)).reshape(rc, tn)
            t = lax.broadcasted_iota(jnp.int32, (rc, tn), 0) & (seq_len - 1)
            g1 = jnp.where(t == 0, s1, g1)
            g2 = jnp.where(t == 0, s0, jnp.where(t == 1, s1, g2))
            so_ref[c * bs:(c + 1) * bs] = g.reshape(bs, seq_len, tn)[:, seq_len - keep:, :]
        conv = g2 * cw_ref[0:1, :] + g1 * cw_ref[1:2, :] + g * cw_ref[2:3, :] + cb_ref[...]
        y_ref[r0:r0 + rc, :] = (jax.nn.gelu(conv) * up).astype(BF16)
    if not has_state:
        last = gbuf_ref[pad + tm - keep:pad + tm, :]
        gbuf_ref[pad - keep:pad, :] = last
        so_ref[...] = last


def _ffn_up(h, w_gate, w_up, conv_w, conv_b, l, depth, nseq, seq_len, tm, rc, tn, state, prev):
    rows, k = h.shape
    n = w_gate.shape[2]
    keep = CONV_W - 1
    has_state = state is not None
    assert seq_len & (seq_len - 1) == 0 and seq_len >= CONV_W
    kern = functools.partial(_ffn_up_kernel, tm=tm, rc=rc, seq_len=seq_len, has_state=has_state)
    wsp = pl.BlockSpec((None, k, tn), lambda j, i: (l, 0, j))
    in_specs = [pl.BlockSpec((tm, k), lambda j, i: (i, 0)), wsp, wsp,
                pl.BlockSpec((None, CONV_W, tn), lambda j, i: (l, 0, j)),
                pl.BlockSpec((None, 1, tn), lambda j, i: (l, 0, j))]
    args = [h, w_gate, w_up, conv_w, _vec3(conv_b)]
    if has_state:
        bs = tm // seq_len
        in_specs.append(pl.BlockSpec((None, bs, keep, tn), lambda j, i: (l, i, 0, j)))
        args.append(state)
        so_spec = _stacked_spec(prev, depth, l, (bs, keep, tn), lambda j, i: (i, 0, j))
    else:
        tps = seq_len // tm
        so_spec = _stacked_spec(prev, depth, l, (None, keep, tn), lambda j, i: (i // tps, 0, j))
    kern, cspecs, cargs, aliases = _stacked(kern, len(args), prev, depth, l)
    return pl.pallas_call(
        kern, grid=(n // tn, rows // tm),
        in_specs=in_specs + cspecs,
        out_specs=[pl.BlockSpec((None, tm, tn), lambda j, i: (j, i, 0)), so_spec],
        out_shape=[jax.ShapeDtypeStruct((n // tn, rows, tn), BF16),
                   jax.ShapeDtypeStruct((depth, nseq, keep, n), F32)],
        scratch_shapes=[pltpu.VMEM((k, tn), BF16), pltpu.VMEM((k, tn), BF16),
                        pltpu.VMEM((tm + SUBLANES, tn), F32)],
        input_output_aliases=aliases,
        compiler_params=_params(2), name="ffn_up",
    )(*args, *cargs)


def _layer(l, depth, x, h, mod, w, cfg, s_ret, conv_prev, outs):
    nseq, seq_len, tm, tm_post = cfg["nseq"], cfg["seq_len"], cfg["tm"], cfg["tm_post"]
    ret_out, conv_out, v_out = outs

    proj = _inproj(h, w["w_in"], w["sgu_ln_g"], w["sgu_ln_b"], l, cfg["tm_in"], cfg["rc_in"][l], seq_len,
                   cfg["pos0"])
    ya, v_out = _sgu(proj, cfg["sgu_w"], cfg["sgu_b"], l, depth, seq_len, tm, cfg["emit_v"], v_out)
    if s_ret is None:
        yb, ret_out = _ret_prompt(proj, w["ret_gn_g"], l, depth, nseq, seq_len, ret_out)
    else:
        yb, ret_out = _ret_sample(proj, w["ret_gn_g"], s_ret, l, seq_len, cfg["ret_bs"], ret_out)
    merged = _merge(ya, yb, proj, w["w_branch_a"], w["w_branch_b"], l, tm, cfg["rc_merge"][l])
    x, h2 = _matmul_post(merged, w["w_out"], x, w["norm_post1"], mod, l, K_G1, seq_len, cfg["tm_out"],
                         (w["norm_pre2"], l, K_SC2, K_SH2), cfg["rc_out"])

    y, conv_out = _ffn_up(h2, w["ffn_w_gate"], w["ffn_w_up"], w["ffn_conv_w"], w["ffn_conv_b"], l, depth,
                          nseq, seq_len, cfg["tm_ffn"], cfg["rc_ffn"], FFN_TILE, conv_prev, conv_out)
    nxt = None if l == depth - 1 else (w["norm_pre1"], l + 1, K_SC1, K_SH1)
    x, h_next = _matmul_post(y, w["ffn_w_down_bf16"], x, w["norm_post2"], mod, l, K_G2, seq_len, tm_post, nxt,
                             cfg["rc_post"])
    return x, h_next, (ret_out, conv_out, v_out)


def kernel(x_prompt, x_sample, state_ret, state_conv, c_prompt, c_sample, w_ada, b_ada, norm_pre1, norm_post1, norm_pre2, norm_post2, w_in, sgu_w_s, sgu_b_s, sgu_ln_g, sgu_ln_b, ret_gn_g, w_branch_a, w_branch_b, w_out, ffn_w_gate, ffn_w_up, ffn_conv_w, ffn_conv_b, ffn_w_down):
    w = dict(norm_pre1=norm_pre1, norm_post1=norm_post1, norm_pre2=norm_pre2, norm_post2=norm_post2, w_in=w_in,
             sgu_ln_g=sgu_ln_g, sgu_ln_b=sgu_ln_b, ret_gn_g=ret_gn_g, w_branch_a=w_branch_a,
             w_branch_b=w_branch_b, w_out=w_out, ffn_w_gate=ffn_w_gate, ffn_w_up=ffn_w_up,
             ffn_conv_w=ffn_conv_w, ffn_conv_b=ffn_conv_b, ffn_w_down_bf16=_cast_bf16(ffn_w_down))
    depth = w_ada.shape[0]
    nb, seq, d = x_prompt.shape
    ndb, dseq, _ = x_sample.shape

    mod_p, mod_s = _modulation(c_prompt, c_sample, w_ada, b_ada)

    sgu_wp, sgu_bp = _sgu_tables(sgu_w_s, sgu_b_s, seq)
    sgu_ws, sgu_bs = _sgu_tables(sgu_w_s, sgu_b_s, dseq)
    cfg_p = dict(nseq=nb, seq_len=seq, tm=1024, tm_in=2048, rc_in=(512, 256), rc_merge=(256, 512), tm_ffn=2048,
                 rc_ffn=128,
                 tm_out=512, rc_out=256, tm_post=256, rc_post=256, pos0=0, sgu_w=sgu_wp, sgu_b=sgu_bp,
                 emit_v=False)
    cfg_s = dict(nseq=ndb, seq_len=dseq, tm=1024, tm_in=1024, rc_in=(256, 256), rc_merge=(256, 256), tm_ffn=1024,
                 rc_ffn=256,
                 tm_out=512, rc_out=256, tm_post=256, rc_post=256, pos0=PAST_LEN, ret_bs=8, sgu_w=sgu_ws,
                 sgu_b=sgu_bs, emit_v=True)

    xp = x_prompt.reshape(nb * seq, d)
    xs = x_sample.reshape(ndb * dseq, d)
    hp = _prenorm(xp, norm_pre1, mod_p, 0, K_SC1, K_SH1, seq, 512)
    hs = _prenorm(xs, norm_pre1, mod_s, 0, K_SC1, K_SH1, dseq, 512)
    outs_p = outs_s = (None, None, None)
    for l in range(depth):
        xp, hp, outs_p = _layer(l, depth, xp, hp, mod_p, w, cfg_p, None, None, outs_p)
        xs, hs, outs_s = _layer(l, depth, xs, hs, mod_s, w, cfg_s, state_ret, state_conv, outs_s)
    ret_p, conv_p, _ = outs_p
    ret_s, conv_s, v_s = outs_s
    return (xp.reshape(nb, seq, d), xs.reshape(ndb, dseq, d), ret_p, ret_s, conv_p, conv_s,
            v_s.reshape(depth, ndb, dseq, -1))
```

```python
import functools
import math

import jax
import jax.numpy as jnp
from jax import lax
from jax.experimental import pallas as pl
from jax.experimental.pallas import tpu as pltpu

F32 = jnp.float32
BF16 = jnp.bfloat16

EPS = 1e-6
ROPE_BASE = 10000.0
PAST_LEN = 16384
SGU_GROUPS = 8
SGU_CHUNK = 128
RET_HEADS = 8
RET_CHUNK = 128
CONV_W = 3

SUBLANES = 8
BF16_ROWS = 16
FFN_TILE = 512
VMEM_LIMIT_BYTES = 56 * 1024 * 1024


def _params(n_axes):
    return pltpu.CompilerParams(dimension_semantics=("arbitrary",) * n_axes,
                                vmem_limit_bytes=VMEM_LIMIT_BYTES)


def _stacked(kernel, n_in, prev, depth, l):
    if prev is None:
        def first(*refs):
            refs = list(refs)
            so = refs[n_in + 1]
            for k in range(depth):
                if k != l:
                    so[k] = jnp.zeros(so.shape[1:], so.dtype)
            refs[n_in + 1] = so.at[l]
            kernel(*refs)
        return first, [], [], {}

    def later(*refs):
        kernel(*refs[:n_in], *refs[n_in + 1:])
    return later, [pl.BlockSpec(memory_space=pl.ANY)], [prev], {n_in: 1}


def _stacked_spec(prev, depth, l, block, index_fn):
    if prev is None:
        return pl.BlockSpec((depth,) + block, lambda *g: (0,) + index_fn(*g))
    return pl.BlockSpec((None,) + block, lambda *g: (l,) + index_fn(*g))


def _rms(x, g):
    return x * lax.rsqrt(jnp.mean(x * x, axis=-1, keepdims=True) + EPS) * g


def _silu(x):
    return x * jax.nn.sigmoid(x)


def _vec3(p):
    return p.reshape(p.shape[0], 1, p.shape[1])


def _mod_kernel(c_ref, w_ref, b_ref, op_ref, os_ref, *, n_p, s0):
    a = _silu(c_ref[...]).astype(BF16)
    m = jnp.dot(a, w_ref[...].astype(BF16), preferred_element_type=F32) + b_ref[...]
    op_ref[...] = m[0:n_p][:, None, :]
    os_ref[...] = m[s0:][:, None, :]


def _modulation(c_prompt, c_sample, w_ada, b_ada, tn=1024):
    depth, d, n = w_ada.shape
    n_p, n_s = c_prompt.shape[0], c_sample.shape[0]
    s0 = -(-n_p // BF16_ROWS) * BF16_ROWS
    c_all = jnp.concatenate([c_prompt, jnp.zeros((s0 - n_p, d), F32), c_sample], axis=0)
    rows = c_all.shape[0]
    assert rows % BF16_ROWS == 0
    return pl.pallas_call(
        functools.partial(_mod_kernel, n_p=n_p, s0=s0),
        grid=(depth, n // tn),
        in_specs=[pl.BlockSpec((rows, d), lambda l, j: (0, 0)),
                  pl.BlockSpec((None, d, tn), lambda l, j: (l, 0, j)),
                  pl.BlockSpec((None, 1, tn), lambda l, j: (l, 0, j))],
        out_specs=[pl.BlockSpec((None, n_p, 1, tn), lambda l, j: (l, 0, 0, j)),
                   pl.BlockSpec((None, n_s, 1, tn), lambda l, j: (l, 0, 0, j))],
        out_shape=[jax.ShapeDtypeStruct((depth, n_p, 1, n), F32),
                   jax.ShapeDtypeStruct((depth, n_s, 1, n), F32)],
        compiler_params=_params(2),
        name="adaln_mod",
    )(c_all, w_ada, b_ada.reshape(depth, 1, n))


K_SH1, K_SC1, K_G1, K_SH2, K_SC2, K_G2 = range(6)


def _mod_spec(l, k, tm, seq_len, d, grid_rank):
    if seq_len >= tm:
        tps = seq_len // tm
        return pl.BlockSpec((None, 1, 1, d), lambda *g: (l, g[grid_rank - 1] // tps, 0, k))
    return pl.BlockSpec((None, tm // seq_len, 1, d), lambda *g: (l, g[grid_rank - 1], 0, k))


def _rows_of(m_ref, r0, nrows, seq_len, fn=lambda m: m):
    d = m_ref.shape[-1]
    if m_ref.shape[0] == 1:
        return fn(m_ref[0])
    nb = nrows // seq_len
    m = fn(m_ref[r0 // seq_len:r0 // seq_len + nb])
    return jnp.broadcast_to(m, (nb, seq_len, d)).reshape(nrows, d)


def _unit_rms(x):
    return x * lax.rsqrt(jnp.mean(x * x, axis=-1, keepdims=True) + EPS)


def _prenorm_kernel(x_ref, g_ref, sc_ref, sh_ref, h_ref, *, seq_len):
    tm = x_ref.shape[0]
    sc = _rows_of(sc_ref, 0, tm, seq_len)
    sh = _rows_of(sh_ref, 0, tm, seq_len)
    h_ref[...] = (_rms(x_ref[...], g_ref[...]) * (1.0 + sc) + sh).astype(BF16)


def _prenorm(x, g_pre, mod, l, k_sc, k_sh, seq_len, tm):
    rows, d = x.shape
    act = pl.BlockSpec((tm, d), lambda i: (i, 0))
    return pl.pallas_call(
        functools.partial(_prenorm_kernel, seq_len=seq_len), grid=(rows // tm,),
        in_specs=[act, pl.BlockSpec((None, 1, d), lambda i: (l, 0, 0)),
                  _mod_spec(l, k_sc, tm, seq_len, d, 1), _mod_spec(l, k_sh, tm, seq_len, d, 1)],
        out_specs=act,
        out_shape=jax.ShapeDtypeStruct((rows, d), BF16),
        compiler_params=_params(1), name="prenorm",
    )(x, _vec3(g_pre), mod, mod)


def _cast_kernel(w_ref, o_ref):
    o_ref[...] = w_ref[...].astype(BF16)


def _cast_bf16(w, steps=4):
    depth, k, n = w.shape
    tk = k // steps
    assert tk * steps == k and tk % BF16_ROWS == 0
    spec = pl.BlockSpec((None, tk, n), lambda l, i: (l, i, 0))
    return pl.pallas_call(
        _cast_kernel, grid=(depth, k // tk), in_specs=[spec], out_specs=spec,
        out_shape=jax.ShapeDtypeStruct(w.shape, BF16),
        compiler_params=_params(2), name="cast_bf16",
    )(w)


def _mm_post_kernel(m_ref, w_ref, x_ref, gpost_ref, gate_ref, *rest, tm, rc, seq_len, has_next, cast):
    rest = list(rest)
    if has_next:
        gpre_ref, sc_ref, sh_ref = rest[:3]
        rest = rest[3:]
    xo_ref = rest.pop(0)
    h_ref = rest.pop(0) if has_next else None
    if cast:
        wb_ref = rest.pop(0)

        @pl.when(pl.program_id(0) == 0)
        def _():
            wb_ref[...] = w_ref[...].astype(BF16)
    else:
        wb_ref = w_ref

    nk, _, tk = m_ref.shape
    for c in range(tm // rc):
        r0 = c * rc
        rows = slice(r0, r0 + rc)
        t = jnp.dot(m_ref[0, rows, :], wb_ref[0:tk, :], preferred_element_type=F32)
        for kk in range(1, nk):
            t += jnp.dot(m_ref[kk, rows, :], wb_ref[kk * tk:(kk + 1) * tk, :], preferred_element_type=F32)
        gain = _rows_of(gate_ref, r0, rc, seq_len, lambda g: g * gpost_ref[...])
        x = x_ref[rows, :] + _unit_rms(t) * gain
        xo_ref[rows, :] = x
        if has_next:
            scale = _rows_of(sc_ref, r0, rc, seq_len, lambda s: (1.0 + s) * gpre_ref[...])
            h = _unit_rms(x) * scale + _rows_of(sh_ref, r0, rc, seq_len)
            h_ref[rows, :] = h.astype(BF16)


def _matmul_post(m, w, x, g_post, mod, l, k_gate, seq_len, tm, nxt, rc):
    nk, rows, tk = m.shape
    k = nk * tk
    d = w.shape[2]
    assert k == w.shape[1]
    cast = w.dtype != BF16
    has_next = nxt is not None
    act = pl.BlockSpec((tm, d), lambda i: (i, 0))

    def vec():
        return pl.BlockSpec((None, 1, d), lambda i: (l, 0, 0))
    in_specs = [pl.BlockSpec((nk, tm, tk), lambda i: (0, i, 0)),
                pl.BlockSpec((None, k, d), lambda i: (l, 0, 0), pipeline_mode=pl.Buffered(1)),
                act, vec(), _mod_spec(l, k_gate, tm, seq_len, d, 1)]
    args = [m, w, x, _vec3(g_post), mod]
    out_specs, out_shape = [act], [jax.ShapeDtypeStruct((rows, d), F32)]
    if has_next:
        g_pre, ln, k_sc, k_sh = nxt
        in_specs += [pl.BlockSpec((None, 1, d), lambda i: (ln, 0, 0)),
                     _mod_spec(ln, k_sc, tm, seq_len, d, 1), _mod_spec(ln, k_sh, tm, seq_len, d, 1)]
        args += [_vec3(g_pre), mod, mod]
        out_specs.append(act)
        out_shape.append(jax.ShapeDtypeStruct((rows, d), BF16))
    kern = functools.partial(_mm_post_kernel, tm=tm, rc=rc, seq_len=seq_len, has_next=has_next, cast=cast)
    out = pl.pallas_call(
        kern, grid=(rows // tm,), in_specs=in_specs, out_specs=out_specs, out_shape=out_shape,
        scratch_shapes=[pltpu.VMEM((k, d), BF16)] if cast else [],
        compiler_params=_params(1), name="matmul_post",
    )(*args)
    return (out[0], out[1]) if has_next else (out[0], None)


IN_TILE = 1024
J_U, J_V, J_Q, J_K, J_VR, J_GR, J_GA, J_GB = 0, 1, 2, 3, 4, 5, 6, 8
IN_GROUP_ROWS = 512


def _inproj_kernel(h_ref, w_ref, lng_ref, lnb_ref, o_ref, wb_ref, *, tm, rc, unroll, seq_len, pos0, head_dim):
    j = pl.program_id(0)
    i = pl.program_id(1)

    @pl.when(i == 0)
    def _():
        wb_ref[...] = w_ref[...].astype(BF16)

    def run(epilogue):
        def group(gi, carry):
            for u in range(unroll):
                r0 = pl.multiple_of((gi * unroll + u) * rc, rc)
                acc = jnp.dot(h_ref[pl.ds(r0, rc), :], wb_ref[...], preferred_element_type=F32)
                o_ref[pl.ds(r0, rc), :] = epilogue(acc, r0).astype(BF16)
            return carry
        lax.fori_loop(0, tm // (rc * unroll), group, 0)

    def layer_norm_gelu(acc, r0):
        v = jax.nn.gelu(acc)
        mu = jnp.mean(v, axis=-1, keepdims=True)
        var = jnp.mean(jnp.square(v - mu), axis=-1, keepdims=True)
        return (v - mu) * lax.rsqrt(var + EPS) * lng_ref[...] + lnb_ref[...]

    def rope(acc, r0):
        half = head_dim // 2
        lane = lax.broadcasted_iota(jnp.int32, (rc, head_dim), 1)
        row = lax.broadcasted_iota(jnp.int32, (rc, head_dim), 0)
        pos = pos0 + ((i * tm + r0 + row) & (seq_len - 1))
        inv = jnp.exp((lane & (half - 1)).astype(F32) * (-math.log(ROPE_BASE) / half))
        ang = pos.astype(F32) * inv
        scale = jnp.where(j == J_K, head_dim ** -0.5, 1.0).astype(F32)
        cos = jnp.cos(ang) * scale
        sin = jnp.where(lane < half, -jnp.sin(ang), jnp.sin(ang)) * scale
        heads = [acc[:, h * head_dim:(h + 1) * head_dim] for h in range(IN_TILE // head_dim)]
        return jnp.concatenate([xh * cos + pltpu.roll(xh, half, axis=1) * sin for xh in heads], axis=1)

    pl.when(j == J_U)(lambda: run(lambda acc, c: jax.nn.gelu(acc)))
    pl.when(j == J_V)(lambda: run(layer_norm_gelu))
    pl.when((j == J_Q) | (j == J_K))(lambda: run(rope))
    pl.when(j == J_VR)(lambda: run(lambda acc, c: acc))
    pl.when(j == J_GR)(lambda: run(lambda acc, c: _silu(acc)))
    pl.when(j >= J_GA)(lambda: run(lambda acc, c: jax.nn.sigmoid(acc)))


def _inproj(h, w_in, ln_g, ln_b, l, tm, rc, seq_len, pos0):
    rows, k = h.shape
    n = w_in.shape[2]
    assert seq_len & (seq_len - 1) == 0 and n == 10 * IN_TILE
    unroll = min(IN_GROUP_ROWS // rc, tm // rc)
    kern = functools.partial(_inproj_kernel, tm=tm, rc=rc, unroll=unroll, seq_len=seq_len, pos0=pos0,
                             head_dim=IN_TILE // RET_HEADS)
    vec = pl.BlockSpec((None, 1, IN_TILE), lambda j, i: (l, 0, 0))
    return pl.pallas_call(
        kern, grid=(n // IN_TILE, rows // tm),
        in_specs=[pl.BlockSpec((tm, k), lambda j, i: (i, 0)),
                  pl.BlockSpec((None, k, IN_TILE), lambda j, i: (l, 0, j)),
                  vec, vec],
        out_specs=pl.BlockSpec((None, tm, IN_TILE), lambda j, i: (j, i, 0)),
        out_shape=jax.ShapeDtypeStruct((n // IN_TILE, rows, IN_TILE), BF16),
        scratch_shapes=[pltpu.VMEM((k, IN_TILE), BF16)],
        compiler_params=_params(2), name="inproj",
    )(h, w_in, _vec3(ln_g), _vec3(ln_b))


def _sgu_kernel(u_ref, v_ref, w_ref, b_ref, o_ref, *vo_ref, tm, chunk):
    if vo_ref:
        vo_ref[0][...] = v_ref[...].astype(F32)
    r = lax.broadcasted_iota(jnp.int32, (SGU_CHUNK, SGU_CHUNK), 0)
    c = lax.broadcasted_iota(jnp.int32, (SGU_CHUNK, SGU_CHUNK), 1)
    mask = (c <= r) & ((r & -chunk) == (c & -chunk))
    gd = IN_TILE // SGU_GROUPS
    for g in range(SGU_GROUPS):
        w = jnp.where(mask, w_ref[g], 0.0).astype(BF16)
        bias = b_ref[:, g:g + 1]
        for t in range(tm // SGU_CHUNK):
            rs = slice(t * SGU_CHUNK, (t + 1) * SGU_CHUNK)
            cs = slice(g * gd, (g + 1) * gd)
            z = jnp.dot(w, v_ref[rs, cs], preferred_element_type=F32) + bias
            o_ref[rs, cs] = (u_ref[rs, cs].astype(F32) * z).astype(BF16)


def _sgu_tables(w_s, b_s, seq_len):
    chunk = min(SGU_CHUNK, seq_len)
    reps = SGU_CHUNK // chunk
    w_t = jnp.tile(w_s[:, :, :chunk, :chunk], (1, 1, reps, reps))
    b_t = jnp.tile(jnp.swapaxes(b_s[:, :, :chunk], 1, 2), (1, reps, 1))
    return w_t, b_t


def _sgu(proj, w_t, b_t, l, depth, seq_len, tm, emit_v, prev):
    rows = proj.shape[1]
    kern = functools.partial(_sgu_kernel, tm=tm, chunk=min(SGU_CHUNK, seq_len))
    in_specs = [pl.BlockSpec((None, tm, IN_TILE), lambda i: (J_U, i, 0)),
                pl.BlockSpec((None, tm, IN_TILE), lambda i: (J_V, i, 0)),
                pl.BlockSpec((None, SGU_GROUPS, SGU_CHUNK, SGU_CHUNK), lambda i: (l, 0, 0, 0)),
                pl.BlockSpec((None, SGU_CHUNK, SGU_GROUPS), lambda i: (l, 0, 0))]
    out_specs = [pl.BlockSpec((tm, IN_TILE), lambda i: (i, 0))]
    out_shape = [jax.ShapeDtypeStruct((rows, IN_TILE), BF16)]
    cspecs, cargs, aliases = [], [], {}
    if emit_v:
        kern, cspecs, cargs, aliases = _stacked(kern, 4, prev, depth, l)
        out_specs.append(_stacked_spec(prev, depth, l, (tm, IN_TILE), lambda i: (i, 0)))
        out_shape.append(jax.ShapeDtypeStruct((depth, rows, IN_TILE), F32))
    out = pl.pallas_call(
        kern, grid=(rows // tm,),
        in_specs=in_specs + cspecs, out_specs=out_specs, out_shape=out_shape,
        input_output_aliases=aliases,
        compiler_params=_params(1), name="sgu",
    )(proj, proj, w_t, b_t, *cargs)
    return (out[0], out[1]) if emit_v else (out[0], None)


RET_CHUNKS_PER_STEP = 2


def _log_gamma(h):
    return math.log(1.0 - 2.0 ** (-5.0 - h))


def _ret_decays(h, c):
    lg = _log_gamma(h)
    ri = lax.broadcasted_iota(jnp.int32, (c, c), 0)
    ci = lax.broadcasted_iota(jnp.int32, (c, c), 1)
    diff = (ri - ci).astype(F32)
    decay = jnp.where(ri >= ci, jnp.exp(lg * jnp.maximum(diff, 0.0)), 0.0)
    idx = lax.broadcasted_iota(jnp.int32, (c, 1), 0).astype(F32)
    return decay, jnp.exp(lg * (idx + 1.0)), jnp.exp(lg * (c - 1.0 - idx))


def _ret_chunk(qh, kh, vh, grh, gnh, s, h, c, decays):
    decay, q_dec, k_dec = decays
    scores = lax.dot_general(qh, kh, (((1,), (1,)), ((), ())), preferred_element_type=F32) * decay
    inner = jnp.dot(scores.astype(BF16), vh, preferred_element_type=F32)
    cross = jnp.dot(qh, s.astype(BF16), preferred_element_type=F32) * q_dec
    kd = (kh.astype(F32) * k_dec).astype(BF16)
    s_new = (s * math.exp(_log_gamma(h) * c)
             + lax.dot_general(kd, vh, (((0,), (0,)), ((), ())), preferred_element_type=F32))
    o = inner + cross
    mu = jnp.mean(o, axis=-1, keepdims=True)
    var = jnp.mean(jnp.square(o - mu), axis=-1, keepdims=True)
    y = (o - mu) * lax.rsqrt(var + EPS) * gnh
    return grh * y, s_new


def _ret_prompt_kernel(q_ref, k_ref, v_ref, gr_ref, gn_ref, o_ref, so_ref, s_ref, dec_ref, *, hd):
    n = pl.program_id(1)

    @pl.when(n == 0)
    def _():
        s_ref[...] = jnp.zeros_like(s_ref)

    @pl.when((n == 0) & (pl.program_id(0) == 0))
    def _():
        for h in range(RET_HEADS):
            dec_ref[h] = _ret_decays(h, RET_CHUNK)[0]

    for h in range(RET_HEADS):
        cs = slice(h * hd, (h + 1) * hd)
        _, q_dec, k_dec = _ret_decays(h, RET_CHUNK)
        s = s_ref[h]
        for ci in range(q_ref.shape[0] // RET_CHUNK):
            rs = slice(ci * RET_CHUNK, (ci + 1) * RET_CHUNK)
            y, s = _ret_chunk(q_ref[rs, cs], k_ref[rs, cs], v_ref[rs, cs], gr_ref[rs, cs].astype(F32),
                              gn_ref[:, cs], s, h, RET_CHUNK, (dec_ref[h], q_dec, k_dec))
            o_ref[rs, cs] = y.astype(BF16)
        s_ref[h] = s

    @pl.when(n == pl.num_programs(1) - 1)
    def _():
        so_ref[...] = s_ref[...]


def _ret_prompt(proj, gn_g, l, depth, nseq, seq_len, prev):
    hd = IN_TILE // RET_HEADS
    tr = RET_CHUNKS_PER_STEP * RET_CHUNK
    nc = seq_len // tr

    def col(jc):
        return pl.BlockSpec((None, tr, IN_TILE), lambda b, n: (jc, b * nc + n, 0))
    kern, cspecs, cargs, aliases = _stacked(functools.partial(_ret_prompt_kernel, hd=hd), 5, prev, depth, l)
    return pl.pallas_call(
        kern, grid=(nseq, nc),
        in_specs=[col(J_Q), col(J_K), col(J_VR), col(J_GR),
                  pl.BlockSpec((None, 1, IN_TILE), lambda b, n: (l, 0, 0))] + cspecs,
        out_specs=[pl.BlockSpec((tr, IN_TILE), lambda b, n: (b * nc + n, 0)),
                   _stacked_spec(prev, depth, l, (None, RET_HEADS, hd, hd), lambda b, n: (b, 0, 0, 0))],
        out_shape=[jax.ShapeDtypeStruct((nseq * seq_len, IN_TILE), BF16),
                   jax.ShapeDtypeStruct((depth, nseq, RET_HEADS, hd, hd), F32)],
        scratch_shapes=[pltpu.VMEM((RET_HEADS, hd, hd), F32),
                        pltpu.VMEM((RET_HEADS, RET_CHUNK, RET_CHUNK), F32)],
        input_output_aliases=aliases,
        compiler_params=_params(2), name="ret_prompt",
    )(proj, proj, proj, proj, _vec3(gn_g), *cargs)


def _ret_sample_kernel(q_ref, k_ref, v_ref, gr_ref, gn_ref, s_ref, o_ref, so_ref, y_ref, *, hd, bs, seq_len):
    q = q_ref[...].astype(F32)
    k = k_ref[...].astype(F32)
    v = v_ref[...].astype(F32)
    gr = gr_ref[...].astype(F32)
    for h in range(RET_HEADS):
        cs = slice(h * hd, (h + 1) * hd)
        decays = _ret_decays(h, seq_len)
        for b in range(bs):
            rs = slice(b * seq_len, (b + 1) * seq_len)
            y, s_new = _ret_chunk(q[rs, cs].astype(BF16), k[rs, cs].astype(BF16), v[rs, cs].astype(BF16),
                                  gr[rs, cs], gn_ref[:, cs], s_ref[b, h], h, seq_len, decays)
            y_ref[rs, cs] = y
            so_ref[b, h] = s_new
    o_ref[...] = y_ref[...].astype(BF16)


def _ret_sample(proj, gn_g, state, l, seq_len, bs, prev):
    depth, nseq = state.shape[:2]
    hd = IN_TILE // RET_HEADS

    def col(jc):
        return pl.BlockSpec((None, bs * seq_len, IN_TILE), lambda i: (jc, i, 0))
    kern, cspecs, cargs, aliases = _stacked(
        functools.partial(_ret_sample_kernel, hd=hd, bs=bs, seq_len=seq_len), 6, prev, depth, l)
    return pl.pallas_call(
        kern, grid=(nseq // bs,),
        in_specs=[col(J_Q), col(J_K), col(J_VR), col(J_GR),
                  pl.BlockSpec((None, 1, IN_TILE), lambda i: (l, 0, 0)),
                  pl.BlockSpec((None, bs, RET_HEADS, hd, hd), lambda i: (l, i, 0, 0, 0))] + cspecs,
        out_specs=[pl.BlockSpec((bs * seq_len, IN_TILE), lambda i: (i, 0)),
                   _stacked_spec(prev, depth, l, (bs, RET_HEADS, hd, hd), lambda i: (i, 0, 0, 0))],
        out_shape=[jax.ShapeDtypeStruct((nseq * seq_len, IN_TILE), BF16),
                   jax.ShapeDtypeStruct(state.shape, F32)],
        scratch_shapes=[pltpu.VMEM((bs * seq_len, IN_TILE), F32)],
        input_output_aliases=aliases,
        compiler_params=_params(1), name="ret_sample",
    )(proj, proj, proj, proj, _vec3(gn_g), state, *cargs)


def _merge_kernel(ya_ref, yb_ref, ga_ref, gb_ref, wa_ref, wb_ref, o_ref, wab_ref, wbb_ref, *, rc):
    @pl.when(pl.program_id(1) == 0)
    def _():
        wab_ref[...] = wa_ref[...].astype(BF16)
        wbb_ref[...] = wb_ref[...].astype(BF16)
    for c in range(o_ref.shape[0] // rc):
        rows = slice(c * rc, (c + 1) * rc)
        a = jnp.dot(ya_ref[rows, :], wab_ref[...], preferred_element_type=F32)
        b = jnp.dot(yb_ref[rows, :], wbb_ref[...], preferred_element_type=F32)
        o_ref[rows, :] = (ga_ref[rows, :].astype(F32) * a + gb_ref[rows, :].astype(F32) * b).astype(BF16)


def _merge(ya, yb, proj, w_a, w_b, l, tm, rc):
    rows, k = ya.shape
    n = w_a.shape[2]
    tn = IN_TILE
    row = pl.BlockSpec((tm, k), lambda j, i: (i, 0))
    wsp = pl.BlockSpec((None, k, tn), lambda j, i: (l, 0, j))
    return pl.pallas_call(
        functools.partial(_merge_kernel, rc=rc), grid=(n // tn, rows // tm),
        in_specs=[row, row,
                  pl.BlockSpec((None, tm, tn), lambda j, i: (J_GA + j, i, 0)),
                  pl.BlockSpec((None, tm, tn), lambda j, i: (J_GB + j, i, 0)),
                  wsp, wsp],
        out_specs=pl.BlockSpec((None, tm, tn), lambda j, i: (j, i, 0)),
        out_shape=jax.ShapeDtypeStruct((n // tn, rows, tn), BF16),
        scratch_shapes=[pltpu.VMEM((k, tn), BF16), pltpu.VMEM((k, tn), BF16)],
        compiler_params=_params(2), name="merge",
    )(ya, yb, proj, proj, w_a, w_b)


def _ffn_up_kernel(h_ref, wg_ref, wu_ref, cw_ref, cb_ref, *rest, tm, rc, seq_len, has_state):
    if has_state:
        st_ref, y_ref, so_ref, wgb_ref, wub_ref, gbuf_ref = rest
    else:
        y_ref, so_ref, wgb_ref, wub_ref, gbuf_ref = rest
    i = pl.program_id(1)
    pad = SUBLANES
    keep = CONV_W - 1

    @pl.when(i == 0)
    def _():
        wgb_ref[...] = wg_ref[...].astype(BF16)
        wub_ref[...] = wu_ref[...].astype(BF16)

    tn = y_ref.shape[1]
    if has_state:
        gbuf_ref[0:pad, :] = jnp.zeros((pad, tn), F32)
    else:
        @pl.when(((i * tm) & (seq_len - 1)) == 0)
        def _():
            gbuf_ref[0:pad, :] = jnp.zeros((pad, tn), F32)
    for c in range(tm // rc):
        r0 = c * rc
        h = h_ref[r0:r0 + rc, :]
        g = jnp.dot(h, wgb_ref[...], preferred_element_type=F32)
        up = jnp.dot(h, wub_ref[...], preferred_element_type=F32)
        gbuf_ref[pad + r0:pad + r0 + rc, :] = g
        g1 = gbuf_ref[pad - 1 + r0:pad - 1 + r0 + rc, :]
        g2 = gbuf_ref[pad - 2 + r0:pad - 2 + r0 + rc, :]
        if has_state:
            bs = rc // seq_len
            st = st_ref[c * bs:(c + 1) * bs]
            s0 = jnp.broadcast_to(st[:, 0:1, :], (bs, seq_len, tn)).reshape(rc, tn)
            s1 = jnp.broadcast_to(st[:, 1:2, :], (bs, seq_len, tn)).reshape(rc, tn)
            t = lax.broadcasted_iota(jnp.int32, (rc, tn), 0) & (seq_len - 1)
            g1 = jnp.where(t == 0, s1, g1)
            g2 = jnp.where(t == 0, s0, jnp.where(t == 1, s1, g2))
            so_ref[c * bs:(c + 1) * bs] = g.reshape(bs, seq_len, tn)[:, seq_len - keep:, :]
        conv = g2 * cw_ref[0:1, :] + g1 * cw_ref[1:2, :] + g * cw_ref[2:3, :] + cb_ref[...]
        y_ref[r0:r0 + rc, :] = (jax.nn.gelu(conv) * up).astype(BF16)
    if not has_state:
        last = gbuf_ref[pad + tm - keep:pad + tm, :]
        gbuf_ref[pad - keep:pad, :] = last
        so_ref[...] = last


def _ffn_up(h, w_gate, w_up, conv_w, conv_b, l, depth, nseq, seq_len, tm, rc, tn, state, prev):
    rows, k = h.shape
    n = w_gate.shape[2]
    keep = CONV_W - 1
    has_state = state is not None
    assert seq_len & (seq_len - 1) == 0 and seq_len >= CONV_W
    kern = functools.partial(_ffn_up_kernel, tm=tm, rc=rc, seq_len=seq_len, has_state=has_state)
    wsp = pl.BlockSpec((None, k, tn), lambda j, i: (l, 0, j))
    in_specs = [pl.BlockSpec((tm, k), lambda j, i: (i, 0)), wsp, wsp,
                pl.BlockSpec((None, CONV_W, tn), lambda j, i: (l, 0, j)),
                pl.BlockSpec((None, 1, tn), lambda j, i: (l, 0, j))]
    args = [h, w_gate, w_up, conv_w, _vec3(conv_b)]
    if has_state:
        bs = tm // seq_len
        in_specs.append(pl.BlockSpec((None, bs, keep, tn), lambda j, i: (l, i, 0, j)))
        args.append(state)
        so_spec = _stacked_spec(prev, depth, l, (bs, keep, tn), lambda j, i: (i, 0, j))
    else:
        tps = seq_len // tm
        so_spec = _stacked_spec(prev, depth, l, (None, keep, tn), lambda j, i: (i // tps, 0, j))
    kern, cspecs, cargs, aliases = _stacked(kern, len(args), prev, depth, l)
    return pl.pallas_call(
        kern, grid=(n // tn, rows // tm),
        in_specs=in_specs + cspecs,
        out_specs=[pl.BlockSpec((None, tm, tn), lambda j, i: (j, i, 0)), so_spec],
        out_shape=[jax.ShapeDtypeStruct((n // tn, rows, tn), BF16),
                   jax.ShapeDtypeStruct((depth, nseq, keep, n), F32)],
        scratch_shapes=[pltpu.VMEM((k, tn), BF16), pltpu.VMEM((k, tn), BF16),
                        pltpu.VMEM((tm + SUBLANES, tn), F32)],
        input_output_aliases=aliases,
        compiler_params=_params(2), name="ffn_up",
    )(*args, *cargs)


def _layer(l, depth, x, h, mod, w, cfg, s_ret, conv_prev, outs):
    nseq, seq_len, tm, tm_post = cfg["nseq"], cfg["seq_len"], cfg["tm"], cfg["tm_post"]
    ret_out, conv_out, v_out = outs

    proj = _inproj(h, w["w_in"], w["sgu_ln_g"], w["sgu_ln_b"], l, cfg["tm_in"], cfg["rc_in"], seq_len,
                   cfg["pos0"])
    ya, v_out = _sgu(proj, cfg["sgu_w"], cfg["sgu_b"], l, depth, seq_len, tm, cfg["emit_v"], v_out)
    if s_ret is None:
        yb, ret_out = _ret_prompt(proj, w["ret_gn_g"], l, depth, nseq, seq_len, ret_out)
    else:
        yb, ret_out = _ret_sample(proj, w["ret_gn_g"], s_ret, l, seq_len, cfg["ret_bs"], ret_out)
    merged = _merge(ya, yb, proj, w["w_branch_a"], w["w_branch_b"], l, tm, cfg["rc_merge"])
    x, h2 = _matmul_post(merged, w["w_out"], x, w["norm_post1"], mod, l, K_G1, seq_len, cfg["tm_out"],
                         (w["norm_pre2"], l, K_SC2, K_SH2), cfg["rc_out"])

    y, conv_out = _ffn_up(h2, w["ffn_w_gate"], w["ffn_w_up"], w["ffn_conv_w"], w["ffn_conv_b"], l, depth,
                          nseq, seq_len, cfg["tm_ffn"], cfg["rc_ffn"], FFN_TILE, conv_prev, conv_out)
    nxt = None if l == depth - 1 else (w["norm_pre1"], l + 1, K_SC1, K_SH1)
    x, h_next = _matmul_post(y, w["ffn_w_down_bf16"], x, w["norm_post2"], mod, l, K_G2, seq_len, tm_post, nxt,
                             cfg["rc_post"])
    return x, h_next, (ret_out, conv_out, v_out)


def kernel(x_prompt, x_sample, state_ret, state_conv, c_prompt, c_sample, w_ada, b_ada, norm_pre1, norm_post1, norm_pre2, norm_post2, w_in, sgu_w_s, sgu_b_s, sgu_ln_g, sgu_ln_b, ret_gn_g, w_branch_a, w_branch_b, w_out, ffn_w_gate, ffn_w_up, ffn_conv_w, ffn_conv_b, ffn_w_down):
    w = dict(norm_pre1=norm_pre1, norm_post1=norm_post1, norm_pre2=norm_pre2, norm_post2=norm_post2, w_in=w_in,
             sgu_ln_g=sgu_ln_g, sgu_ln_b=sgu_ln_b, ret_gn_g=ret_gn_g, w_branch_a=w_branch_a,
             w_branch_b=w_branch_b, w_out=w_out, ffn_w_gate=ffn_w_gate, ffn_w_up=ffn_w_up,
             ffn_conv_w=ffn_conv_w, ffn_conv_b=ffn_conv_b, ffn_w_down_bf16=_cast_bf16(ffn_w_down))
    depth = w_ada.shape[0]
    nb, seq, d = x_prompt.shape
    ndb, dseq, _ = x_sample.shape

    mod_p, mod_s = _modulation(c_prompt, c_sample, w_ada, b_ada)

    sgu_wp, sgu_bp = _sgu_tables(sgu_w_s, sgu_b_s, seq)
    sgu_ws, sgu_bs = _sgu_tables(sgu_w_s, sgu_b_s, dseq)
    cfg_p = dict(nseq=nb, seq_len=seq, tm=1024, tm_in=2048, rc_in=256, rc_merge=512, tm_ffn=2048, rc_ffn=128,
                 tm_out=512, rc_out=256, tm_post=256, rc_post=256, pos0=0, sgu_w=sgu_wp, sgu_b=sgu_bp,
                 emit_v=False)
    cfg_s = dict(nseq=ndb, seq_len=dseq, tm=1024, tm_in=1024, rc_in=256, rc_merge=256, tm_ffn=1024, rc_ffn=256,
                 tm_out=512, rc_out=256, tm_post=256, rc_post=256, pos0=PAST_LEN, ret_bs=8, sgu_w=sgu_ws,
                 sgu_b=sgu_bs, emit_v=True)

    xp = x_prompt.reshape(nb * seq, d)
    xs = x_sample.reshape(ndb * dseq, d)
    hp = _prenorm(xp, norm_pre1, mod_p, 0, K_SC1, K_SH1, seq, 512)
    hs = _prenorm(xs, norm_pre1, mod_s, 0, K_SC1, K_SH1, dseq, 512)
    outs_p = outs_s = (None, None, None)
    for l in range(depth):
        xp, hp, outs_p = _layer(l, depth, xp, hp, mod_p, w, cfg_p, None, None, outs_p)
        xs, hs, outs_s = _layer(l, depth, xs, hs, mod_s, w, cfg_s, state_ret, state_conv, outs_s)
    ret_p, conv_p, _ = outs_p
    ret_s, conv_s, v_s = outs_s
    return (xp.reshape(nb, seq, d), xs.reshape(ndb, dseq, d), ret_p, ret_s, conv_p, conv_s,
            v_s.reshape(depth, ndb, dseq, -1))
```

```python
import functools
import math

import jax
import jax.numpy as jnp
from jax import lax
from jax.experimental import pallas as pl
from jax.experimental.pallas import tpu as pltpu

F32 = jnp.float32
BF16 = jnp.bfloat16

EPS = 1e-6
ROPE_BASE = 10000.0
PAST_LEN = 16384
SGU_GROUPS = 8
SGU_CHUNK = 128
RET_HEADS = 8
RET_CHUNK = 128
CONV_W = 3

SUBLANES = 8
BF16_ROWS = 16
FFN_TILE = 512
VMEM_LIMIT_BYTES = 56 * 1024 * 1024


def _params(n_axes):
    return pltpu.CompilerParams(dimension_semantics=("arbitrary",) * n_axes,
                                vmem_limit_bytes=VMEM_LIMIT_BYTES)


def _stacked(kernel, n_in, prev, depth, l):
    if prev is None:
        def first(*refs):
            refs = list(refs)
            so = refs[n_in + 1]
            for k in range(depth):
                if k != l:
                    so[k] = jnp.zeros(so.shape[1:], so.dtype)
            refs[n_in + 1] = so.at[l]
            kernel(*refs)
        return first, [], [], {}

    def later(*refs):
        kernel(*refs[:n_in], *refs[n_in + 1:])
    return later, [pl.BlockSpec(memory_space=pl.ANY)], [prev], {n_in: 1}


def _stacked_spec(prev, depth, l, block, index_fn):
    if prev is None:
        return pl.BlockSpec((depth,) + block, lambda *g: (0,) + index_fn(*g))
    return pl.BlockSpec((None,) + block, lambda *g: (l,) + index_fn(*g))


def _rms(x, g):
    return x * lax.rsqrt(jnp.mean(x * x, axis=-1, keepdims=True) + EPS) * g


def _silu(x):
    return x * jax.nn.sigmoid(x)


def _vec3(p):
    return p.reshape(p.shape[0], 1, p.shape[1])


def _mod_kernel(c_ref, w_ref, b_ref, op_ref, os_ref, *, n_p, s0):
    a = _silu(c_ref[...]).astype(BF16)
    m = jnp.dot(a, w_ref[...].astype(BF16), preferred_element_type=F32) + b_ref[...]
    op_ref[...] = m[0:n_p][:, None, :]
    os_ref[...] = m[s0:][:, None, :]


def _modulation(c_prompt, c_sample, w_ada, b_ada, tn=1024):
    depth, d, n = w_ada.shape
    n_p, n_s = c_prompt.shape[0], c_sample.shape[0]
    s0 = -(-n_p // BF16_ROWS) * BF16_ROWS
    c_all = jnp.concatenate([c_prompt, jnp.zeros((s0 - n_p, d), F32), c_sample], axis=0)
    rows = c_all.shape[0]
    assert rows % BF16_ROWS == 0
    return pl.pallas_call(
        functools.partial(_mod_kernel, n_p=n_p, s0=s0),
        grid=(depth, n // tn),
        in_specs=[pl.BlockSpec((rows, d), lambda l, j: (0, 0)),
                  pl.BlockSpec((None, d, tn), lambda l, j: (l, 0, j)),
                  pl.BlockSpec((None, 1, tn), lambda l, j: (l, 0, j))],
        out_specs=[pl.BlockSpec((None, n_p, 1, tn), lambda l, j: (l, 0, 0, j)),
                   pl.BlockSpec((None, n_s, 1, tn), lambda l, j: (l, 0, 0, j))],
        out_shape=[jax.ShapeDtypeStruct((depth, n_p, 1, n), F32),
                   jax.ShapeDtypeStruct((depth, n_s, 1, n), F32)],
        compiler_params=_params(2),
        name="adaln_mod",
    )(c_all, w_ada, b_ada.reshape(depth, 1, n))


K_SH1, K_SC1, K_G1, K_SH2, K_SC2, K_G2 = range(6)


def _mod_spec(l, k, tm, seq_len, d, grid_rank):
    if seq_len >= tm:
        tps = seq_len // tm
        return pl.BlockSpec((None, 1, 1, d), lambda *g: (l, g[grid_rank - 1] // tps, 0, k))
    return pl.BlockSpec((None, tm // seq_len, 1, d), lambda *g: (l, g[grid_rank - 1], 0, k))


def _rows_of(m_ref, r0, nrows, seq_len, fn=lambda m: m):
    d = m_ref.shape[-1]
    if m_ref.shape[0] == 1:
        return fn(m_ref[0])
    nb = nrows // seq_len
    m = fn(m_ref[r0 // seq_len:r0 // seq_len + nb])
    return jnp.broadcast_to(m, (nb, seq_len, d)).reshape(nrows, d)


def _unit_rms(x):
    return x * lax.rsqrt(jnp.mean(x * x, axis=-1, keepdims=True) + EPS)


def _prenorm_kernel(x_ref, g_ref, sc_ref, sh_ref, h_ref, *, seq_len):
    tm = x_ref.shape[0]
    sc = _rows_of(sc_ref, 0, tm, seq_len)
    sh = _rows_of(sh_ref, 0, tm, seq_len)
    h_ref[...] = (_rms(x_ref[...], g_ref[...]) * (1.0 + sc) + sh).astype(BF16)


def _prenorm(x, g_pre, mod, l, k_sc, k_sh, seq_len, tm):
    rows, d = x.shape
    act = pl.BlockSpec((tm, d), lambda i: (i, 0))
    return pl.pallas_call(
        functools.partial(_prenorm_kernel, seq_len=seq_len), grid=(rows // tm,),
        in_specs=[act, pl.BlockSpec((None, 1, d), lambda i: (l, 0, 0)),
                  _mod_spec(l, k_sc, tm, seq_len, d, 1), _mod_spec(l, k_sh, tm, seq_len, d, 1)],
        out_specs=act,
        out_shape=jax.ShapeDtypeStruct((rows, d), BF16),
        compiler_params=_params(1), name="prenorm",
    )(x, _vec3(g_pre), mod, mod)


def _cast_kernel(w_ref, o_ref):
    o_ref[...] = w_ref[...].astype(BF16)


def _cast_bf16(w, steps=4):
    depth, k, n = w.shape
    tk = k // steps
    assert tk * steps == k and tk % BF16_ROWS == 0
    spec = pl.BlockSpec((None, tk, n), lambda l, i: (l, i, 0))
    return pl.pallas_call(
        _cast_kernel, grid=(depth, k // tk), in_specs=[spec], out_specs=spec,
        out_shape=jax.ShapeDtypeStruct(w.shape, BF16),
        compiler_params=_params(2), name="cast_bf16",
    )(w)


def _mm_post_kernel(m_ref, w_ref, x_ref, gpost_ref, gate_ref, *rest, tm, rc, seq_len, has_next, cast):
    rest = list(rest)
    if has_next:
        gpre_ref, sc_ref, sh_ref = rest[:3]
        rest = rest[3:]
    xo_ref = rest.pop(0)
    h_ref = rest.pop(0) if has_next else None
    if cast:
        wb_ref = rest.pop(0)

        @pl.when(pl.program_id(0) == 0)
        def _():
            wb_ref[...] = w_ref[...].astype(BF16)
    else:
        wb_ref = w_ref

    nk, _, tk = m_ref.shape
    for c in range(tm // rc):
        r0 = c * rc
        rows = slice(r0, r0 + rc)
        t = jnp.dot(m_ref[0, rows, :], wb_ref[0:tk, :], preferred_element_type=F32)
        for kk in range(1, nk):
            t += jnp.dot(m_ref[kk, rows, :], wb_ref[kk * tk:(kk + 1) * tk, :], preferred_element_type=F32)
        gain = _rows_of(gate_ref, r0, rc, seq_len, lambda g: g * gpost_ref[...])
        x = x_ref[rows, :] + _unit_rms(t) * gain
        xo_ref[rows, :] = x
        if has_next:
            scale = _rows_of(sc_ref, r0, rc, seq_len, lambda s: (1.0 + s) * gpre_ref[...])
            h = _unit_rms(x) * scale + _rows_of(sh_ref, r0, rc, seq_len)
            h_ref[rows, :] = h.astype(BF16)


def _matmul_post(m, w, x, g_post, mod, l, k_gate, seq_len, tm, nxt, rc):
    nk, rows, tk = m.shape
    k = nk * tk
    d = w.shape[2]
    assert k == w.shape[1]
    cast = w.dtype != BF16
    has_next = nxt is not None
    act = pl.BlockSpec((tm, d), lambda i: (i, 0))

    def vec():
        return pl.BlockSpec((None, 1, d), lambda i: (l, 0, 0))
    in_specs = [pl.BlockSpec((nk, tm, tk), lambda i: (0, i, 0)),
                pl.BlockSpec((None, k, d), lambda i: (l, 0, 0), pipeline_mode=pl.Buffered(1)),
                act, vec(), _mod_spec(l, k_gate, tm, seq_len, d, 1)]
    args = [m, w, x, _vec3(g_post), mod]
    out_specs, out_shape = [act], [jax.ShapeDtypeStruct((rows, d), F32)]
    if has_next:
        g_pre, ln, k_sc, k_sh = nxt
        in_specs += [pl.BlockSpec((None, 1, d), lambda i: (ln, 0, 0)),
                     _mod_spec(ln, k_sc, tm, seq_len, d, 1), _mod_spec(ln, k_sh, tm, seq_len, d, 1)]
        args += [_vec3(g_pre), mod, mod]
        out_specs.append(act)
        out_shape.append(jax.ShapeDtypeStruct((rows, d), BF16))
    kern = functools.partial(_mm_post_kernel, tm=tm, rc=rc, seq_len=seq_len, has_next=has_next, cast=cast)
    out = pl.pallas_call(
        kern, grid=(rows // tm,), in_specs=in_specs, out_specs=out_specs, out_shape=out_shape,
        scratch_shapes=[pltpu.VMEM((k, d), BF16)] if cast else [],
        compiler_params=_params(1), name="matmul_post",
    )(*args)
    return (out[0], out[1]) if has_next else (out[0], None)


IN_TILE = 1024
J_U, J_V, J_Q, J_K, J_VR, J_GR, J_GA, J_GB = 0, 1, 2, 3, 4, 5, 6, 8
IN_GROUP_ROWS = 512


def _inproj_kernel(h_ref, w_ref, lng_ref, lnb_ref, o_ref, wb_ref, *, tm, rc, unroll, seq_len, pos0, head_dim):
    j = pl.program_id(0)
    i = pl.program_id(1)

    @pl.when(i == 0)
    def _():
        wb_ref[...] = w_ref[...].astype(BF16)

    def run(epilogue):
        def group(gi, carry):
            for u in range(unroll):
                r0 = pl.multiple_of((gi * unroll + u) * rc, rc)
                acc = jnp.dot(h_ref[pl.ds(r0, rc), :], wb_ref[...], preferred_element_type=F32)
                o_ref[pl.ds(r0, rc), :] = epilogue(acc, r0).astype(BF16)
            return carry
        lax.fori_loop(0, tm // (rc * unroll), group, 0)

    def layer_norm_gelu(acc, r0):
        v = jax.nn.gelu(acc)
        mu = jnp.mean(v, axis=-1, keepdims=True)
        var = jnp.mean(jnp.square(v - mu), axis=-1, keepdims=True)
        return (v - mu) * lax.rsqrt(var + EPS) * lng_ref[...] + lnb_ref[...]

    def rope(acc, r0):
        half = head_dim // 2
        lane = lax.broadcasted_iota(jnp.int32, (rc, head_dim), 1)
        row = lax.broadcasted_iota(jnp.int32, (rc, head_dim), 0)
        pos = pos0 + ((i * tm + r0 + row) & (seq_len - 1))
        inv = jnp.exp((lane & (half - 1)).astype(F32) * (-math.log(ROPE_BASE) / half))
        ang = pos.astype(F32) * inv
        scale = jnp.where(j == J_K, head_dim ** -0.5, 1.0).astype(F32)
        cos = jnp.cos(ang) * scale
        sin = jnp.where(lane < half, -jnp.sin(ang), jnp.sin(ang)) * scale
        heads = [acc[:, h * head_dim:(h + 1) * head_dim] for h in range(IN_TILE // head_dim)]
        return jnp.concatenate([xh * cos + pltpu.roll(xh, half, axis=1) * sin for xh in heads], axis=1)

    pl.when(j == J_U)(lambda: run(lambda acc, c: jax.nn.gelu(acc)))
    pl.when(j == J_V)(lambda: run(layer_norm_gelu))
    pl.when((j == J_Q) | (j == J_K))(lambda: run(rope))
    pl.when(j == J_VR)(lambda: run(lambda acc, c: acc))
    pl.when(j == J_GR)(lambda: run(lambda acc, c: _silu(acc)))
    pl.when(j >= J_GA)(lambda: run(lambda acc, c: jax.nn.sigmoid(acc)))


def _inproj(h, w_in, ln_g, ln_b, l, tm, rc, seq_len, pos0):
    rows, k = h.shape
    n = w_in.shape[2]
    assert seq_len & (seq_len - 1) == 0 and n == 10 * IN_TILE
    unroll = min(IN_GROUP_ROWS // rc, tm // rc)
    kern = functools.partial(_inproj_kernel, tm=tm, rc=rc, unroll=unroll, seq_len=seq_len, pos0=pos0,
                             head_dim=IN_TILE // RET_HEADS)
    vec = pl.BlockSpec((None, 1, IN_TILE), lambda j, i: (l, 0, 0))
    return pl.pallas_call(
        kern, grid=(n // IN_TILE, rows // tm),
        in_specs=[pl.BlockSpec((tm, k), lambda j, i: (i, 0)),
                  pl.BlockSpec((None, k, IN_TILE), lambda j, i: (l, 0, j)),
                  vec, vec],
        out_specs=pl.BlockSpec((None, tm, IN_TILE), lambda j, i: (j, i, 0)),
        out_shape=jax.ShapeDtypeStruct((n // IN_TILE, rows, IN_TILE), BF16),
        scratch_shapes=[pltpu.VMEM((k, IN_TILE), BF16)],
        compiler_params=_params(2), name="inproj",
    )(h, w_in, _vec3(ln_g), _vec3(ln_b))


def _sgu_kernel(u_ref, v_ref, w_ref, b_ref, o_ref, *vo_ref, tm, chunk):
    if vo_ref:
        vo_ref[0][...] = v_ref[...].astype(F32)
    r = lax.broadcasted_iota(jnp.int32, (SGU_CHUNK, SGU_CHUNK), 0)
    c = lax.broadcasted_iota(jnp.int32, (SGU_CHUNK, SGU_CHUNK), 1)
    mask = (c <= r) & ((r & -chunk) == (c & -chunk))
    gd = IN_TILE // SGU_GROUPS
    for g in range(SGU_GROUPS):
        w = jnp.where(mask, w_ref[g], 0.0).astype(BF16)
        bias = b_ref[:, g:g + 1]
        for t in range(tm // SGU_CHUNK):
            rs = slice(t * SGU_CHUNK, (t + 1) * SGU_CHUNK)
            cs = slice(g * gd, (g + 1) * gd)
            z = jnp.dot(w, v_ref[rs, cs], preferred_element_type=F32) + bias
            o_ref[rs, cs] = (u_ref[rs, cs].astype(F32) * z).astype(BF16)


def _sgu_tables(w_s, b_s, seq_len):
    chunk = min(SGU_CHUNK, seq_len)
    reps = SGU_CHUNK // chunk
    if reps == 1:
        return w_s, jnp.swapaxes(b_s, 1, 2)
    w_t = jnp.tile(w_s[:, :, :chunk, :chunk], (1, 1, reps, reps))
    b_t = jnp.tile(jnp.swapaxes(b_s[:, :, :chunk], 1, 2), (1, reps, 1))
    return w_t, b_t


def _sgu(proj, w_t, b_t, l, depth, seq_len, tm, emit_v, prev):
    rows = proj.shape[1]
    kern = functools.partial(_sgu_kernel, tm=tm, chunk=min(SGU_CHUNK, seq_len))
    in_specs = [pl.BlockSpec((None, tm, IN_TILE), lambda i: (J_U, i, 0)),
                pl.BlockSpec((None, tm, IN_TILE), lambda i: (J_V, i, 0)),
                pl.BlockSpec((None, SGU_GROUPS, SGU_CHUNK, SGU_CHUNK), lambda i: (l, 0, 0, 0)),
                pl.BlockSpec((None, SGU_CHUNK, SGU_GROUPS), lambda i: (l, 0, 0))]
    out_specs = [pl.BlockSpec((tm, IN_TILE), lambda i: (i, 0))]
    out_shape = [jax.ShapeDtypeStruct((rows, IN_TILE), BF16)]
    cspecs, cargs, aliases = [], [], {}
    if emit_v:
        kern, cspecs, cargs, aliases = _stacked(kern, 4, prev, depth, l)
        out_specs.append(_stacked_spec(prev, depth, l, (tm, IN_TILE), lambda i: (i, 0)))
        out_shape.append(jax.ShapeDtypeStruct((depth, rows, IN_TILE), F32))
    out = pl.pallas_call(
        kern, grid=(rows // tm,),
        in_specs=in_specs + cspecs, out_specs=out_specs, out_shape=out_shape,
        input_output_aliases=aliases,
        compiler_params=_params(1), name="sgu",
    )(proj, proj, w_t, b_t, *cargs)
    return (out[0], out[1]) if emit_v else (out[0], None)


RET_CHUNKS_PER_STEP = 2


def _log_gamma(h):
    return math.log(1.0 - 2.0 ** (-5.0 - h))


def _ret_decays(h, c):
    lg = _log_gamma(h)
    ri = lax.broadcasted_iota(jnp.int32, (c, c), 0)
    ci = lax.broadcasted_iota(jnp.int32, (c, c), 1)
    diff = (ri - ci).astype(F32)
    decay = jnp.where(ri >= ci, jnp.exp(lg * jnp.maximum(diff, 0.0)), 0.0)
    idx = lax.broadcasted_iota(jnp.int32, (c, 1), 0).astype(F32)
    return decay, jnp.exp(lg * (idx + 1.0)), jnp.exp(lg * (c - 1.0 - idx))


def _ret_chunk(qh, kh, vh, grh, gnh, s, h, c, decays):
    decay, q_dec, k_dec = decays
    scores = lax.dot_general(qh, kh, (((1,), (1,)), ((), ())), preferred_element_type=F32) * decay
    inner = jnp.dot(scores.astype(BF16), vh, preferred_element_type=F32)
    cross = jnp.dot(qh, s.astype(BF16), preferred_element_type=F32) * q_dec
    kd = (kh.astype(F32) * k_dec).astype(BF16)
    s_new = (s * math.exp(_log_gamma(h) * c)
             + lax.dot_general(kd, vh, (((0,), (0,)), ((), ())), preferred_element_type=F32))
    o = inner + cross
    mu = jnp.mean(o, axis=-1, keepdims=True)
    var = jnp.mean(jnp.square(o - mu), axis=-1, keepdims=True)
    y = (o - mu) * lax.rsqrt(var + EPS) * gnh
    return grh * y, s_new


def _ret_prompt_kernel(q_ref, k_ref, v_ref, gr_ref, gn_ref, o_ref, so_ref, s_ref, dec_ref, *, hd):
    n = pl.program_id(1)

    @pl.when(n == 0)
    def _():
        s_ref[...] = jnp.zeros_like(s_ref)

    @pl.when((n == 0) & (pl.program_id(0) == 0))
    def _():
        for h in range(RET_HEADS):
            dec_ref[h] = _ret_decays(h, RET_CHUNK)[0]

    for h in range(RET_HEADS):
        cs = slice(h * hd, (h + 1) * hd)
        _, q_dec, k_dec = _ret_decays(h, RET_CHUNK)
        s = s_ref[h]
        for ci in range(q_ref.shape[0] // RET_CHUNK):
            rs = slice(ci * RET_CHUNK, (ci + 1) * RET_CHUNK)
            y, s = _ret_chunk(q_ref[rs, cs], k_ref[rs, cs], v_ref[rs, cs], gr_ref[rs, cs].astype(F32),
                              gn_ref[:, cs], s, h, RET_CHUNK, (dec_ref[h], q_dec, k_dec))
            o_ref[rs, cs] = y.astype(BF16)
        s_ref[h] = s

    @pl.when(n == pl.num_programs(1) - 1)
    def _():
        so_ref[...] = s_ref[...]


def _ret_prompt(proj, gn_g, l, depth, nseq, seq_len, prev):
    hd = IN_TILE // RET_HEADS
    tr = RET_CHUNKS_PER_STEP * RET_CHUNK
    nc = seq_len // tr

    def col(jc):
        return pl.BlockSpec((None, tr, IN_TILE), lambda b, n: (jc, b * nc + n, 0))
    kern, cspecs, cargs, aliases = _stacked(functools.partial(_ret_prompt_kernel, hd=hd), 5, prev, depth, l)
    return pl.pallas_call(
        kern, grid=(nseq, nc),
        in_specs=[col(J_Q), col(J_K), col(J_VR), col(J_GR),
                  pl.BlockSpec((None, 1, IN_TILE), lambda b, n: (l, 0, 0))] + cspecs,
        out_specs=[pl.BlockSpec((tr, IN_TILE), lambda b, n: (b * nc + n, 0)),
                   _stacked_spec(prev, depth, l, (None, RET_HEADS, hd, hd), lambda b, n: (b, 0, 0, 0))],
        out_shape=[jax.ShapeDtypeStruct((nseq * seq_len, IN_TILE), BF16),
                   jax.ShapeDtypeStruct((depth, nseq, RET_HEADS, hd, hd), F32)],
        scratch_shapes=[pltpu.VMEM((RET_HEADS, hd, hd), F32),
                        pltpu.VMEM((RET_HEADS, RET_CHUNK, RET_CHUNK), F32)],
        input_output_aliases=aliases,
        compiler_params=_params(2), name="ret_prompt",
    )(proj, proj, proj, proj, _vec3(gn_g), *cargs)


def _ret_sample_kernel(q_ref, k_ref, v_ref, gr_ref, gn_ref, s_ref, o_ref, so_ref, y_ref, *, hd, bs, seq_len):
    q = q_ref[...].astype(F32)
    k = k_ref[...].astype(F32)
    v = v_ref[...].astype(F32)
    gr = gr_ref[...].astype(F32)
    for h in range(RET_HEADS):
        cs = slice(h * hd, (h + 1) * hd)
        decays = _ret_decays(h, seq_len)
        for b in range(bs):
            rs = slice(b * seq_len, (b + 1) * seq_len)
            y, s_new = _ret_chunk(q[rs, cs].astype(BF16), k[rs, cs].astype(BF16), v[rs, cs].astype(BF16),
                                  gr[rs, cs], gn_ref[:, cs], s_ref[b, h], h, seq_len, decays)
            y_ref[rs, cs] = y
            so_ref[b, h] = s_new
    o_ref[...] = y_ref[...].astype(BF16)


def _ret_sample(proj, gn_g, state, l, seq_len, bs, prev):
    depth, nseq = state.shape[:2]
    hd = IN_TILE // RET_HEADS

    def col(jc):
        return pl.BlockSpec((None, bs * seq_len, IN_TILE), lambda i: (jc, i, 0))
    kern, cspecs, cargs, aliases = _stacked(
        functools.partial(_ret_sample_kernel, hd=hd, bs=bs, seq_len=seq_len), 6, prev, depth, l)
    return pl.pallas_call(
        kern, grid=(nseq // bs,),
        in_specs=[col(J_Q), col(J_K), col(J_VR), col(J_GR),
                  pl.BlockSpec((None, 1, IN_TILE), lambda i: (l, 0, 0)),
                  pl.BlockSpec((None, bs, RET_HEADS, hd, hd), lambda i: (l, i, 0, 0, 0))] + cspecs,
        out_specs=[pl.BlockSpec((bs * seq_len, IN_TILE), lambda i: (i, 0)),
                   _stacked_spec(prev, depth, l, (bs, RET_HEADS, hd, hd), lambda i: (i, 0, 0, 0))],
        out_shape=[jax.ShapeDtypeStruct((nseq * seq_len, IN_TILE), BF16),
                   jax.ShapeDtypeStruct(state.shape, F32)],
        scratch_shapes=[pltpu.VMEM((bs * seq_len, IN_TILE), F32)],
        input_output_aliases=aliases,
        compiler_params=_params(1), name="ret_sample",
    )(proj, proj, proj, proj, _vec3(gn_g), state, *cargs)


def _merge_kernel(ya_ref, yb_ref, ga_ref, gb_ref, wa_ref, wb_ref, o_ref, wab_ref, wbb_ref, *, rc):
    @pl.when(pl.program_id(1) == 0)
    def _():
        wab_ref[...] = wa_ref[...].astype(BF16)
        wbb_ref[...] = wb_ref[...].astype(BF16)
    for c in range(o_ref.shape[0] // rc):
        rows = slice(c * rc, (c + 1) * rc)
        a = jnp.dot(ya_ref[rows, :], wab_ref[...], preferred_element_type=F32)
        b = jnp.dot(yb_ref[rows, :], wbb_ref[...], preferred_element_type=F32)
        o_ref[rows, :] = (ga_ref[rows, :].astype(F32) * a + gb_ref[rows, :].astype(F32) * b).astype(BF16)


def _merge(ya, yb, proj, w_a, w_b, l, tm, rc):
    rows, k = ya.shape
    n = w_a.shape[2]
    tn = IN_TILE
    row = pl.BlockSpec((tm, k), lambda j, i: (i, 0))
    wsp = pl.BlockSpec((None, k, tn), lambda j, i: (l, 0, j))
    return pl.pallas_call(
        functools.partial(_merge_kernel, rc=rc), grid=(n // tn, rows // tm),
        in_specs=[row, row,
                  pl.BlockSpec((None, tm, tn), lambda j, i: (J_GA + j, i, 0)),
                  pl.BlockSpec((None, tm, tn), lambda j, i: (J_GB + j, i, 0)),
                  wsp, wsp],
        out_specs=pl.BlockSpec((None, tm, tn), lambda j, i: (j, i, 0)),
        out_shape=jax.ShapeDtypeStruct((n // tn, rows, tn), BF16),
        scratch_shapes=[pltpu.VMEM((k, tn), BF16), pltpu.VMEM((k, tn), BF16)],
        compiler_params=_params(2), name="merge",
    )(ya, yb, proj, proj, w_a, w_b)


def _ffn_up_kernel(h_ref, wg_ref, wu_ref, cw_ref, cb_ref, *rest, tm, rc, seq_len, has_state):
    if has_state:
        st_ref, y_ref, so_ref, wgb_ref, wub_ref, gbuf_ref = rest
    else:
        y_ref, so_ref, wgb_ref, wub_ref, gbuf_ref = rest
    i = pl.program_id(1)
    pad = SUBLANES
    keep = CONV_W - 1

    @pl.when(i == 0)
    def _():
        wgb_ref[...] = wg_ref[...].astype(BF16)
        wub_ref[...] = wu_ref[...].astype(BF16)

    tn = y_ref.shape[1]
    if has_state:
        gbuf_ref[0:pad, :] = jnp.zeros((pad, tn), F32)
    else:
        @pl.when(((i * tm) & (seq_len - 1)) == 0)
        def _():
            gbuf_ref[0:pad, :] = jnp.zeros((pad, tn), F32)
    for c in range(tm // rc):
        r0 = c * rc
        h = h_ref[r0:r0 + rc, :]
        g = jnp.dot(h, wgb_ref[...], preferred_element_type=F32)
        up = jnp.dot(h, wub_ref[...], preferred_element_type=F32)
        gbuf_ref[pad + r0:pad + r0 + rc, :] = g
        g1 = gbuf_ref[pad - 1 + r0:pad - 1 + r0 + rc, :]
        g2 = gbuf_ref[pad - 2 + r0:pad - 2 + r0 + rc, :]
        if has_state:
            bs = rc // seq_len
            st = st_ref[c * bs:(c + 1) * bs]
            s0 = jnp.broadcast_to(st[:, 0:1, :], (bs, seq_len, tn)).reshape(rc, tn)
            s1 = jnp.broadcast_to(st[:, 1:2, :], (bs, seq_len, tn)).reshape(rc, tn)
            t = lax.broadcasted_iota(jnp.int32, (rc, tn), 0) & (seq_len - 1)
            g1 = jnp.where(t == 0, s1, g1)
            g2 = jnp.where(t == 0, s0, jnp.where(t == 1, s1, g2))
            so_ref[c * bs:(c + 1) * bs] = g.reshape(bs, seq_len, tn)[:, seq_len - keep:, :]
        conv = g2 * cw_ref[0:1, :] + g1 * cw_ref[1:2, :] + g * cw_ref[2:3, :] + cb_ref[...]
        y_ref[r0:r0 + rc, :] = (jax.nn.gelu(conv) * up).astype(BF16)
    if not has_state:
        last = gbuf_ref[pad + tm - keep:pad + tm, :]
        gbuf_ref[pad - keep:pad, :] = last
        so_ref[...] = last


def _ffn_up(h, w_gate, w_up, conv_w, conv_b, l, depth, nseq, seq_len, tm, rc, tn, state, prev):
    rows, k = h.shape
    n = w_gate.shape[2]
    keep = CONV_W - 1
    has_state = state is not None
    assert seq_len & (seq_len - 1) == 0 and seq_len >= CONV_W
    kern = functools.partial(_ffn_up_kernel, tm=tm, rc=rc, seq_len=seq_len, has_state=has_state)
    wsp = pl.BlockSpec((None, k, tn), lambda j, i: (l, 0, j))
    in_specs = [pl.BlockSpec((tm, k), lambda j, i: (i, 0)), wsp, wsp,
                pl.BlockSpec((None, CONV_W, tn), lambda j, i: (l, 0, j)),
                pl.BlockSpec((None, 1, tn), lambda j, i: (l, 0, j))]
    args = [h, w_gate, w_up, conv_w, _vec3(conv_b)]
    if has_state:
        bs = tm // seq_len
        in_specs.append(pl.BlockSpec((None, bs, keep, tn), lambda j, i: (l, i, 0, j)))
        args.append(state)
        so_spec = _stacked_spec(prev, depth, l, (bs, keep, tn), lambda j, i: (i, 0, j))
    else:
        tps = seq_len // tm
        so_spec = _stacked_spec(prev, depth, l, (None, keep, tn), lambda j, i: (i // tps, 0, j))
    kern, cspecs, cargs, aliases = _stacked(kern, len(args), prev, depth, l)
    return pl.pallas_call(
        kern, grid=(n // tn, rows // tm),
        in_specs=in_specs + cspecs,
        out_specs=[pl.BlockSpec((None, tm, tn), lambda j, i: (j, i, 0)), so_spec],
        out_shape=[jax.ShapeDtypeStruct((n // tn, rows, tn), BF16),
                   jax.ShapeDtypeStruct((depth, nseq, keep, n), F32)],
        scratch_shapes=[pltpu.VMEM((k, tn), BF16), pltpu.VMEM((k, tn), BF16),
                        pltpu.VMEM((tm + SUBLANES, tn), F32)],
        input_output_aliases=aliases,
        compiler_params=_params(2), name="ffn_up",
    )(*args, *cargs)


def _layer(l, depth, x, h, mod, w, cfg, s_ret, conv_prev, outs):
    nseq, seq_len, tm, tm_post = cfg["nseq"], cfg["seq_len"], cfg["tm"], cfg["tm_post"]
    ret_out, conv_out, v_out = outs

    proj = _inproj(h, w["w_in"], w["sgu_ln_g"], w["sgu_ln_b"], l, cfg["tm_in"], cfg["rc_in"], seq_len,
                   cfg["pos0"])
    ya, v_out = _sgu(proj, cfg["sgu_w"], cfg["sgu_b"], l, depth, seq_len, tm, cfg["emit_v"], v_out)
    if s_ret is None:
        yb, ret_out = _ret_prompt(proj, w["ret_gn_g"], l, depth, nseq, seq_len, ret_out)
    else:
        yb, ret_out = _ret_sample(proj, w["ret_gn_g"], s_ret, l, seq_len, cfg["ret_bs"], ret_out)
    merged = _merge(ya, yb, proj, w["w_branch_a"], w["w_branch_b"], l, tm, cfg["rc_merge"])
    x, h2 = _matmul_post(merged, w["w_out"], x, w["norm_post1"], mod, l, K_G1, seq_len, cfg["tm_out"],
                         (w["norm_pre2"], l, K_SC2, K_SH2), cfg["rc_out"])

    y, conv_out = _ffn_up(h2, w["ffn_w_gate"], w["ffn_w_up"], w["ffn_conv_w"], w["ffn_conv_b"], l, depth,
                          nseq, seq_len, cfg["tm_ffn"], cfg["rc_ffn"], FFN_TILE, conv_prev, conv_out)
    nxt = None if l == depth - 1 else (w["norm_pre1"], l + 1, K_SC1, K_SH1)
    x, h_next = _matmul_post(y, w["ffn_w_down_bf16"], x, w["norm_post2"], mod, l, K_G2, seq_len, tm_post, nxt,
                             cfg["rc_post"])
    return x, h_next, (ret_out, conv_out, v_out)


def kernel(x_prompt, x_sample, state_ret, state_conv, c_prompt, c_sample, w_ada, b_ada, norm_pre1, norm_post1, norm_pre2, norm_post2, w_in, sgu_w_s, sgu_b_s, sgu_ln_g, sgu_ln_b, ret_gn_g, w_branch_a, w_branch_b, w_out, ffn_w_gate, ffn_w_up, ffn_conv_w, ffn_conv_b, ffn_w_down):
    w = dict(norm_pre1=norm_pre1, norm_post1=norm_post1, norm_pre2=norm_pre2, norm_post2=norm_post2, w_in=w_in,
             sgu_ln_g=sgu_ln_g, sgu_ln_b=sgu_ln_b, ret_gn_g=ret_gn_g, w_branch_a=w_branch_a,
             w_branch_b=w_branch_b, w_out=w_out, ffn_w_gate=ffn_w_gate, ffn_w_up=ffn_w_up,
             ffn_conv_w=ffn_conv_w, ffn_conv_b=ffn_conv_b, ffn_w_down_bf16=_cast_bf16(ffn_w_down))
    depth = w_ada.shape[0]
    nb, seq, d = x_prompt.shape
    ndb, dseq, _ = x_sample.shape

    mod_p, mod_s = _modulation(c_prompt, c_sample, w_ada, b_ada)

    sgu_wp, sgu_bp = _sgu_tables(sgu_w_s, sgu_b_s, seq)
    sgu_ws, sgu_bs = _sgu_tables(sgu_w_s, sgu_b_s, dseq)
    cfg_p = dict(nseq=nb, seq_len=seq, tm=1024, tm_in=2048, rc_in=256, rc_merge=512, tm_ffn=2048, rc_ffn=128,
                 tm_out=512, rc_out=256, tm_post=256, rc_post=256, pos0=0, sgu_w=sgu_wp, sgu_b=sgu_bp,
                 emit_v=False)
    cfg_s = dict(nseq=ndb, seq_len=dseq, tm=1024, tm_in=1024, rc_in=256, rc_merge=256, tm_ffn=1024, rc_ffn=256,
                 tm_out=512, rc_out=256, tm_post=256, rc_post=256, pos0=PAST_LEN, ret_bs=8, sgu_w=sgu_ws,
                 sgu_b=sgu_bs, emit_v=True)

    xp = x_prompt.reshape(nb * seq, d)
    xs = x_sample.reshape(ndb * dseq, d)
    hp = _prenorm(xp, norm_pre1, mod_p, 0, K_SC1, K_SH1, seq, 512)
    hs = _prenorm(xs, norm_pre1, mod_s, 0, K_SC1, K_SH1, dseq, 512)
    outs_p = outs_s = (None, None, None)
    for l in range(depth):
        xp, hp, outs_p = _layer(l, depth, xp, hp, mod_p, w, cfg_p, None, None, outs_p)
        xs, hs, outs_s = _layer(l, depth, xs, hs, mod_s, w, cfg_s, state_ret, state_conv, outs_s)
    ret_p, conv_p, _ = outs_p
    ret_s, conv_s, v_s = outs_s
    return (xp.reshape(nb, seq, d), xs.reshape(ndb, dseq, d), ret_p, ret_s, conv_p, conv_s,
            v_s.reshape(depth, ndb, dseq, -1))
```
